```python
import math
import jax, jax.numpy as jnp
from jax import lax
import numpy as np

D_MODEL = 1024
BATCH = 4
SEQ = 4096
DEPTH = 2

CHUNK = 64
RMS_EPS = 1e-6

SSM_GROUP = 16
SSM_WIDTH = D_MODEL // 4
SSM_GROUPS = SSM_WIDTH // SSM_GROUP
SSM_STATE = 64
GMLP_HEADS = 4
GMLP_HEAD_DIM = 64
GMLP_WIDTH = GMLP_HEADS * GMLP_HEAD_DIM
GMLP_BLOCK = 128
ATT_HEADS = 8
ATT_HEAD_DIM = 64
ATT_WIDTH = ATT_HEADS * ATT_HEAD_DIM
KV_RANK = 128
IDX_HEADS = 4
IDX_DIM = 64
TOPK_MAX = 256
Q_BLOCK = 128

MIX_WIDTH = SSM_WIDTH + GMLP_WIDTH + ATT_WIDTH

IN_SIZES = (SSM_WIDTH, GMLP_WIDTH, GMLP_WIDTH, ATT_WIDTH, KV_RANK, IDX_HEADS * IDX_DIM, IDX_DIM, IDX_HEADS)
IN_WIDTH = sum(IN_SIZES)
IN_SPLITS = [sum(IN_SIZES[:i + 1]) for i in range(len(IN_SIZES) - 1)]

PEER_HEADS = 8
PEER_KEY_DIM = 128
PEER_N_KEYS = 128
PEER_N_EXPERTS = PEER_N_KEYS * PEER_N_KEYS
PEER_TOPK = 16
PEER_TOKEN_BLOCK = 128

kernel_name = "hybrid_s5_gmlp_dsa_peer_adaln"


def rms_norm(x, g):
    xf = x.astype(jnp.float32)
    y = xf * lax.rsqrt(jnp.mean(xf * xf, axis=-1, keepdims=True) + RMS_EPS)
    return (y * g.astype(jnp.float32)).astype(x.dtype)


def ssm_mixer(u, a_re_log, a_im, b_re, b_im, c_re, c_im, d_skip, log_dt, w_glu, b_glu):
    f32 = jnp.float32
    Bsz, L, _ = u.shape
    uf = u.astype(f32)
    ug = uf.reshape(Bsz, L, SSM_GROUPS, SSM_GROUP)
    lam_re = -jnp.exp(a_re_log.astype(f32))
    lam_im = a_im.astype(f32)
    dt = jnp.exp(log_dt.astype(f32))[:, None]
    mag = jnp.exp(lam_re * dt)
    abar_re = mag * jnp.cos(lam_im * dt)
    abar_im = mag * jnp.sin(lam_im * dt)
    den = lam_re * lam_re + lam_im * lam_im
    p = abar_re - 1.0
    q = abar_im
    f_re = (p * lam_re + q * lam_im) / den
    f_im = (q * lam_re - p * lam_im) / den
    br = b_re.astype(f32)
    bi = b_im.astype(f32)
    bb_re = f_re[..., None] * br - f_im[..., None] * bi
    bb_im = f_re[..., None] * bi + f_im[..., None] * br
    bu_re = jnp.einsum('gph,blgh->blgp', bb_re, ug)
    bu_im = jnp.einsum('gph,blgh->blgp', bb_im, ug)
    a_re_t = jnp.broadcast_to(abar_re, bu_re.shape)
    a_im_t = jnp.broadcast_to(abar_im, bu_re.shape)

    def combine(left, right):
        a1r, a1i, b1r, b1i = left
        a2r, a2i, b2r, b2i = right
        return (a1r * a2r - a1i * a2i,
                a1r * a2i + a1i * a2r,
                a2r * b1r - a2i * b1i + b2r,
                a2r * b1i + a2i * b1r + b2i)

    _, _, s_re, s_im = lax.associative_scan(combine, (a_re_t, a_im_t, bu_re, bu_im), axis=1)
    y = (jnp.einsum('ghp,blgp->blgh', c_re.astype(f32), s_re)
         - jnp.einsum('ghp,blgp->blgh', c_im.astype(f32), s_im))
    y = y.reshape(Bsz, L, SSM_WIDTH) + d_skip.astype(f32) * uf
    yg = jax.nn.gelu(y)
    out = yg * jax.nn.sigmoid(yg @ w_glu.astype(f32) + b_glu.astype(f32))
    return out.astype(u.dtype)


def gmlp_mixer(u, v, w_sp, b_sp):
    Bsz, L, _ = u.shape
    nb = L // GMLP_BLOCK
    vf = v.reshape(Bsz, nb, GMLP_BLOCK, GMLP_HEADS, GMLP_HEAD_DIM).astype(jnp.float32)
    mu = jnp.mean(vf, axis=-1, keepdims=True)
    var = jnp.mean(jnp.square(vf - mu), axis=-1, keepdims=True)
    vn = ((vf - mu) * lax.rsqrt(var + RMS_EPS)).astype(v.dtype)
    chunk_id = jnp.arange(GMLP_BLOCK) // CHUNK
    mask = chunk_id[:, None] >= chunk_id[None, :]
    w = jnp.where(mask[None], w_sp, 0.0)
    mixed = jnp.einsum('hts,bnshd->bnthd', w, vn) + b_sp.T[None, None, :, :, None]
    out = u.reshape(Bsz, nb, GMLP_BLOCK, GMLP_HEADS, GMLP_HEAD_DIM) * mixed
    return out.reshape(Bsz, L, GMLP_WIDTH)


def dsa_mixer(q, c_kv, q_idx, k_idx, w_idx, w_uk, w_uv):
    f32 = jnp.float32
    Bsz, L = q.shape[0], q.shape[1]
    topk = min(TOPK_MAX, L // 4)
    nqb = L // Q_BLOCK
    slopes = jnp.exp2(-8.0 * (jnp.arange(ATT_HEADS, dtype=f32) + 1.0) / ATT_HEADS)
    q_abs = jnp.einsum('blhd,hdr->blhr', q, w_uk) * (ATT_HEAD_DIM ** -0.5)
    w_scaled = w_idx * (IDX_HEADS ** -0.5)
    key_chunk = jnp.arange(L) // CHUNK
    b_ix = jnp.arange(Bsz)[:, None, None]

    def to_blocks(a):
        return jnp.moveaxis(a.reshape((Bsz, nqb, Q_BLOCK) + a.shape[2:]), 1, 0)

    def block_fn(args):
        qa, qi, wi, qpos = args
        s = jnp.einsum('bthd,bsd->bths', qi, k_idx) * (IDX_DIM ** -0.5)
        score = jnp.einsum('bth,bths->bts', wi, jax.nn.relu(s)).astype(f32)
        adm = key_chunk[None, :] <= (qpos // CHUNK)[:, None]
        score = jnp.where(adm[None], score, -jnp.inf)
        sel_score, sel = lax.top_k(score, topk)
        valid = jnp.isfinite(sel_score)
        kv = c_kv[b_ix, sel]
        logits = jnp.einsum('bthr,btkr->bthk', qa, kv).astype(f32)
        dist = jnp.abs(qpos[None, :, None] - sel).astype(f32)
        logits = logits - slopes[None, None, :, None] * dist[:, :, None, :]
        logits = jnp.where(valid[:, :, None, :], logits, -jnp.inf)
        probs = jax.nn.softmax(logits, axis=-1).astype(kv.dtype)
        return jnp.einsum('bthk,btkr->bthr', probs, kv)

    qpos_blocks = jnp.arange(L).reshape(nqb, Q_BLOCK)
    o = lax.map(block_fn, (to_blocks(q_abs), to_blocks(q_idx), to_blocks(w_scaled), qpos_blocks))
    o = jnp.moveaxis(o, 0, 1).reshape(Bsz, L, ATT_HEADS, KV_RANK)
    out = jnp.einsum('blhr,hrd->blhd', o, w_uv)
    return out.reshape(Bsz, L, ATT_WIDTH)


def peer_ffn(h, w_q, sub_k1, sub_k2, peer_u, peer_v):
    f32 = jnp.float32
    Bsz, L, D = h.shape
    half = PEER_KEY_DIM // 2
    q = (h @ w_q).reshape(Bsz, L, PEER_HEADS, 2, half)
    s1 = jnp.einsum('blhd,nd->blhn', q[..., 0, :], sub_k1).astype(f32)
    s2 = jnp.einsum('blhd,nd->blhn', q[..., 1, :], sub_k2).astype(f32)
    v1, i1 = lax.top_k(s1, PEER_TOPK)
    v2, i2 = lax.top_k(s2, PEER_TOPK)
    ncand = PEER_TOPK * PEER_TOPK
    cand = (v1[..., :, None] + v2[..., None, :]).reshape(Bsz, L, PEER_HEADS, ncand)
    cand_idx = (i1[..., :, None] * PEER_N_KEYS + i2[..., None, :]).reshape(Bsz, L, PEER_HEADS, ncand)
    top_s, pos = lax.top_k(cand, PEER_TOPK)
    experts = jnp.take_along_axis(cand_idx, pos, axis=-1)
    gates = jax.nn.softmax(top_s, axis=-1).astype(h.dtype)
    n_tok = Bsz * L
    nb = n_tok // PEER_TOKEN_BLOCK
    hk = PEER_HEADS * PEER_TOPK
    hb = h.reshape(nb, PEER_TOKEN_BLOCK, D)
    eb = experts.reshape(nb, PEER_TOKEN_BLOCK, hk)
    gb = gates.reshape(nb, PEER_TOKEN_BLOCK, hk)

    def block_fn(args):
        hx, e, g = args
        act = jax.nn.gelu(jnp.einsum('tkd,td->tk', peer_u[e], hx))
        return jnp.einsum('tk,tkd->td', g * act, peer_v[e])

    out = lax.map(block_fn, (hb, eb, gb))
    return out.reshape(Bsz, L, D)


def setup_inputs(seed: int = 0) -> dict:
    key = jax.random.key(seed)
    ks = iter(jax.random.split(key, 40))
    D = D_MODEL
    NL = DEPTH

    def nrm(shape, scale):
        return jax.random.normal(next(ks), shape, jnp.float32) * scale

    x = nrm((BATCH, SEQ, D), 1.0)
    c = nrm((BATCH, D), 1.0)
    norm1_g = 1.0 + nrm((NL, D), 0.02)
    norm2_g = 1.0 + nrm((NL, D), 0.02)
    w_mod = nrm((NL, D, 6 * D), 0.5 * D ** -0.5)
    b_mod = nrm((NL, 6 * D), 0.02)
    w_in = nrm((NL, D, IN_WIDTH), D ** -0.5)
    ssm_a_re_log = math.log(0.5) + nrm((NL, SSM_GROUPS, SSM_STATE), 0.01)
    ssm_a_im = math.pi * jnp.arange(SSM_STATE, dtype=jnp.float32)[None, None, :] + nrm((NL, SSM_GROUPS, SSM_STATE), 0.01)
    ssm_b_re = nrm((NL, SSM_GROUPS, SSM_STATE, SSM_GROUP), (2 * SSM_GROUP) ** -0.5)
    ssm_b_im = nrm((NL, SSM_GROUPS, SSM_STATE, SSM_GROUP), (2 * SSM_GROUP) ** -0.5)
    ssm_c_re = nrm((NL, SSM_GROUPS, SSM_GROUP, SSM_STATE), 2.0 * SSM_STATE ** -0.5)
    ssm_c_im = nrm((NL, SSM_GROUPS, SSM_GROUP, SSM_STATE), 2.0 * SSM_STATE ** -0.5)
    ssm_d = nrm((NL, SSM_WIDTH), 0.5)
    ssm_log_dt = jax.random.uniform(next(ks), (NL, SSM_GROUPS), jnp.float32, math.log(1e-3), math.log(1e-1))
    ssm_w_glu = nrm((NL, SSM_WIDTH, SSM_WIDTH), SSM_WIDTH ** -0.5)
    ssm_b_glu = nrm((NL, SSM_WIDTH), 0.02)
    gmlp_w_sp = nrm((NL, GMLP_HEADS, GMLP_BLOCK, GMLP_BLOCK), 0.05)
    gmlp_b_sp = 1.0 + nrm((NL, GMLP_HEADS, GMLP_BLOCK), 0.02)
    kv_norm_g = 1.0 + nrm((NL, KV_RANK), 0.02)
    w_uk = nrm((NL, ATT_HEADS, ATT_HEAD_DIM, KV_RANK), KV_RANK ** -0.5)
    w_uv = nrm((NL, ATT_HEADS, KV_RANK, ATT_HEAD_DIM), KV_RANK ** -0.5)
    w_out = nrm((NL, MIX_WIDTH, D), MIX_WIDTH ** -0.5)
    peer_w_q = nrm((NL, D, PEER_HEADS * PEER_KEY_DIM), D ** -0.5)
    peer_k1 = nrm((NL, PEER_N_KEYS, PEER_KEY_DIM // 2), (PEER_KEY_DIM // 2) ** -0.5)
    peer_k2 = nrm((NL, PEER_N_KEYS, PEER_KEY_DIM // 2), (PEER_KEY_DIM // 2) ** -0.5)
    peer_u = nrm((NL, PEER_N_EXPERTS, D), D ** -0.5)
    peer_v = nrm((NL, PEER_N_EXPERTS, D), 0.3)
    final_g = 1.0 + nrm((D,), 0.02)
    return {"x": x, "c": c, "norm1_g": norm1_g, "norm2_g": norm2_g, "w_mod": w_mod, "b_mod": b_mod,
            "w_in": w_in, "ssm_a_re_log": ssm_a_re_log, "ssm_a_im": ssm_a_im, "ssm_b_re": ssm_b_re,
            "ssm_b_im": ssm_b_im, "ssm_c_re": ssm_c_re, "ssm_c_im": ssm_c_im, "ssm_d": ssm_d,
            "ssm_log_dt": ssm_log_dt, "ssm_w_glu": ssm_w_glu, "ssm_b_glu": ssm_b_glu,
            "gmlp_w_sp": gmlp_w_sp, "gmlp_b_sp": gmlp_b_sp, "kv_norm_g": kv_norm_g, "w_uk": w_uk,
            "w_uv": w_uv, "w_out": w_out, "peer_w_q": peer_w_q, "peer_k1": peer_k1, "peer_k2": peer_k2,
            "peer_u": peer_u, "peer_v": peer_v, "final_g": final_g}


def reference(x, c, norm1_g, norm2_g, w_mod, b_mod, w_in, ssm_a_re_log, ssm_a_im, ssm_b_re, ssm_b_im,
              ssm_c_re, ssm_c_im, ssm_d, ssm_log_dt, ssm_w_glu, ssm_b_glu, gmlp_w_sp, gmlp_b_sp,
              kv_norm_g, w_uk, w_uv, w_out, peer_w_q, peer_k1, peer_k2, peer_u, peer_v, final_g):
    Bsz, L, _ = x.shape
    c_act = jax.nn.silu(c)
    for l in range(DEPTH):
        mod = c_act @ w_mod[l] + b_mod[l]
        sh1, sc1, g1, sh2, sc2, g2 = jnp.split(mod[:, None, :], 6, axis=-1)
        h = rms_norm(x, norm1_g[l]) * (1.0 + sc1) + sh1
        proj = h @ w_in[l]
        u_a, u_b, v_b, q, ckv, qi, ki, wi = jnp.split(proj, IN_SPLITS, axis=-1)
        y_a = ssm_mixer(u_a, ssm_a_re_log[l], ssm_a_im[l], ssm_b_re[l], ssm_b_im[l], ssm_c_re[l],
                        ssm_c_im[l], ssm_d[l], ssm_log_dt[l], ssm_w_glu[l], ssm_b_glu[l])
        y_b = gmlp_mixer(u_b, v_b, gmlp_w_sp[l], gmlp_b_sp[l])
        y_c = dsa_mixer(q.reshape(Bsz, L, ATT_HEADS, ATT_HEAD_DIM), rms_norm(ckv, kv_norm_g[l]),
                        qi.reshape(Bsz, L, IDX_HEADS, IDX_DIM), ki, wi, w_uk[l], w_uv[l])
        mix = jnp.concatenate([y_a, y_b, y_c], axis=-1)
        x = x + g1 * (mix @ w_out[l])
        h = rms_norm(x, norm2_g[l]) * (1.0 + sc2) + sh2
        x = x + g2 * peer_ffn(h, peer_w_q[l], peer_k1[l], peer_k2[l], peer_u[l], peer_v[l])
    return rms_norm(x, final_g)
```

```python
import functools
import math
import jax, jax.numpy as jnp
from jax import lax
import numpy as np
from jax.experimental import pallas as pl
from jax.experimental.pallas import tpu as pltpu

D_MODEL = 1024
BATCH = 4
SEQ = 4096
DEPTH = 2

CHUNK = 64
RMS_EPS = 1e-6

SSM_GROUP = 16
SSM_WIDTH = D_MODEL // 4
SSM_GROUPS = SSM_WIDTH // SSM_GROUP
SSM_STATE = 64
GMLP_HEADS = 4
GMLP_HEAD_DIM = 64
GMLP_WIDTH = GMLP_HEADS * GMLP_HEAD_DIM
GMLP_BLOCK = 128
ATT_HEADS = 8
ATT_HEAD_DIM = 64
ATT_WIDTH = ATT_HEADS * ATT_HEAD_DIM
KV_RANK = 128
IDX_HEADS = 4
IDX_DIM = 64
TOPK_MAX = 256
Q_BLOCK = 128

MIX_WIDTH = SSM_WIDTH + GMLP_WIDTH + ATT_WIDTH

IN_SIZES = (SSM_WIDTH, GMLP_WIDTH, GMLP_WIDTH, ATT_WIDTH, KV_RANK, IDX_HEADS * IDX_DIM, IDX_DIM, IDX_HEADS)
IN_WIDTH = sum(IN_SIZES)
IN_SPLITS = [sum(IN_SIZES[:i + 1]) for i in range(len(IN_SIZES) - 1)]

PEER_HEADS = 8
PEER_KEY_DIM = 128
PEER_N_KEYS = 128
PEER_N_EXPERTS = PEER_N_KEYS * PEER_N_KEYS
PEER_TOPK = 16
PEER_TOKEN_BLOCK = 128


def rms_norm(x, g):
    xf = x.astype(jnp.float32)
    y = xf * lax.rsqrt(jnp.mean(xf * xf, axis=-1, keepdims=True) + RMS_EPS)
    return (y * g.astype(jnp.float32)).astype(x.dtype)


def ssm_mixer(u, a_re_log, a_im, b_re, b_im, c_re, c_im, d_skip, log_dt, w_glu, b_glu):
    f32 = jnp.float32
    Bsz, L, _ = u.shape
    uf = u.astype(f32)
    ug = uf.reshape(Bsz, L, SSM_GROUPS, SSM_GROUP)
    lam_re = -jnp.exp(a_re_log.astype(f32))
    lam_im = a_im.astype(f32)
    dt = jnp.exp(log_dt.astype(f32))[:, None]
    mag = jnp.exp(lam_re * dt)
    abar_re = mag * jnp.cos(lam_im * dt)
    abar_im = mag * jnp.sin(lam_im * dt)
    den = lam_re * lam_re + lam_im * lam_im
    p = abar_re - 1.0
    q = abar_im
    f_re = (p * lam_re + q * lam_im) / den
    f_im = (q * lam_re - p * lam_im) / den
    br = b_re.astype(f32)
    bi = b_im.astype(f32)
    bb_re = f_re[..., None] * br - f_im[..., None] * bi
    bb_im = f_re[..., None] * bi + f_im[..., None] * br
    bu_re = jnp.einsum('gph,blgh->blgp', bb_re, ug)
    bu_im = jnp.einsum('gph,blgh->blgp', bb_im, ug)
    a_re_t = jnp.broadcast_to(abar_re, bu_re.shape)
    a_im_t = jnp.broadcast_to(abar_im, bu_re.shape)

    def combine(left, right):
        a1r, a1i, b1r, b1i = left
        a2r, a2i, b2r, b2i = right
        return (a1r * a2r - a1i * a2i,
                a1r * a2i + a1i * a2r,
                a2r * b1r - a2i * b1i + b2r,
                a2r * b1i + a2i * b1r + b2i)

    _, _, s_re, s_im = lax.associative_scan(combine, (a_re_t, a_im_t, bu_re, bu_im), axis=1)
    y = (jnp.einsum('ghp,blgp->blgh', c_re.astype(f32), s_re)
         - jnp.einsum('ghp,blgp->blgh', c_im.astype(f32), s_im))
    y = y.reshape(Bsz, L, SSM_WIDTH) + d_skip.astype(f32) * uf
    yg = jax.nn.gelu(y)
    out = yg * jax.nn.sigmoid(yg @ w_glu.astype(f32) + b_glu.astype(f32))
    return out.astype(u.dtype)


def gmlp_mixer(u, v, w_sp, b_sp):
    Bsz, L, _ = u.shape
    nb = L // GMLP_BLOCK
    vf = v.reshape(Bsz, nb, GMLP_BLOCK, GMLP_HEADS, GMLP_HEAD_DIM).astype(jnp.float32)
    mu = jnp.mean(vf, axis=-1, keepdims=True)
    var = jnp.mean(jnp.square(vf - mu), axis=-1, keepdims=True)
    vn = ((vf - mu) * lax.rsqrt(var + RMS_EPS)).astype(v.dtype)
    chunk_id = jnp.arange(GMLP_BLOCK) // CHUNK
    mask = chunk_id[:, None] >= chunk_id[None, :]
    w = jnp.where(mask[None], w_sp, 0.0)
    mixed = jnp.einsum('hts,bnshd->bnthd', w, vn) + b_sp.T[None, None, :, :, None]
    out = u.reshape(Bsz, nb, GMLP_BLOCK, GMLP_HEADS, GMLP_HEAD_DIM) * mixed
    return out.reshape(Bsz, L, GMLP_WIDTH)


def dsa_mixer(q, c_kv, q_idx, k_idx, w_idx, w_uk, w_uv):
    f32 = jnp.float32
    Bsz, L = q.shape[0], q.shape[1]
    topk = min(TOPK_MAX, L // 4)
    nqb = L // Q_BLOCK
    slopes = jnp.exp2(-8.0 * (jnp.arange(ATT_HEADS, dtype=f32) + 1.0) / ATT_HEADS)
    q_abs = jnp.einsum('blhd,hdr->blhr', q, w_uk) * (ATT_HEAD_DIM ** -0.5)
    w_scaled = w_idx * (IDX_HEADS ** -0.5)
    key_chunk = jnp.arange(L) // CHUNK
    b_ix = jnp.arange(Bsz)[:, None, None]

    def to_blocks(a):
        return jnp.moveaxis(a.reshape((Bsz, nqb, Q_BLOCK) + a.shape[2:]), 1, 0)

    def block_fn(args):
        qa, qi, wi, qpos = args
        s = jnp.einsum('bthd,bsd->bths', qi, k_idx) * (IDX_DIM ** -0.5)
        score = jnp.einsum('bth,bths->bts', wi, jax.nn.relu(s)).astype(f32)
        adm = key_chunk[None, :] <= (qpos // CHUNK)[:, None]
        score = jnp.where(adm[None], score, -jnp.inf)
        sel_score, sel = lax.top_k(score, topk)
        valid = jnp.isfinite(sel_score)
        kv = c_kv[b_ix, sel]
        logits = jnp.einsum('bthr,btkr->bthk', qa, kv).astype(f32)
        dist = jnp.abs(qpos[None, :, None] - sel).astype(f32)
        logits = logits - slopes[None, None, :, None] * dist[:, :, None, :]
        logits = jnp.where(valid[:, :, None, :], logits, -jnp.inf)
        probs = jax.nn.softmax(logits, axis=-1).astype(kv.dtype)
        return jnp.einsum('bthk,btkr->bthr', probs, kv)

    qpos_blocks = jnp.arange(L).reshape(nqb, Q_BLOCK)
    o = lax.map(block_fn, (to_blocks(q_abs), to_blocks(q_idx), to_blocks(w_scaled), qpos_blocks))
    o = jnp.moveaxis(o, 0, 1).reshape(Bsz, L, ATT_HEADS, KV_RANK)
    out = jnp.einsum('blhr,hrd->blhd', o, w_uv)
    return out.reshape(Bsz, L, ATT_WIDTH)


def peer_ffn(h, w_q, sub_k1, sub_k2, peer_u, peer_v):
    f32 = jnp.float32
    Bsz, L, D = h.shape
    half = PEER_KEY_DIM // 2
    q = (h @ w_q).reshape(Bsz, L, PEER_HEADS, 2, half)
    s1 = jnp.einsum('blhd,nd->blhn', q[..., 0, :], sub_k1).astype(f32)
    s2 = jnp.einsum('blhd,nd->blhn', q[..., 1, :], sub_k2).astype(f32)
    v1, i1 = lax.top_k(s1, PEER_TOPK)
    v2, i2 = lax.top_k(s2, PEER_TOPK)
    ncand = PEER_TOPK * PEER_TOPK
    cand = (v1[..., :, None] + v2[..., None, :]).reshape(Bsz, L, PEER_HEADS, ncand)
    cand_idx = (i1[..., :, None] * PEER_N_KEYS + i2[..., None, :]).reshape(Bsz, L, PEER_HEADS, ncand)
    top_s, pos = lax.top_k(cand, PEER_TOPK)
    experts = jnp.take_along_axis(cand_idx, pos, axis=-1)
    gates = jax.nn.softmax(top_s, axis=-1).astype(h.dtype)
    n_tok = Bsz * L
    nb = n_tok // PEER_TOKEN_BLOCK
    hk = PEER_HEADS * PEER_TOPK
    hb = h.reshape(nb, PEER_TOKEN_BLOCK, D)
    eb = experts.reshape(nb, PEER_TOKEN_BLOCK, hk)
    gb = gates.reshape(nb, PEER_TOKEN_BLOCK, hk)

    def block_fn(args):
        hx, e, g = args
        act = jax.nn.gelu(jnp.einsum('tkd,td->tk', peer_u[e], hx))
        return jnp.einsum('tk,tkd->td', g * act, peer_v[e])

    out = lax.map(block_fn, (hb, eb, gb))
    return out.reshape(Bsz, L, D)


def _final_norm_kernel(x_ref, g_ref, o_ref):
    x = x_ref[...]
    ms = jnp.mean(x * x, axis=-1, keepdims=True)
    o_ref[...] = x * lax.rsqrt(ms + RMS_EPS) * g_ref[...]


def _final_norm(x, g):
    Bsz, L, D = x.shape
    n = Bsz * L
    tm = 512
    out = pl.pallas_call(
        _final_norm_kernel,
        grid=(n // tm,),
        in_specs=[pl.BlockSpec((tm, D), lambda i: (i, 0)), pl.BlockSpec((1, D), lambda i: (0, 0))],
        out_specs=pl.BlockSpec((tm, D), lambda i: (i, 0)),
        out_shape=jax.ShapeDtypeStruct((n, D), x.dtype),
        name="final_norm",
    )(x.reshape(n, D), g.reshape(1, D))
    return out.reshape(Bsz, L, D)


def kernel(x, c, norm1_g, norm2_g, w_mod, b_mod, w_in, ssm_a_re_log, ssm_a_im, ssm_b_re, ssm_b_im, ssm_c_re, ssm_c_im, ssm_d, ssm_log_dt, ssm_w_glu, ssm_b_glu, gmlp_w_sp, gmlp_b_sp, kv_norm_g, w_uk, w_uv, w_out, peer_w_q, peer_k1, peer_k2, peer_u, peer_v, final_g):
    Bsz, L, _ = x.shape
    c_act = jax.nn.silu(c)
    for l in range(DEPTH):
        mod = c_act @ w_mod[l] + b_mod[l]
        sh1, sc1, g1, sh2, sc2, g2 = jnp.split(mod[:, None, :], 6, axis=-1)
        h = rms_norm(x, norm1_g[l]) * (1.0 + sc1) + sh1
        proj = h @ w_in[l]
        u_a, u_b, v_b, q, ckv, qi, ki, wi = jnp.split(proj, IN_SPLITS, axis=-1)
        y_a = ssm_mixer(u_a, ssm_a_re_log[l], ssm_a_im[l], ssm_b_re[l], ssm_b_im[l], ssm_c_re[l],
                        ssm_c_im[l], ssm_d[l], ssm_log_dt[l], ssm_w_glu[l], ssm_b_glu[l])
        y_b = gmlp_mixer(u_b, v_b, gmlp_w_sp[l], gmlp_b_sp[l])
        y_c = dsa_mixer(q.reshape(Bsz, L, ATT_HEADS, ATT_HEAD_DIM), rms_norm(ckv, kv_norm_g[l]),
                        qi.reshape(Bsz, L, IDX_HEADS, IDX_DIM), ki, wi, w_uk[l], w_uv[l])
        mix = jnp.concatenate([y_a, y_b, y_c], axis=-1)
        x = x + g1 * (mix @ w_out[l])
        h = rms_norm(x, norm2_g[l]) * (1.0 + sc2) + sh2
        x = x + g2 * peer_ffn(h, peer_w_q[l], peer_k1[l], peer_k2[l], peer_u[l], peer_v[l])
    return _final_norm(x, final_g)
```

```python
import functools
import math

import jax
import jax.numpy as jnp
from jax import lax
from jax.experimental import pallas as pl
from jax.experimental.pallas import tpu as pltpu

CHUNK = 64
RMS_EPS = 1e-6

SSM_GROUP = 16
SSM_STATE = 64
SSM_WIDTH = 256
SSM_GROUPS = SSM_WIDTH // SSM_GROUP
SSM_T = 32
GMLP_HEADS = 4
GMLP_HEAD_DIM = 64
GMLP_WIDTH = GMLP_HEADS * GMLP_HEAD_DIM
GMLP_BLOCK = 128
ATT_HEADS = 8
ATT_HEAD_DIM = 64
ATT_WIDTH = ATT_HEADS * ATT_HEAD_DIM
KV_RANK = 128
IDX_HEADS = 4
IDX_DIM = 64
TOPK_MAX = 256

IN_SIZES = (SSM_WIDTH, GMLP_WIDTH, GMLP_WIDTH, ATT_WIDTH, KV_RANK, IDX_HEADS * IDX_DIM, IDX_DIM, IDX_HEADS)
IN_WIDTH = sum(IN_SIZES)
IN_PAD = 1792
COL_UB, COL_VB, COL_Q, COL_CKV, COL_QI, COL_KW = 256, 512, 768, 1280, 1408, 1664

PEER_HEADS = 8
PEER_N_KEYS = 128
PEER_HALF = 64
PEER_TOPK = 16

LANES = 128
VMEM_LIMIT = 56 * 1024 * 1024
NEG = -1e30
INT_MIN = -(2 ** 31)

F32 = jnp.float32
BF16 = jnp.bfloat16


def _cparams(*sem):
    return pltpu.CompilerParams(dimension_semantics=sem, vmem_limit_bytes=VMEM_LIMIT)


def _dot(a, b):
    return jnp.dot(a, b, preferred_element_type=F32)


def _dot_nt(a, b):
    return lax.dot_general(a, b, (((1,), (1,)), ((), ())), preferred_element_type=F32)


def _mod_kernel(c_ref, w_ref, b_ref, o_ref):
    c = c_ref[...]
    ca = c * jax.nn.sigmoid(c)
    o_ref[...] = _dot(ca.astype(BF16), w_ref[...].astype(BF16)) + b_ref[...]


def _modulation(c, w_mod, b_mod):
    bsz, d = c.shape
    n6 = w_mod.shape[1]
    return pl.pallas_call(
        _mod_kernel,
        grid=(n6 // d,),
        in_specs=[pl.BlockSpec((bsz, d), lambda j: (0, 0)),
                  pl.BlockSpec((d, d), lambda j: (0, j)),
                  pl.BlockSpec((1, d), lambda j: (0, j))],
        out_specs=pl.BlockSpec((bsz, d), lambda j: (0, j)),
        out_shape=jax.ShapeDtypeStruct((bsz, n6), F32),
        compiler_params=_cparams("parallel"),
        name="modulation",
    )(c, w_mod, b_mod.reshape(1, n6))


def _in_kernel(x_ref, g_ref, sc_ref, sh_ref, w_ref, kvg_ref, pavg_ref, wsp_ref, bsp_ref,
               ua_ref, yb_ref, q_ref, ckv_ref, qi_ref, ki_ref, wi_ref):
    tm = x_ref.shape[0]
    x = x_ref[...]
    ms = jnp.mean(x * x, axis=-1, keepdims=True)
    h = x * lax.rsqrt(ms + RMS_EPS) * g_ref[...]
    h = h * (1.0 + sc_ref[0]) + sh_ref[0]
    proj = _dot(h.astype(BF16), w_ref[...])

    ua_ref[...] = proj[:, 0:COL_UB]
    q_ref[...] = proj[:, COL_Q:COL_CKV].astype(BF16)
    ckv = proj[:, COL_CKV:COL_QI]
    ckv_ms = jnp.mean(ckv * ckv, axis=-1, keepdims=True)
    ckv_ref[...] = (ckv * lax.rsqrt(ckv_ms + RMS_EPS) * kvg_ref[...]).astype(BF16)
    qi_ref[...] = proj[:, COL_QI:COL_KW].astype(BF16)
    kw = proj[:, COL_KW:IN_PAD]
    ki_ref[...] = kw[:, 0:IDX_DIM].astype(BF16)
    wi_ref[...] = kw

    u_b = proj[:, COL_UB:COL_VB]
    v_b = proj[:, COL_VB:COL_Q]
    pavg = pavg_ref[...]

    def head_mean(a):
        hi = a.astype(BF16)
        lo = (a - hi.astype(F32)).astype(BF16)
        return _dot(hi, pavg) + _dot(lo, pavg)

    mu = head_mean(v_b)
    dv = v_b - mu
    var = head_mean(dv * dv)
    vn = (dv * lax.rsqrt(var + RMS_EPS)).astype(BF16)
    lane_head = lax.broadcasted_iota(jnp.int32, (GMLP_BLOCK, GMLP_WIDTH), 1) // GMLP_HEAD_DIM
    for blk in range(tm // GMLP_BLOCK):
        rows = slice(blk * GMLP_BLOCK, (blk + 1) * GMLP_BLOCK)
        vblk = vn[rows, :]
        mixed = bsp_ref[...]
        for hh in range(GMLP_HEADS):
            mixed = mixed + jnp.where(lane_head == hh, _dot(wsp_ref[hh], vblk), 0.0)
        yb_ref[rows, :] = (u_b[rows, :] * mixed).astype(BF16)


def _input_stage(x2, seq, g, sc, sh, w_in_pad, kvg, pavg, wsp, bsp, tm=512):
    n, d = x2.shape
    tpb = seq // tm
    row = lambda i: (i, 0)
    const2 = lambda i: (0, 0)
    per_b = lambda i: (i // tpb, 0, 0)
    outs = [(SSM_WIDTH, F32), (GMLP_WIDTH, BF16), (ATT_WIDTH, BF16), (KV_RANK, BF16),
            (IDX_HEADS * IDX_DIM, BF16), (IDX_DIM, BF16), (LANES, F32)]
    return pl.pallas_call(
        _in_kernel,
        grid=(n // tm,),
        in_specs=[pl.BlockSpec((tm, d), row),
                  pl.BlockSpec((1, d), const2),
                  pl.BlockSpec((1, 1, d), per_b),
                  pl.BlockSpec((1, 1, d), per_b),
                  pl.BlockSpec((d, IN_PAD), const2),
                  pl.BlockSpec((1, KV_RANK), const2),
                  pl.BlockSpec((GMLP_WIDTH, GMLP_WIDTH), const2),
                  pl.BlockSpec((GMLP_HEADS, GMLP_BLOCK, GMLP_BLOCK), lambda i: (0, 0, 0)),
                  pl.BlockSpec((GMLP_BLOCK, GMLP_WIDTH), const2)],
        out_specs=[pl.BlockSpec((tm, w), row) for w, _ in outs],
        out_shape=[jax.ShapeDtypeStruct((n, w), dt) for w, dt in outs],
        compiler_params=_cparams("parallel"),
        name="input_proj_gmlp",
    )(x2, g, sc, sh, w_in_pad, kvg, pavg, wsp, bsp)


def _ssm_prep(a_re_log, a_im, b_re, b_im, c_re, c_im, log_dt, t_len):
    hp = lax.Precision.HIGHEST
    lam_re = -jnp.exp(a_re_log)
    lam_im = a_im
    dt = jnp.exp(log_dt)[:, None]
    mag = jnp.exp(lam_re * dt)
    abar_re = mag * jnp.cos(lam_im * dt)
    abar_im = mag * jnp.sin(lam_im * dt)
    den = lam_re * lam_re + lam_im * lam_im
    p = abar_re - 1.0
    qq = abar_im
    f_re = (p * lam_re + qq * lam_im) / den
    f_im = (qq * lam_re - p * lam_im) / den
    bb_re = f_re[..., None] * b_re - f_im[..., None] * b_im
    bb_im = f_re[..., None] * b_im + f_im[..., None] * b_re
    k = jnp.arange(t_len + 1, dtype=F32)[:, None, None]
    pmag = jnp.exp(k * (lam_re * dt))
    ang = k * (lam_im * dt)
    pw_re = pmag * jnp.cos(ang)
    pw_im = pmag * jnp.sin(ang)
    cp_re = c_re[None] * pw_re[:, :, None, :] - c_im[None] * pw_im[:, :, None, :]
    cp_im = c_re[None] * pw_im[:, :, None, :] + c_im[None] * pw_re[:, :, None, :]
    kern = (jnp.einsum('kghp,gpj->kghj', cp_re[:t_len], bb_re, precision=hp)
            - jnp.einsum('kghp,gpj->kghj', cp_im[:t_len], bb_im, precision=hp))
    tt = jnp.arange(t_len)
    lag = tt[None, :] - tt[:, None]
    toep = jnp.where((lag >= 0)[:, :, None, None, None], kern[jnp.clip(lag, 0)], 0.0)
    g_n, h_n = SSM_GROUPS, SSM_GROUP
    m = toep.transpose(2, 0, 4, 1, 3).reshape(g_n, t_len * h_n, t_len * h_n)
    rev_re = pw_re[t_len - 1 - tt]
    rev_im = pw_im[t_len - 1 - tt]
    w_re = rev_re[..., None] * bb_re[None] - rev_im[..., None] * bb_im[None]
    w_im = rev_re[..., None] * bb_im[None] + rev_im[..., None] * bb_re[None]
    w_re = w_re.transpose(1, 0, 3, 2).reshape(g_n, t_len * h_n, SSM_STATE)
    w_im = w_im.transpose(1, 0, 3, 2).reshape(g_n, t_len * h_n, SSM_STATE)
    v_re = cp_re[1:].transpose(1, 3, 0, 2).reshape(g_n, SSM_STATE, t_len * h_n)
    v_im = (-cp_im[1:]).transpose(1, 3, 0, 2).reshape(g_n, SSM_STATE, t_len * h_n)
    at_re = pw_re[t_len][:, None, :]
    at_im = pw_im[t_len][:, None, :]
    return (m.astype(BF16), w_re.astype(BF16), w_im.astype(BF16), v_re.astype(BF16), v_im.astype(BF16),
            at_re, at_im)


def _ssm_kernel(u_ref, m_ref, wre_ref, wim_ref, vre_ref, vim_ref, are_ref, aim_ref, y_ref,
                lre_ref, lim_ref, pre_ref, pim_ref, *, nb, nchunks):
    u = u_ref[...]
    lre_ref[...] = _dot(u, wre_ref[...])
    lim_ref[...] = _dot(u, wim_ref[...])
    ar = are_ref[...]
    ai = aim_ref[...]
    s_re = jnp.zeros((nb, SSM_STATE), F32)
    s_im = jnp.zeros((nb, SSM_STATE), F32)
    for c in range(nchunks):
        rows = slice(c * nb, (c + 1) * nb)
        pre_ref[rows, :] = s_re
        pim_ref[rows, :] = s_im
        s_re, s_im = (ar * s_re - ai * s_im + lre_ref[rows, :],
                      ar * s_im + ai * s_re + lim_ref[rows, :])
    y_ref[...] = (_dot(u, m_ref[...])
                  + _dot(pre_ref[...].astype(BF16), vre_ref[...])
                  + _dot(pim_ref[...].astype(BF16), vim_ref[...]))


def _ssm_scan(u_a, bsz, seq, prep):
    m, w_re, w_im, v_re, v_im, at_re, at_im = prep
    t_len = SSM_T
    nchunks = seq // t_len
    r = nchunks * bsz
    tw = t_len * SSM_GROUP
    u = u_a.reshape(bsz, nchunks, t_len, SSM_GROUPS, SSM_GROUP).transpose(3, 1, 0, 2, 4)
    u = u.reshape(SSM_GROUPS, r, tw).astype(BF16)
    grp = lambda g: (g, 0, 0)
    y = pl.pallas_call(
        functools.partial(_ssm_kernel, nb=bsz, nchunks=nchunks),
        grid=(SSM_GROUPS,),
        in_specs=[pl.BlockSpec((None, r, tw), grp),
                  pl.BlockSpec((None, tw, tw), grp),
                  pl.BlockSpec((None, tw, SSM_STATE), grp),
                  pl.BlockSpec((None, tw, SSM_STATE), grp),
                  pl.BlockSpec((None, SSM_STATE, tw), grp),
                  pl.BlockSpec((None, SSM_STATE, tw), grp),
                  pl.BlockSpec((None, 1, SSM_STATE), grp),
                  pl.BlockSpec((None, 1, SSM_STATE), grp)],
        out_specs=pl.BlockSpec((None, r, tw), grp),
        out_shape=jax.ShapeDtypeStruct((SSM_GROUPS, r, tw), F32),
        scratch_shapes=[pltpu.VMEM((r, SSM_STATE), F32) for _ in range(4)],
        compiler_params=_cparams("parallel"),
        name="ssm_scan",
    )(u, m, w_re, w_im, v_re, v_im, at_re, at_im)
    y = y.reshape(SSM_GROUPS, nchunks, bsz, t_len, SSM_GROUP).transpose(2, 1, 3, 0, 4)
    return y.reshape(bsz * seq, SSM_WIDTH)


def _dsa_kernel(q_ref, qi_ref, wi_ref, ki_ref, ckv_ref, wuk_ref, wuv_ref, tri_ref, o_ref,
                key_ref, bias_ref, dist_ref, *, topk):
    tq = q_ref.shape[0]
    seq = ki_ref.shape[0]
    t0 = pl.program_id(1) * tq
    row = t0 + lax.broadcasted_iota(jnp.int32, (tq, 1), 0)
    col = lax.broadcasted_iota(jnp.int32, (1, seq), 1)
    adm = (col // CHUNK) <= (row // CHUNK)

    ki = ki_ref[...]
    score = jnp.zeros((tq, seq), F32)
    for h in range(IDX_HEADS):
        s = _dot_nt(qi_ref[:, h * IDX_DIM:(h + 1) * IDX_DIM], ki)
        w = wi_ref[:, IDX_DIM + h:IDX_DIM + h + 1] * ((IDX_HEADS ** -0.5) * (IDX_DIM ** -0.5))
        score = score + jnp.maximum(s, 0.0) * w
    score = jnp.where(score == 0.0, 0.0, score)
    bits = lax.bitcast_convert_type(score, jnp.int32)
    key = jnp.where(bits < 0, bits ^ jnp.int32(0x7FFFFFFF), bits)
    key_ref[...] = jnp.where(adm, key, jnp.int32(INT_MIN))

    kf = jnp.float32(topk)
    ans = jnp.full((tq, 1), INT_MIN, jnp.int32)
    for bit in range(31, -1, -1):
        inc = INT_MIN if bit == 31 else (1 << bit)
        trial = ans + jnp.int32(inc)
        cnt = jnp.sum((key_ref[...] >= trial).astype(F32), axis=1, keepdims=True)
        ans = jnp.where(cnt >= kf, trial, ans)

    key = key_ref[...]
    n_gt = jnp.sum((key > ans).astype(F32), axis=1, keepdims=True)
    room = kf - n_gt
    carry = jnp.zeros((tq, 1), F32)
    tri = tri_ref[...]
    for jb in range(seq // LANES):
        cols = slice(jb * LANES, (jb + 1) * LANES)
        kb = key_ref[:, cols]
        eq = kb == ans
        pre = _dot(eq.astype(BF16), tri) + carry
        sel = (kb > ans) | (eq & (pre <= room))
        colb = jb * LANES + lax.broadcasted_iota(jnp.int32, (1, LANES), 1)
        admb = (colb // CHUNK) <= (row // CHUNK)
        bias_ref[:, cols] = jnp.where(sel & admb, 0.0, NEG)
        carry = pre[:, LANES - 1:LANES]
    dist_ref[...] = jnp.abs(row - col).astype(F32)

    ckv = ckv_ref[...]
    for h in range(ATT_HEADS):
        hs = slice(h * ATT_HEAD_DIM, (h + 1) * ATT_HEAD_DIM)
        qa = (_dot(q_ref[:, hs], wuk_ref[h]) * (ATT_HEAD_DIM ** -0.5)).astype(BF16)
        slope = 2.0 ** (-8.0 * (h + 1) / ATT_HEADS)
        lg = _dot_nt(qa, ckv) + (bias_ref[...] - slope * dist_ref[...])
        mx = jnp.max(lg, axis=1, keepdims=True)
        p = jnp.exp(lg - mx)
        den = jnp.sum(p, axis=1, keepdims=True)
        o = _dot(p.astype(BF16), ckv) / den
        o_ref[:, hs] = _dot(o.astype(BF16), wuv_ref[h]).astype(o_ref.dtype)


def _dsa(q, qi, wi, ki, ckv, w_uk, w_uv, bsz, seq, tq=128):
    n = q.shape[0]
    nqt = seq // tq
    topk = min(TOPK_MAX, seq // 4)
    tri = (jnp.arange(LANES)[:, None] <= jnp.arange(LANES)[None, :]).astype(BF16)
    qrow = lambda b, i: (b * nqt + i, 0)
    krow = lambda b, i: (b, 0)
    c3 = lambda b, i: (0, 0, 0)
    return pl.pallas_call(
        functools.partial(_dsa_kernel, topk=topk),
        grid=(bsz, nqt),
        in_specs=[pl.BlockSpec((tq, ATT_WIDTH), qrow),
                  pl.BlockSpec((tq, IDX_HEADS * IDX_DIM), qrow),
                  pl.BlockSpec((tq, LANES), qrow),
                  pl.BlockSpec((seq, IDX_DIM), krow),
                  pl.BlockSpec((seq, KV_RANK), krow),
                  pl.BlockSpec((ATT_HEADS, ATT_HEAD_DIM, KV_RANK), c3),
                  pl.BlockSpec((ATT_HEADS, KV_RANK, ATT_HEAD_DIM), c3),
                  pl.BlockSpec((LANES, LANES), lambda b, i: (0, 0))],
        out_specs=pl.BlockSpec((tq, ATT_WIDTH), qrow),
        out_shape=jax.ShapeDtypeStruct((n, ATT_WIDTH), BF16),
        scratch_shapes=[pltpu.VMEM((tq, seq), jnp.int32),
                        pltpu.VMEM((tq, seq), F32),
                        pltpu.VMEM((tq, seq), F32)],
        compiler_params=_cparams("parallel", "parallel"),
        name="dsa_attention",
    )(q, qi, wi, ki, ckv, w_uk.astype(BF16), w_uv.astype(BF16), tri)


def _out_kernel(ys_ref, ua_ref, yb_ref, yc_ref, x_ref, d_ref, wglu_ref, bglu_ref, wo_ref, g1_ref,
                n2_ref, sc_ref, sh_ref, xo_ref, h2_ref):
    y = ys_ref[...] + d_ref[...] * ua_ref[...]
    yg = jax.nn.gelu(y)
    z = _dot(yg.astype(BF16), wglu_ref[...]) + bglu_ref[...]
    ya = yg * jax.nn.sigmoid(z)
    a_w, b_w = SSM_WIDTH, SSM_WIDTH + GMLP_WIDTH
    mix = (_dot(ya.astype(BF16), wo_ref[0:a_w, :])
           + _dot(yb_ref[...], wo_ref[a_w:b_w, :])
           + _dot(yc_ref[...], wo_ref[b_w:, :]))
    xn = x_ref[...] + g1_ref[0] * mix
    xo_ref[...] = xn
    ms = jnp.mean(xn * xn, axis=-1, keepdims=True)
    h = xn * lax.rsqrt(ms + RMS_EPS) * n2_ref[...]
    h2_ref[...] = (h * (1.0 + sc_ref[0]) + sh_ref[0]).astype(BF16)


def _output_stage(ys, ua, yb, yc, x2, seq, d_skip, w_glu, b_glu, w_out, g1, n2g, sc2, sh2, tm=512):
    n, d = x2.shape
    tpb = seq // tm
    row = lambda i: (i, 0)
    const2 = lambda i: (0, 0)
    per_b = lambda i: (i // tpb, 0, 0)
    return pl.pallas_call(
        _out_kernel,
        grid=(n // tm,),
        in_specs=[pl.BlockSpec((tm, SSM_WIDTH), row),
                  pl.BlockSpec((tm, SSM_WIDTH), row),
                  pl.BlockSpec((tm, GMLP_WIDTH), row),
                  pl.BlockSpec((tm, ATT_WIDTH), row),
                  pl.BlockSpec((tm, d), row),
                  pl.BlockSpec((1, SSM_WIDTH), const2),
                  pl.BlockSpec((SSM_WIDTH, SSM_WIDTH), const2),
                  pl.BlockSpec((1, SSM_WIDTH), const2),
                  pl.BlockSpec((d, d), const2),
                  pl.BlockSpec((1, 1, d), per_b),
                  pl.BlockSpec((1, d), const2),
                  pl.BlockSpec((1, 1, d), per_b),
                  pl.BlockSpec((1, 1, d), per_b)],
        out_specs=[pl.BlockSpec((tm, d), row), pl.BlockSpec((tm, d), row)],
        out_shape=[jax.ShapeDtypeStruct((n, d), F32), jax.ShapeDtypeStruct((n, d), BF16)],
        compiler_params=_cparams("parallel"),
        name="out_proj_norm2",
    )(ys, ua, yb, yc, x2, d_skip, w_glu, b_glu, w_out, g1, n2g, sc2, sh2)


def _top_rows(x, n_top, dst_ref):
    for k in range(n_top):
        mx = jnp.max(x, axis=0, keepdims=True)
        dst_ref[k:k + 1, :] = mx
        x = jnp.where(x == mx, -jnp.inf, x)


def _peer_kernel(h2_ref, x_ref, g2_ref, fg_ref, wq_ref, kbd_ref, u_ref, vt_ref, o_ref,
                 s2_ref, e2_ref, c_ref, e1_ref, acc_ref, v1_ref, v2_ref, cand_ref, top_ref,
                 *, final_norm):
    j = pl.program_id(1)
    te = u_ref.shape[0]
    n1 = te // PEER_N_KEYS
    h2 = h2_ref[...]

    @pl.when(j == 0)
    def _():
        q = _dot(h2, wq_ref[...]).astype(BF16)
        st = _dot_nt(kbd_ref[...], q)
        for h in range(PEER_HEADS):
            base = h * 2 * PEER_N_KEYS
            s1 = st[base:base + PEER_N_KEYS, :]
            s2 = st[base + PEER_N_KEYS:base + 2 * PEER_N_KEYS, :]
            _top_rows(s1, PEER_TOPK + 1, v1_ref)
            _top_rows(s2, PEER_TOPK + 1, v2_ref)
            v2 = v2_ref[0:PEER_TOPK, :]
            for a in range(PEER_TOPK):
                cand_ref[a * PEER_TOPK:(a + 1) * PEER_TOPK, :] = v1_ref[a:a + 1, :] + v2
            _top_rows(cand_ref[...], PEER_TOPK + 1, top_ref)
            top = top_ref[0:PEER_TOPK, :]
            z = jnp.sum(jnp.exp(top - top_ref[0:1, :]), axis=0, keepdims=True)
            nxt = jnp.maximum(top_ref[PEER_TOPK:PEER_TOPK + 1, :],
                              jnp.maximum(v1_ref[PEER_TOPK:PEER_TOPK + 1, :] + v2_ref[0:1, :],
                                          v1_ref[0:1, :] + v2_ref[PEER_TOPK:PEER_TOPK + 1, :]))
            thr = 0.5 * (top_ref[PEER_TOPK - 1:PEER_TOPK, :] + nxt)
            e1_ref[h] = jnp.exp(s1 - v1_ref[0:1, :]) / z
            e2_ref[h] = jnp.exp(s2 - v2_ref[0:1, :])
            c_ref[h] = thr - s1
            s2_ref[h] = s2
        acc_ref[...] = jnp.zeros_like(acc_ref)

    act = jax.nn.gelu(_dot_nt(u_ref[...], h2))
    r0 = pl.multiple_of(j * n1, n1)
    gated = []
    for il in range(n1):
        g = jnp.zeros((PEER_N_KEYS, h2.shape[0]), F32)
        for h in range(PEER_HEADS):
            cs = c_ref[h, pl.ds(r0, n1), :]
            es = e1_ref[h, pl.ds(r0, n1), :]
            g = g + jnp.where(s2_ref[h] >= cs[il:il + 1, :], e2_ref[h], 0.0) * es[il:il + 1, :]
        gated.append((g * act[il * PEER_N_KEYS:(il + 1) * PEER_N_KEYS, :]).astype(BF16))
    ga = jnp.concatenate(gated, axis=0)
    acc_ref[...] += _dot(vt_ref[...], ga)

    @pl.when(j == pl.num_programs(1) - 1)
    def _():
        out = x_ref[...] + g2_ref[0] * acc_ref[...].T
        if final_norm:
            ms = jnp.mean(out * out, axis=-1, keepdims=True)
            out = out * lax.rsqrt(ms + RMS_EPS) * fg_ref[...]
        o_ref[...] = out


def _peer(h2, x2, seq, g2, fg, wq, kbd, u_bf, vt_bf, final_norm, tm=512, te=1024):
    n, d = x2.shape
    n_exp = u_bf.shape[0]
    tpb = seq // tm
    trow = lambda i, j: (i, 0)
    c2 = lambda i, j: (0, 0)
    hs = (PEER_HEADS, PEER_N_KEYS, tm)
    return pl.pallas_call(
        functools.partial(_peer_kernel, final_norm=final_norm),
        grid=(n // tm, n_exp // te),
        in_specs=[pl.BlockSpec((tm, d), trow),
                  pl.BlockSpec((tm, d), trow),
                  pl.BlockSpec((1, 1, d), lambda i, j: (i // tpb, 0, 0)),
                  pl.BlockSpec((1, d), c2),
                  pl.BlockSpec((d, PEER_HEADS * 2 * PEER_HALF), c2),
                  pl.BlockSpec((PEER_HEADS * 2 * PEER_N_KEYS, PEER_HEADS * 2 * PEER_HALF), c2),
                  pl.BlockSpec((te, d), lambda i, j: (j, 0)),
                  pl.BlockSpec((d, te), lambda i, j: (0, j))],
        out_specs=pl.BlockSpec((tm, d), trow),
        out_shape=jax.ShapeDtypeStruct((n, d), F32),
        scratch_shapes=[pltpu.VMEM(hs, F32), pltpu.VMEM(hs, F32), pltpu.VMEM(hs, F32), pltpu.VMEM(hs, F32),
                        pltpu.VMEM((d, tm), F32),
                        pltpu.VMEM((PEER_TOPK + 8, tm), F32), pltpu.VMEM((PEER_TOPK + 8, tm), F32),
                        pltpu.VMEM((PEER_TOPK * PEER_TOPK, tm), F32),
                        pltpu.VMEM((PEER_TOPK + 8, tm), F32)],
        compiler_params=_cparams("parallel", "arbitrary"),
        name="peer_dense",
    )(h2, x2, g2, fg, wq, kbd, u_bf, vt_bf)


def _peer_key_matrix(k1, k2):
    blocks = []
    for h in range(PEER_HEADS):
        for half, kk in enumerate((k1, k2)):
            col = (2 * h + half) * PEER_HALF
            blocks.append(jnp.pad(kk, ((0, 0), (col, PEER_HEADS * 2 * PEER_HALF - col - PEER_HALF))))
    return jnp.concatenate(blocks, axis=0).astype(BF16)


def kernel(x, c, norm1_g, norm2_g, w_mod, b_mod, w_in, ssm_a_re_log, ssm_a_im, ssm_b_re, ssm_b_im, ssm_c_re, ssm_c_im, ssm_d, ssm_log_dt, ssm_w_glu, ssm_b_glu, gmlp_w_sp, gmlp_b_sp, kv_norm_g, w_uk, w_uv, w_out, peer_w_q, peer_k1, peer_k2, peer_u, peer_v, final_g):
    bsz, seq, d = x.shape
    depth = w_mod.shape[0]
    x2 = x.reshape(bsz * seq, d)
    head_of = jnp.arange(GMLP_WIDTH) // GMLP_HEAD_DIM
    pavg = ((head_of[:, None] == head_of[None, :]).astype(F32) / GMLP_HEAD_DIM).astype(BF16)
    chunk_of = jnp.arange(GMLP_BLOCK) // CHUNK
    sp_mask = chunk_of[:, None] >= chunk_of[None, :]
    for l in range(depth):
        mod = _modulation(c, w_mod[l], b_mod[l])
        sh1, sc1, g1, sh2, sc2, g2 = [mod[:, i * d:(i + 1) * d].reshape(bsz, 1, d) for i in range(6)]
        w_in_pad = jnp.pad(w_in[l], ((0, 0), (0, IN_PAD - IN_WIDTH))).astype(BF16)
        wsp = jnp.where(sp_mask[None], gmlp_w_sp[l], 0.0).astype(BF16)
        bsp = jnp.repeat(gmlp_b_sp[l].T, GMLP_HEAD_DIM, axis=1)
        ua, yb, q, ckv, qi, ki, wi = _input_stage(
            x2, seq, norm1_g[l].reshape(1, d), sc1, sh1, w_in_pad, kv_norm_g[l].reshape(1, KV_RANK),
            pavg, wsp, bsp)
        prep = _ssm_prep(ssm_a_re_log[l], ssm_a_im[l], ssm_b_re[l], ssm_b_im[l], ssm_c_re[l], ssm_c_im[l],
                         ssm_log_dt[l], SSM_T)
        ys = _ssm_scan(ua, bsz, seq, prep)
        yc = _dsa(q, qi, wi, ki, ckv, w_uk[l], w_uv[l], bsz, seq)
        x2, h2 = _output_stage(
            ys, ua, yb, yc, x2, seq, ssm_d[l].reshape(1, SSM_WIDTH), ssm_w_glu[l].astype(BF16),
            ssm_b_glu[l].reshape(1, SSM_WIDTH), w_out[l].astype(BF16), g1, norm2_g[l].reshape(1, d), sc2, sh2)
        x2 = _peer(h2, x2, seq, g2, final_g.reshape(1, d), peer_w_q[l].astype(BF16),
                   _peer_key_matrix(peer_k1[l], peer_k2[l]), peer_u[l].astype(BF16),
                   peer_v[l].T.astype(BF16), final_norm=(l == depth - 1))
    return x2.reshape(bsz, seq, d)
```

```python
import functools
import math

import jax
import jax.numpy as jnp
from jax import lax
from jax.experimental import pallas as pl
from jax.experimental.pallas import tpu as pltpu

CHUNK = 64
RMS_EPS = 1e-6

SSM_GROUP = 16
SSM_STATE = 64
SSM_WIDTH = 256
SSM_GROUPS = SSM_WIDTH // SSM_GROUP
SSM_T = 32
GMLP_HEADS = 4
GMLP_HEAD_DIM = 64
GMLP_WIDTH = GMLP_HEADS * GMLP_HEAD_DIM
GMLP_BLOCK = 128
ATT_HEADS = 8
ATT_HEAD_DIM = 64
ATT_WIDTH = ATT_HEADS * ATT_HEAD_DIM
KV_RANK = 128
IDX_HEADS = 4
IDX_DIM = 64
TOPK_MAX = 256

IN_SIZES = (SSM_WIDTH, GMLP_WIDTH, GMLP_WIDTH, ATT_WIDTH, KV_RANK, IDX_HEADS * IDX_DIM, IDX_DIM, IDX_HEADS)
IN_WIDTH = sum(IN_SIZES)
IN_PAD = 1792
COL_UB, COL_VB, COL_Q, COL_CKV, COL_QI, COL_KW = 256, 512, 768, 1280, 1408, 1664

PEER_HEADS = 8
PEER_N_KEYS = 128
PEER_HALF = 64
PEER_TOPK = 16

LANES = 128
VMEM_LIMIT = 56 * 1024 * 1024
NEG = -1e30
INT_MIN = -(2 ** 31)

F32 = jnp.float32
BF16 = jnp.bfloat16


def _cparams(*sem):
    return pltpu.CompilerParams(dimension_semantics=sem, vmem_limit_bytes=VMEM_LIMIT)


def _dot(a, b):
    return jnp.dot(a, b, preferred_element_type=F32)


def _dot_nt(a, b):
    return lax.dot_general(a, b, (((1,), (1,)), ((), ())), preferred_element_type=F32)


def _mod_kernel(c_ref, w_ref, b_ref, o_ref):
    c = c_ref[...]
    ca = c * jax.nn.sigmoid(c)
    o_ref[...] = _dot(ca.astype(BF16), w_ref[...].astype(BF16)) + b_ref[...]


def _modulation(c, w_mod, b_mod):
    bsz, d = c.shape
    n6 = w_mod.shape[1]
    return pl.pallas_call(
        _mod_kernel,
        grid=(n6 // d,),
        in_specs=[pl.BlockSpec((bsz, d), lambda j: (0, 0)),
                  pl.BlockSpec((d, d), lambda j: (0, j)),
                  pl.BlockSpec((1, d), lambda j: (0, j))],
        out_specs=pl.BlockSpec((bsz, d), lambda j: (0, j)),
        out_shape=jax.ShapeDtypeStruct((bsz, n6), F32),
        compiler_params=_cparams("parallel"),
        name="modulation",
    )(c, w_mod, b_mod.reshape(1, n6))


def _in_kernel(x_ref, g_ref, sc_ref, sh_ref, w_ref, kvg_ref, pavg_ref, wsp_ref, bsp_ref,
               ua_ref, yb_ref, q_ref, ckv_ref, qi_ref, ki_ref, wi_ref):
    tm = x_ref.shape[0]
    x = x_ref[...]
    ms = jnp.mean(x * x, axis=-1, keepdims=True)
    h = x * lax.rsqrt(ms + RMS_EPS) * g_ref[...]
    h = h * (1.0 + sc_ref[0]) + sh_ref[0]
    proj = _dot(h.astype(BF16), w_ref[...])

    ua_ref[...] = proj[:, 0:COL_UB]
    q_ref[...] = proj[:, COL_Q:COL_CKV].astype(BF16)
    ckv = proj[:, COL_CKV:COL_QI]
    ckv_ms = jnp.mean(ckv * ckv, axis=-1, keepdims=True)
    ckv_ref[...] = (ckv * lax.rsqrt(ckv_ms + RMS_EPS) * kvg_ref[...]).astype(BF16)
    qi_ref[...] = proj[:, COL_QI:COL_KW].astype(BF16)
    kw = proj[:, COL_KW:IN_PAD]
    ki_ref[...] = kw[:, 0:IDX_DIM].astype(BF16)
    wi_ref[...] = kw

    u_b = proj[:, COL_UB:COL_VB]
    v_b = proj[:, COL_VB:COL_Q]
    pavg = pavg_ref[...]

    def head_mean(a):
        hi = a.astype(BF16)
        lo = (a - hi.astype(F32)).astype(BF16)
        return _dot(hi, pavg) + _dot(lo, pavg)

    mu = head_mean(v_b)
    dv = v_b - mu
    var = head_mean(dv * dv)
    vn = (dv * lax.rsqrt(var + RMS_EPS)).astype(BF16)
    lane_head = lax.broadcasted_iota(jnp.int32, (GMLP_BLOCK, GMLP_WIDTH), 1) // GMLP_HEAD_DIM
    for blk in range(tm // GMLP_BLOCK):
        rows = slice(blk * GMLP_BLOCK, (blk + 1) * GMLP_BLOCK)
        vblk = vn[rows, :]
        mixed = bsp_ref[...]
        for hh in range(GMLP_HEADS):
            mixed = mixed + jnp.where(lane_head == hh, _dot(wsp_ref[hh], vblk), 0.0)
        yb_ref[rows, :] = (u_b[rows, :] * mixed).astype(BF16)


def _input_stage(x2, seq, g, sc, sh, w_in_pad, kvg, pavg, wsp, bsp, tm=512):
    n, d = x2.shape
    tpb = seq // tm
    row = lambda i: (i, 0)
    const2 = lambda i: (0, 0)
    per_b = lambda i: (i // tpb, 0, 0)
    outs = [(SSM_WIDTH, F32), (GMLP_WIDTH, BF16), (ATT_WIDTH, BF16), (KV_RANK, BF16),
            (IDX_HEADS * IDX_DIM, BF16), (IDX_DIM, BF16), (LANES, F32)]
    return pl.pallas_call(
        _in_kernel,
        grid=(n // tm,),
        in_specs=[pl.BlockSpec((tm, d), row),
                  pl.BlockSpec((1, d), const2),
                  pl.BlockSpec((1, 1, d), per_b),
                  pl.BlockSpec((1, 1, d), per_b),
                  pl.BlockSpec((d, IN_PAD), const2),
                  pl.BlockSpec((1, KV_RANK), const2),
                  pl.BlockSpec((GMLP_WIDTH, GMLP_WIDTH), const2),
                  pl.BlockSpec((GMLP_HEADS, GMLP_BLOCK, GMLP_BLOCK), lambda i: (0, 0, 0)),
                  pl.BlockSpec((GMLP_BLOCK, GMLP_WIDTH), const2)],
        out_specs=[pl.BlockSpec((tm, w), row) for w, _ in outs],
        out_shape=[jax.ShapeDtypeStruct((n, w), dt) for w, dt in outs],
        compiler_params=_cparams("parallel"),
        name="input_proj_gmlp",
    )(x2, g, sc, sh, w_in_pad, kvg, pavg, wsp, bsp)


def _ssm_prep(a_re_log, a_im, b_re, b_im, c_re, c_im, log_dt, t_len):
    hp = lax.Precision.HIGHEST
    lam_re = -jnp.exp(a_re_log)
    lam_im = a_im
    dt = jnp.exp(log_dt)[:, None]
    mag = jnp.exp(lam_re * dt)
    abar_re = mag * jnp.cos(lam_im * dt)
    abar_im = mag * jnp.sin(lam_im * dt)
    den = lam_re * lam_re + lam_im * lam_im
    p = abar_re - 1.0
    qq = abar_im
    f_re = (p * lam_re + qq * lam_im) / den
    f_im = (qq * lam_re - p * lam_im) / den
    bb_re = f_re[..., None] * b_re - f_im[..., None] * b_im
    bb_im = f_re[..., None] * b_im + f_im[..., None] * b_re
    k = jnp.arange(t_len + 1, dtype=F32)[:, None, None]
    pmag = jnp.exp(k * (lam_re * dt))
    ang = k * (lam_im * dt)
    pw_re = pmag * jnp.cos(ang)
    pw_im = pmag * jnp.sin(ang)
    cp_re = c_re[None] * pw_re[:, :, None, :] - c_im[None] * pw_im[:, :, None, :]
    cp_im = c_re[None] * pw_im[:, :, None, :] + c_im[None] * pw_re[:, :, None, :]
    kern = (jnp.einsum('kghp,gpj->kghj', cp_re[:t_len], bb_re, precision=hp)
            - jnp.einsum('kghp,gpj->kghj', cp_im[:t_len], bb_im, precision=hp))
    tt = jnp.arange(t_len)
    lag = tt[None, :] - tt[:, None]
    toep = jnp.where((lag >= 0)[:, :, None, None, None], kern[jnp.clip(lag, 0)], 0.0)
    g_n, h_n = SSM_GROUPS, SSM_GROUP
    m = toep.transpose(2, 0, 4, 1, 3).reshape(g_n, t_len * h_n, t_len * h_n)
    rev_re = pw_re[t_len - 1 - tt]
    rev_im = pw_im[t_len - 1 - tt]
    w_re = rev_re[..., None] * bb_re[None] - rev_im[..., None] * bb_im[None]
    w_im = rev_re[..., None] * bb_im[None] + rev_im[..., None] * bb_re[None]
    w_re = w_re.transpose(1, 0, 3, 2).reshape(g_n, t_len * h_n, SSM_STATE)
    w_im = w_im.transpose(1, 0, 3, 2).reshape(g_n, t_len * h_n, SSM_STATE)
    v_re = cp_re[1:].transpose(1, 3, 0, 2).reshape(g_n, SSM_STATE, t_len * h_n)
    v_im = (-cp_im[1:]).transpose(1, 3, 0, 2).reshape(g_n, SSM_STATE, t_len * h_n)
    at_re = pw_re[t_len][:, None, :]
    at_im = pw_im[t_len][:, None, :]
    return (m.astype(BF16), w_re.astype(BF16), w_im.astype(BF16), v_re.astype(BF16), v_im.astype(BF16),
            at_re, at_im)


def _ssm_kernel(u_ref, m_ref, wre_ref, wim_ref, vre_ref, vim_ref, are_ref, aim_ref, y_ref,
                lre_ref, lim_ref, pre_ref, pim_ref, *, nb, nchunks):
    u = u_ref[...]
    lre_ref[...] = _dot(u, wre_ref[...])
    lim_ref[...] = _dot(u, wim_ref[...])
    ar = are_ref[...]
    ai = aim_ref[...]
    s_re = jnp.zeros((nb, SSM_STATE), F32)
    s_im = jnp.zeros((nb, SSM_STATE), F32)
    for c in range(nchunks):
        rows = slice(c * nb, (c + 1) * nb)
        pre_ref[rows, :] = s_re
        pim_ref[rows, :] = s_im
        s_re, s_im = (ar * s_re - ai * s_im + lre_ref[rows, :],
                      ar * s_im + ai * s_re + lim_ref[rows, :])
    y_ref[...] = (_dot(u, m_ref[...])
                  + _dot(pre_ref[...].astype(BF16), vre_ref[...])
                  + _dot(pim_ref[...].astype(BF16), vim_ref[...]))


def _ssm_scan(u_a, bsz, seq, prep):
    m, w_re, w_im, v_re, v_im, at_re, at_im = prep
    t_len = SSM_T
    nchunks = seq // t_len
    r = nchunks * bsz
    tw = t_len * SSM_GROUP
    u = u_a.reshape(bsz, nchunks, t_len, SSM_GROUPS, SSM_GROUP).transpose(3, 1, 0, 2, 4)
    u = u.reshape(SSM_GROUPS, r, tw).astype(BF16)
    grp = lambda g: (g, 0, 0)
    y = pl.pallas_call(
        functools.partial(_ssm_kernel, nb=bsz, nchunks=nchunks),
        grid=(SSM_GROUPS,),
        in_specs=[pl.BlockSpec((None, r, tw), grp),
                  pl.BlockSpec((None, tw, tw), grp),
                  pl.BlockSpec((None, tw, SSM_STATE), grp),
                  pl.BlockSpec((None, tw, SSM_STATE), grp),
                  pl.BlockSpec((None, SSM_STATE, tw), grp),
                  pl.BlockSpec((None, SSM_STATE, tw), grp),
                  pl.BlockSpec((None, 1, SSM_STATE), grp),
                  pl.BlockSpec((None, 1, SSM_STATE), grp)],
        out_specs=pl.BlockSpec((None, r, tw), grp),
        out_shape=jax.ShapeDtypeStruct((SSM_GROUPS, r, tw), F32),
        scratch_shapes=[pltpu.VMEM((r, SSM_STATE), F32) for _ in range(4)],
        compiler_params=_cparams("parallel"),
        name="ssm_scan",
    )(u, m, w_re, w_im, v_re, v_im, at_re, at_im)
    y = y.reshape(SSM_GROUPS, nchunks, bsz, t_len, SSM_GROUP).transpose(2, 1, 3, 0, 4)
    return y.reshape(bsz * seq, SSM_WIDTH)


def _dsa_kernel(q_ref, qi_ref, wi_ref, ki_ref, ckv_ref, wuk_ref, wuv_ref, tri_ref, o_ref,
                key_ref, dm_ref, *, topk, kb_len):
    tq = q_ref.shape[0]
    t0 = pl.program_id(1) * tq
    n_kb = (t0 + tq + kb_len - 1) // kb_len
    row = t0 + lax.broadcasted_iota(jnp.int32, (tq, 1), 0)
    row_chunk = row // CHUNK
    lane = lax.broadcasted_iota(jnp.int32, (1, kb_len), 1)
    n_sub = kb_len // LANES

    def fold_lanes(m):
        out = m[:, 0:LANES]
        for sb in range(1, n_sub):
            out = out + m[:, sb * LANES:(sb + 1) * LANES]
        return out

    qi_h = [qi_ref[:, h * IDX_DIM:(h + 1) * IDX_DIM] for h in range(IDX_HEADS)]
    w_h = [wi_ref[:, IDX_DIM + h:IDX_DIM + h + 1] * ((IDX_HEADS ** -0.5) * (IDX_DIM ** -0.5))
           for h in range(IDX_HEADS)]

    def score_block(kb, carry):
        k0 = pl.multiple_of(kb * kb_len, kb_len)
        ki = ki_ref[pl.ds(k0, kb_len), :]
        score = jnp.zeros((tq, kb_len), F32)
        for h in range(IDX_HEADS):
            score = score + jnp.maximum(_dot_nt(qi_h[h], ki), 0.0) * w_h[h]
        score = jnp.where(score == 0.0, 0.0, score)
        bits = lax.bitcast_convert_type(score, jnp.int32)
        key = jnp.where(bits < 0, bits ^ jnp.int32(0x7FFFFFFF), bits)
        adm = ((k0 + lane) // CHUNK) <= row_chunk
        key_ref[:, pl.ds(k0, kb_len)] = jnp.where(adm, key, jnp.int32(INT_MIN))
        return carry

    lax.fori_loop(0, n_kb, score_block, 0)

    def count(pred):
        def body(kb, acc):
            k0 = pl.multiple_of(kb * kb_len, kb_len)
            return acc + fold_lanes(pred(key_ref[:, pl.ds(k0, kb_len)]).astype(F32))
        acc = lax.fori_loop(0, n_kb, body, jnp.zeros((tq, LANES), F32))
        return jnp.sum(acc, axis=1, keepdims=True)

    kf = jnp.float32(topk)

    def bit_step(step, ans):
        trial = ans + jnp.left_shift(jnp.int32(1), 31 - step)
        return jnp.where(count(lambda kblk: kblk >= trial) >= kf, trial, ans)

    ans = lax.fori_loop(0, 32, bit_step, jnp.full((tq, 1), INT_MIN, jnp.int32))

    room = kf - count(lambda kblk: kblk > ans)
    tri = tri_ref[...]

    def select_block(kb, carry):
        k0 = pl.multiple_of(kb * kb_len, kb_len)
        for sb in range(n_sub):
            c0 = k0 + sb * LANES
            kblk = key_ref[:, pl.ds(c0, LANES)]
            eq = kblk == ans
            pre = _dot(eq.astype(BF16), tri) + carry
            sel = (kblk > ans) | (eq & (pre <= room))
            colb = c0 + lane[:, 0:LANES]
            adm = (colb // CHUNK) <= row_chunk
            dist = jnp.abs(row - colb).astype(F32)
            dm_ref[:, pl.ds(c0, LANES)] = jnp.where(sel & adm, dist, -NEG)
            carry = carry + jnp.sum(eq.astype(F32), axis=1, keepdims=True)
        return carry

    lax.fori_loop(0, n_kb, select_block, jnp.zeros((tq, 1), F32))

    for h in range(ATT_HEADS):
        hs = slice(h * ATT_HEAD_DIM, (h + 1) * ATT_HEAD_DIM)
        qa = (_dot(q_ref[:, hs], wuk_ref[h]) * (ATT_HEAD_DIM ** -0.5)).astype(BF16)
        slope = 2.0 ** (-8.0 * (h + 1) / ATT_HEADS)

        def att_block(kb, carry):
            m_run, l_run, acc = carry
            k0 = pl.multiple_of(kb * kb_len, kb_len)
            ckv = ckv_ref[pl.ds(k0, kb_len), :]
            lg = _dot_nt(qa, ckv) - slope * dm_ref[:, pl.ds(k0, kb_len)]
            m_new = jnp.maximum(m_run, jnp.max(lg, axis=1, keepdims=True))
            alpha = jnp.exp(m_run - m_new)
            p = jnp.exp(lg - m_new)
            l_new = alpha * l_run + jnp.sum(p, axis=1, keepdims=True)
            return m_new, l_new, alpha * acc + _dot(p.astype(BF16), ckv)

        init = (jnp.full((tq, 1), -jnp.inf, F32), jnp.zeros((tq, 1), F32), jnp.zeros((tq, KV_RANK), F32))
        _, l_fin, acc = lax.fori_loop(0, n_kb, att_block, init)
        o = acc / l_fin
        o_ref[:, hs] = _dot(o.astype(BF16), wuv_ref[h]).astype(o_ref.dtype)


def _dsa(q, qi, wi, ki, ckv, w_uk, w_uv, bsz, seq, tq=128, kb_len=512):
    n = q.shape[0]
    nqt = seq // tq
    topk = min(TOPK_MAX, seq // 4)
    tri = (jnp.arange(LANES)[:, None] <= jnp.arange(LANES)[None, :]).astype(BF16)
    qrow = lambda b, i: (b * nqt + i, 0)
    krow = lambda b, i: (b, 0)
    c3 = lambda b, i: (0, 0, 0)
    return pl.pallas_call(
        functools.partial(_dsa_kernel, topk=topk, kb_len=min(kb_len, seq)),
        grid=(bsz, nqt),
        in_specs=[pl.BlockSpec((tq, ATT_WIDTH), qrow),
                  pl.BlockSpec((tq, IDX_HEADS * IDX_DIM), qrow),
                  pl.BlockSpec((tq, LANES), qrow),
                  pl.BlockSpec((seq, IDX_DIM), krow),
                  pl.BlockSpec((seq, KV_RANK), krow),
                  pl.BlockSpec((ATT_HEADS, ATT_HEAD_DIM, KV_RANK), c3),
                  pl.BlockSpec((ATT_HEADS, KV_RANK, ATT_HEAD_DIM), c3),
                  pl.BlockSpec((LANES, LANES), lambda b, i: (0, 0))],
        out_specs=pl.BlockSpec((tq, ATT_WIDTH), qrow),
        out_shape=jax.ShapeDtypeStruct((n, ATT_WIDTH), BF16),
        scratch_shapes=[pltpu.VMEM((tq, seq), jnp.int32),
                        pltpu.VMEM((tq, seq), F32)],
        compiler_params=_cparams("parallel", "parallel"),
        name="dsa_attention",
    )(q, qi, wi, ki, ckv, w_uk.astype(BF16), w_uv.astype(BF16), tri)


def _out_kernel(ys_ref, ua_ref, yb_ref, yc_ref, x_ref, d_ref, wglu_ref, bglu_ref, wo_ref, g1_ref,
                n2_ref, sc_ref, sh_ref, xo_ref, h2_ref):
    y = ys_ref[...] + d_ref[...] * ua_ref[...]
    yg = jax.nn.gelu(y)
    z = _dot(yg.astype(BF16), wglu_ref[...]) + bglu_ref[...]
    ya = yg * jax.nn.sigmoid(z)
    a_w, b_w = SSM_WIDTH, SSM_WIDTH + GMLP_WIDTH
    mix = (_dot(ya.astype(BF16), wo_ref[0:a_w, :])
           + _dot(yb_ref[...], wo_ref[a_w:b_w, :])
           + _dot(yc_ref[...], wo_ref[b_w:, :]))
    xn = x_ref[...] + g1_ref[0] * mix
    xo_ref[...] = xn
    ms = jnp.mean(xn * xn, axis=-1, keepdims=True)
    h = xn * lax.rsqrt(ms + RMS_EPS) * n2_ref[...]
    h2_ref[...] = (h * (1.0 + sc_ref[0]) + sh_ref[0]).astype(BF16)


def _output_stage(ys, ua, yb, yc, x2, seq, d_skip, w_glu, b_glu, w_out, g1, n2g, sc2, sh2, tm=512):
    n, d = x2.shape
    tpb = seq // tm
    row = lambda i: (i, 0)
    const2 = lambda i: (0, 0)
    per_b = lambda i: (i // tpb, 0, 0)
    return pl.pallas_call(
        _out_kernel,
        grid=(n // tm,),
        in_specs=[pl.BlockSpec((tm, SSM_WIDTH), row),
                  pl.BlockSpec((tm, SSM_WIDTH), row),
                  pl.BlockSpec((tm, GMLP_WIDTH), row),
                  pl.BlockSpec((tm, ATT_WIDTH), row),
                  pl.BlockSpec((tm, d), row),
                  pl.BlockSpec((1, SSM_WIDTH), const2),
                  pl.BlockSpec((SSM_WIDTH, SSM_WIDTH), const2),
                  pl.BlockSpec((1, SSM_WIDTH), const2),
                  pl.BlockSpec((d, d), const2),
                  pl.BlockSpec((1, 1, d), per_b),
                  pl.BlockSpec((1, d), const2),
                  pl.BlockSpec((1, 1, d), per_b),
                  pl.BlockSpec((1, 1, d), per_b)],
        out_specs=[pl.BlockSpec((tm, d), row), pl.BlockSpec((tm, d), row)],
        out_shape=[jax.ShapeDtypeStruct((n, d), F32), jax.ShapeDtypeStruct((n, d), BF16)],
        compiler_params=_cparams("parallel"),
        name="out_proj_norm2",
    )(ys, ua, yb, yc, x2, d_skip, w_glu, b_glu, w_out, g1, n2g, sc2, sh2)


def _extract_top(x, n_top):
    tops = []
    for _ in range(n_top):
        mx = jnp.max(x, axis=0, keepdims=True)
        tops.append(mx)
        x = jnp.where(x == mx, -jnp.inf, x)
    return tops


def _peer_kernel(h2_ref, x_ref, g2_ref, fg_ref, wq_ref, kbd_ref, u_ref, vt_ref, o_ref,
                 st_ref, e1_ref, e2_ref, acc_ref, at_ref, ga_ref, v1_ref, v2_ref, *, final_norm):
    j = pl.program_id(1)
    tm = h2_ref.shape[0]
    te = u_ref.shape[0]
    n1 = te // PEER_N_KEYS
    nk = PEER_N_KEYS
    h2 = h2_ref[...]

    @pl.when(j == 0)
    def _():
        q = _dot(h2, wq_ref[...]).astype(BF16)
        st_ref[...] = _dot_nt(kbd_ref[...], q)

        def per_head(h, carry):
            base = pl.multiple_of(h * 2 * nk, 2 * nk)
            for lc in range(tm // LANES):
                ls = slice(lc * LANES, (lc + 1) * LANES)
                t1 = _extract_top(st_ref[pl.ds(base, nk), ls], PEER_TOPK + 1)
                t2 = _extract_top(st_ref[pl.ds(base + nk, nk), ls], PEER_TOPK + 1)
                for k in range(PEER_TOPK):
                    v1_ref[k:k + 1, :] = t1[k]
                    v2_ref[k:k + 1, :] = t2[k]
                v2_all = v2_ref[0:PEER_TOPK, :]
                v2_top = v2_ref[0:8, :]
                cand = [t1[0] + v2_all]
                cand += [t1[a] + v2_top for a in range(1, 8)]
                cand.append(v1_ref[8:PEER_TOPK, :] + t2[0])
                top = _extract_top(jnp.concatenate(cand, axis=0), PEER_TOPK + 1)
                z = jnp.zeros_like(top[0])
                for k in range(PEER_TOPK):
                    z = z + jnp.exp(top[k] - top[0])
                nxt = jnp.maximum(top[PEER_TOPK], jnp.maximum(t1[PEER_TOPK] + t2[0], t1[0] + t2[PEER_TOPK]))
                thr = 0.5 * (top[PEER_TOPK - 1] + nxt)
                s1 = st_ref[pl.ds(base, nk), ls]
                s2 = st_ref[pl.ds(base + nk, nk), ls]
                e1_ref[h, :, ls] = jnp.exp(s1 - t1[0]) * (0.5 / z)
                e2_ref[h, :, ls] = jnp.exp(s2 - t2[0])
                st_ref[pl.ds(base, nk), ls] = thr - s1
            return carry

        lax.fori_loop(0, PEER_HEADS, per_head, 0)
        acc_ref[...] = jnp.zeros_like(acc_ref)
        ga_ref[1] = jnp.zeros(ga_ref.shape[1:], BF16)

    n_blocks = pl.num_programs(1) - 1
    cur = j % 2

    @pl.when(j < n_blocks)
    def _():
        at_ref[...] = _dot_nt(u_ref[...], h2)
        acc_ref[...] += _dot(vt_ref[...], ga_ref[1 - cur])
        r0 = pl.multiple_of(j * n1, n1)
        for lc in range(tm // LANES):
            ls = slice(lc * LANES, (lc + 1) * LANES)
            for il in range(n1):
                g = jnp.zeros((nk, LANES), F32)
                for h in range(PEER_HEADS):
                    base = h * 2 * nk
                    c_row = st_ref[pl.ds(base + r0, n1), ls][il:il + 1, :]
                    e_row = e1_ref[h, pl.ds(r0, n1), ls][il:il + 1, :]
                    s2 = st_ref[base + nk:base + 2 * nk, ls]
                    g = g + jnp.where(s2 >= c_row, e2_ref[h, :, ls], 0.0) * e_row
                a = at_ref[il * nk:(il + 1) * nk, ls]
                inner = a * (0.7978845608028654 + 0.035677408136300125 * (a * a))
                act2 = a + a * jnp.tanh(inner)
                ga_ref[cur, il * nk:(il + 1) * nk, ls] = (g * act2).astype(BF16)

    @pl.when(j == n_blocks)
    def _():
        acc = acc_ref[...] + _dot(vt_ref[...], ga_ref[1 - cur])
        out = x_ref[...] + g2_ref[0] * acc.T
        if final_norm:
            ms = jnp.mean(out * out, axis=-1, keepdims=True)
            out = out * lax.rsqrt(ms + RMS_EPS) * fg_ref[...]
        o_ref[...] = out


def _peer(h2, x2, seq, g2, fg, wq, kbd, u_bf, vt_bf, final_norm, tm=512, te=1024):
    n, d = x2.shape
    n_blk = u_bf.shape[0] // te
    tpb = seq // tm
    trow = lambda i, j: (i, 0)
    c2 = lambda i, j: (0, 0)
    hs = (PEER_HEADS, PEER_N_KEYS, tm)
    return pl.pallas_call(
        functools.partial(_peer_kernel, final_norm=final_norm),
        grid=(n // tm, n_blk + 1),
        in_specs=[pl.BlockSpec((tm, d), trow),
                  pl.BlockSpec((tm, d), trow),
                  pl.BlockSpec((1, 1, d), lambda i, j: (i // tpb, 0, 0)),
                  pl.BlockSpec((1, d), c2),
                  pl.BlockSpec((d, PEER_HEADS * 2 * PEER_HALF), c2),
                  pl.BlockSpec((PEER_HEADS * 2 * PEER_N_KEYS, PEER_HEADS * 2 * PEER_HALF), c2),
                  pl.BlockSpec((te, d), lambda i, j: (jnp.minimum(j, n_blk - 1), 0)),
                  pl.BlockSpec((d, te), lambda i, j: (0, jnp.maximum(j - 1, 0)))],
        out_specs=pl.BlockSpec((tm, d), trow),
        out_shape=jax.ShapeDtypeStruct((n, d), F32),
        scratch_shapes=[pltpu.VMEM((PEER_HEADS * 2 * PEER_N_KEYS, tm), F32),
                        pltpu.VMEM(hs, F32), pltpu.VMEM(hs, F32),
                        pltpu.VMEM((d, tm), F32),
                        pltpu.VMEM((te, tm), F32),
                        pltpu.VMEM((2, te, tm), BF16),
                        pltpu.VMEM((PEER_TOPK, LANES), F32), pltpu.VMEM((PEER_TOPK, LANES), F32)],
        compiler_params=_cparams("parallel", "arbitrary"),
        name="peer_dense",
    )(h2, x2, g2, fg, wq, kbd, u_bf, vt_bf)


def _peer_key_matrix(k1, k2):
    blocks = []
    for h in range(PEER_HEADS):
        for half, kk in enumerate((k1, k2)):
            col = (2 * h + half) * PEER_HALF
            blocks.append(jnp.pad(kk, ((0, 0), (col, PEER_HEADS * 2 * PEER_HALF - col - PEER_HALF))))
    return jnp.concatenate(blocks, axis=0).astype(BF16)


def kernel(x, c, norm1_g, norm2_g, w_mod, b_mod, w_in, ssm_a_re_log, ssm_a_im, ssm_b_re, ssm_b_im, ssm_c_re, ssm_c_im, ssm_d, ssm_log_dt, ssm_w_glu, ssm_b_glu, gmlp_w_sp, gmlp_b_sp, kv_norm_g, w_uk, w_uv, w_out, peer_w_q, peer_k1, peer_k2, peer_u, peer_v, final_g):
    bsz, seq, d = x.shape
    depth = w_mod.shape[0]
    x2 = x.reshape(bsz * seq, d)
    head_of = jnp.arange(GMLP_WIDTH) // GMLP_HEAD_DIM
    pavg = ((head_of[:, None] == head_of[None, :]).astype(F32) / GMLP_HEAD_DIM).astype(BF16)
    chunk_of = jnp.arange(GMLP_BLOCK) // CHUNK
    sp_mask = chunk_of[:, None] >= chunk_of[None, :]
    for l in range(depth):
        mod = _modulation(c, w_mod[l], b_mod[l])
        sh1, sc1, g1, sh2, sc2, g2 = [mod[:, i * d:(i + 1) * d].reshape(bsz, 1, d) for i in range(6)]
        w_in_pad = jnp.pad(w_in[l], ((0, 0), (0, IN_PAD - IN_WIDTH))).astype(BF16)
        wsp = jnp.where(sp_mask[None], gmlp_w_sp[l], 0.0).astype(BF16)
        bsp = jnp.repeat(gmlp_b_sp[l].T, GMLP_HEAD_DIM, axis=1)
        ua, yb, q, ckv, qi, ki, wi = _input_stage(
            x2, seq, norm1_g[l].reshape(1, d), sc1, sh1, w_in_pad, kv_norm_g[l].reshape(1, KV_RANK),
            pavg, wsp, bsp)
        prep = _ssm_prep(ssm_a_re_log[l], ssm_a_im[l], ssm_b_re[l], ssm_b_im[l], ssm_c_re[l], ssm_c_im[l],
                         ssm_log_dt[l], SSM_T)
        ys = _ssm_scan(ua, bsz, seq, prep)
        yc = _dsa(q, qi, wi, ki, ckv, w_uk[l], w_uv[l], bsz, seq)
        x2, h2 = _output_stage(
            ys, ua, yb, yc, x2, seq, ssm_d[l].reshape(1, SSM_WIDTH), ssm_w_glu[l].astype(BF16),
            ssm_b_glu[l].reshape(1, SSM_WIDTH), w_out[l].astype(BF16), g1, norm2_g[l].reshape(1, d), sc2, sh2)
        x2 = _peer(h2, x2, seq, g2, final_g.reshape(1, d), peer_w_q[l].astype(BF16),
                   _peer_key_matrix(peer_k1[l], peer_k2[l]), peer_u[l].astype(BF16),
                   peer_v[l].T.astype(BF16), final_norm=(l == depth - 1))
    return x2.reshape(bsz, seq, d)
```

```python
import functools
import math

import jax
import jax.numpy as jnp
from jax import lax
from jax.experimental import pallas as pl
from jax.experimental.pallas import tpu as pltpu

CHUNK = 64
RMS_EPS = 1e-6

SSM_GROUP = 16
SSM_STATE = 64
SSM_WIDTH = 256
SSM_GROUPS = SSM_WIDTH // SSM_GROUP
SSM_T = 32
GMLP_HEADS = 4
GMLP_HEAD_DIM = 64
GMLP_WIDTH = GMLP_HEADS * GMLP_HEAD_DIM
GMLP_BLOCK = 128
ATT_HEADS = 8
ATT_HEAD_DIM = 64
ATT_WIDTH = ATT_HEADS * ATT_HEAD_DIM
KV_RANK = 128
IDX_HEADS = 4
IDX_DIM = 64
TOPK_MAX = 256

IN_SIZES = (SSM_WIDTH, GMLP_WIDTH, GMLP_WIDTH, ATT_WIDTH, KV_RANK, IDX_HEADS * IDX_DIM, IDX_DIM, IDX_HEADS)
IN_WIDTH = sum(IN_SIZES)
IN_PAD = 1792
COL_UB, COL_VB, COL_Q, COL_CKV, COL_QI, COL_KW = 256, 512, 768, 1280, 1408, 1664

PEER_HEADS = 8
PEER_N_KEYS = 128
PEER_HALF = 64
PEER_TOPK = 16

LANES = 128
VMEM_LIMIT = 56 * 1024 * 1024
NEG = -1e30
INT_MIN = -(2 ** 31)

F32 = jnp.float32
BF16 = jnp.bfloat16


def _cparams(*sem):
    return pltpu.CompilerParams(dimension_semantics=sem, vmem_limit_bytes=VMEM_LIMIT)


def _dot(a, b):
    return jnp.dot(a, b, preferred_element_type=F32)


def _dot_nt(a, b):
    return lax.dot_general(a, b, (((1,), (1,)), ((), ())), preferred_element_type=F32)


def _mod_kernel(c_ref, w_ref, b_ref, o_ref):
    c = c_ref[...]
    ca = c * jax.nn.sigmoid(c)
    o_ref[...] = _dot(ca.astype(BF16), w_ref[...].astype(BF16)) + b_ref[...]


def _modulation(c, w_mod, b_mod):
    bsz, d = c.shape
    n6 = w_mod.shape[1]
    return pl.pallas_call(
        _mod_kernel,
        grid=(n6 // d,),
        in_specs=[pl.BlockSpec((bsz, d), lambda j: (0, 0)),
                  pl.BlockSpec((d, d), lambda j: (0, j)),
                  pl.BlockSpec((1, d), lambda j: (0, j))],
        out_specs=pl.BlockSpec((bsz, d), lambda j: (0, j)),
        out_shape=jax.ShapeDtypeStruct((bsz, n6), F32),
        compiler_params=_cparams("parallel"),
        name="modulation",
    )(c, w_mod, b_mod.reshape(1, n6))


def _in_kernel(x_ref, g_ref, sc_ref, sh_ref, w_ref, kvg_ref, pavg_ref, wsp_ref, bsp_ref,
               ua_ref, yb_ref, q_ref, ckv_ref, qi_ref, ki_ref, wi_ref, ckvt_ref):
    tm = x_ref.shape[0]
    x = x_ref[...]
    ms = jnp.mean(x * x, axis=-1, keepdims=True)
    h = x * lax.rsqrt(ms + RMS_EPS) * g_ref[...]
    h = h * (1.0 + sc_ref[0]) + sh_ref[0]
    proj = _dot(h.astype(BF16), w_ref[...])

    ua_ref[...] = proj[:, 0:COL_UB]
    q_ref[...] = proj[:, COL_Q:COL_CKV].astype(BF16)
    ckv = proj[:, COL_CKV:COL_QI]
    ckv_ms = jnp.mean(ckv * ckv, axis=-1, keepdims=True)
    ckv_n = ckv * lax.rsqrt(ckv_ms + RMS_EPS) * kvg_ref[...]
    ckv_ref[...] = ckv_n.astype(BF16)
    ckvt_ref[...] = ckv_n.T.astype(BF16)
    qi_ref[...] = proj[:, COL_QI:COL_KW].astype(BF16)
    kw = proj[:, COL_KW:IN_PAD]
    ki_ref[...] = kw[:, 0:IDX_DIM].astype(BF16)
    wi_ref[...] = kw

    u_b = proj[:, COL_UB:COL_VB]
    v_b = proj[:, COL_VB:COL_Q]
    pavg = pavg_ref[...]

    def head_mean(a):
        hi = a.astype(BF16)
        lo = (a - hi.astype(F32)).astype(BF16)
        return _dot(hi, pavg) + _dot(lo, pavg)

    mu = head_mean(v_b)
    dv = v_b - mu
    var = head_mean(dv * dv)
    vn = (dv * lax.rsqrt(var + RMS_EPS)).astype(BF16)
    lane_head = lax.broadcasted_iota(jnp.int32, (GMLP_BLOCK, GMLP_WIDTH), 1) // GMLP_HEAD_DIM
    for blk in range(tm // GMLP_BLOCK):
        rows = slice(blk * GMLP_BLOCK, (blk + 1) * GMLP_BLOCK)
        vblk = vn[rows, :]
        mixed = bsp_ref[...]
        for hh in range(GMLP_HEADS):
            mixed = mixed + jnp.where(lane_head == hh, _dot(wsp_ref[hh], vblk), 0.0)
        yb_ref[rows, :] = (u_b[rows, :] * mixed).astype(BF16)


def _input_stage(x2, seq, g, sc, sh, w_in_pad, kvg, pavg, wsp, bsp, tm=512):
    n, d = x2.shape
    tpb = seq // tm
    row = lambda i: (i, 0)
    const2 = lambda i: (0, 0)
    per_b = lambda i: (i // tpb, 0, 0)
    outs = [(SSM_WIDTH, F32), (GMLP_WIDTH, BF16), (ATT_WIDTH, BF16), (KV_RANK, BF16),
            (IDX_HEADS * IDX_DIM, BF16), (IDX_DIM, BF16), (LANES, F32)]
    return pl.pallas_call(
        _in_kernel,
        grid=(n // tm,),
        in_specs=[pl.BlockSpec((tm, d), row),
                  pl.BlockSpec((1, d), const2),
                  pl.BlockSpec((1, 1, d), per_b),
                  pl.BlockSpec((1, 1, d), per_b),
                  pl.BlockSpec((d, IN_PAD), const2),
                  pl.BlockSpec((1, KV_RANK), const2),
                  pl.BlockSpec((GMLP_WIDTH, GMLP_WIDTH), const2),
                  pl.BlockSpec((GMLP_HEADS, GMLP_BLOCK, GMLP_BLOCK), lambda i: (0, 0, 0)),
                  pl.BlockSpec((GMLP_BLOCK, GMLP_WIDTH), const2)],
        out_specs=[pl.BlockSpec((tm, w), row) for w, _ in outs] + [pl.BlockSpec((KV_RANK, tm), lambda i: (0, i))],
        out_shape=[jax.ShapeDtypeStruct((n, w), dt) for w, dt in outs] + [jax.ShapeDtypeStruct((KV_RANK, n), BF16)],
        compiler_params=_cparams("parallel"),
        name="input_proj_gmlp",
    )(x2, g, sc, sh, w_in_pad, kvg, pavg, wsp, bsp)


def _ssm_prep(a_re_log, a_im, b_re, b_im, c_re, c_im, log_dt, t_len):
    hp = lax.Precision.HIGHEST
    lam_re = -jnp.exp(a_re_log)
    lam_im = a_im
    dt = jnp.exp(log_dt)[:, None]
    mag = jnp.exp(lam_re * dt)
    abar_re = mag * jnp.cos(lam_im * dt)
    abar_im = mag * jnp.sin(lam_im * dt)
    den = lam_re * lam_re + lam_im * lam_im
    p = abar_re - 1.0
    qq = abar_im
    f_re = (p * lam_re + qq * lam_im) / den
    f_im = (qq * lam_re - p * lam_im) / den
    bb_re = f_re[..., None] * b_re - f_im[..., None] * b_im
    bb_im = f_re[..., None] * b_im + f_im[..., None] * b_re
    k = jnp.arange(t_len + 1, dtype=F32)[:, None, None]
    pmag = jnp.exp(k * (lam_re * dt))
    ang = k * (lam_im * dt)
    pw_re = pmag * jnp.cos(ang)
    pw_im = pmag * jnp.sin(ang)
    cp_re = c_re[None] * pw_re[:, :, None, :] - c_im[None] * pw_im[:, :, None, :]
    cp_im = c_re[None] * pw_im[:, :, None, :] + c_im[None] * pw_re[:, :, None, :]
    kern = (jnp.einsum('kghp,gpj->kghj', cp_re[:t_len], bb_re, precision=hp)
            - jnp.einsum('kghp,gpj->kghj', cp_im[:t_len], bb_im, precision=hp))
    tt = jnp.arange(t_len)
    lag = tt[None, :] - tt[:, None]
    toep = jnp.where((lag >= 0)[:, :, None, None, None], kern[jnp.clip(lag, 0)], 0.0)
    g_n, h_n = SSM_GROUPS, SSM_GROUP
    m = toep.transpose(2, 0, 4, 1, 3).reshape(g_n, t_len * h_n, t_len * h_n)
    rev_re = pw_re[t_len - 1 - tt]
    rev_im = pw_im[t_len - 1 - tt]
    w_re = rev_re[..., None] * bb_re[None] - rev_im[..., None] * bb_im[None]
    w_im = rev_re[..., None] * bb_im[None] + rev_im[..., None] * bb_re[None]
    w_re = w_re.transpose(1, 0, 3, 2).reshape(g_n, t_len * h_n, SSM_STATE)
    w_im = w_im.transpose(1, 0, 3, 2).reshape(g_n, t_len * h_n, SSM_STATE)
    v_re = cp_re[1:].transpose(1, 3, 0, 2).reshape(g_n, SSM_STATE, t_len * h_n)
    v_im = (-cp_im[1:]).transpose(1, 3, 0, 2).reshape(g_n, SSM_STATE, t_len * h_n)
    at_re = pw_re[t_len][:, None, :]
    at_im = pw_im[t_len][:, None, :]
    return (m.astype(BF16), w_re.astype(BF16), w_im.astype(BF16), v_re.astype(BF16), v_im.astype(BF16),
            at_re, at_im)


def _ssm_kernel(u_ref, m_ref, wre_ref, wim_ref, vre_ref, vim_ref, are_ref, aim_ref, y_ref,
                lre_ref, lim_ref, pre_ref, pim_ref, *, nb, nchunks):
    u = u_ref[...]
    lre_ref[...] = _dot(u, wre_ref[...])
    lim_ref[...] = _dot(u, wim_ref[...])
    ar = are_ref[...]
    ai = aim_ref[...]
    s_re = jnp.zeros((nb, SSM_STATE), F32)
    s_im = jnp.zeros((nb, SSM_STATE), F32)
    for c in range(nchunks):
        rows = slice(c * nb, (c + 1) * nb)
        pre_ref[rows, :] = s_re
        pim_ref[rows, :] = s_im
        s_re, s_im = (ar * s_re - ai * s_im + lre_ref[rows, :],
                      ar * s_im + ai * s_re + lim_ref[rows, :])
    y_ref[...] = (_dot(u, m_ref[...])
                  + _dot(pre_ref[...].astype(BF16), vre_ref[...])
                  + _dot(pim_ref[...].astype(BF16), vim_ref[...]))


def _ssm_scan(u_a, bsz, seq, prep):
    m, w_re, w_im, v_re, v_im, at_re, at_im = prep
    t_len = SSM_T
    nchunks = seq // t_len
    r = nchunks * bsz
    tw = t_len * SSM_GROUP
    u = u_a.reshape(bsz, nchunks, t_len, SSM_GROUPS, SSM_GROUP).transpose(3, 1, 0, 2, 4)
    u = u.reshape(SSM_GROUPS, r, tw).astype(BF16)
    grp = lambda g: (g, 0, 0)
    y = pl.pallas_call(
        functools.partial(_ssm_kernel, nb=bsz, nchunks=nchunks),
        grid=(SSM_GROUPS,),
        in_specs=[pl.BlockSpec((None, r, tw), grp),
                  pl.BlockSpec((None, tw, tw), grp),
                  pl.BlockSpec((None, tw, SSM_STATE), grp),
                  pl.BlockSpec((None, tw, SSM_STATE), grp),
                  pl.BlockSpec((None, SSM_STATE, tw), grp),
                  pl.BlockSpec((None, SSM_STATE, tw), grp),
                  pl.BlockSpec((None, 1, SSM_STATE), grp),
                  pl.BlockSpec((None, 1, SSM_STATE), grp)],
        out_specs=pl.BlockSpec((None, r, tw), grp),
        out_shape=jax.ShapeDtypeStruct((SSM_GROUPS, r, tw), F32),
        scratch_shapes=[pltpu.VMEM((r, SSM_STATE), F32) for _ in range(4)],
        compiler_params=_cparams("parallel"),
        name="ssm_scan",
    )(u, m, w_re, w_im, v_re, v_im, at_re, at_im)
    y = y.reshape(SSM_GROUPS, nchunks, bsz, t_len, SSM_GROUP).transpose(2, 1, 3, 0, 4)
    return y.reshape(bsz * seq, SSM_WIDTH)


def _dsa_kernel(q_ref, qi_ref, wi_ref, ki_ref, ckv_ref, ckvt_ref, wuk_ref, wuvt_ref, tri_ref, o_ref,
                key_ref, dm_ref, qa_ref, ot_ref, *acc_refs, topk, kb_len):
    tq = q_ref.shape[0]
    t0 = pl.program_id(1) * tq
    n_kb = (t0 + tq + kb_len - 1) // kb_len
    q_pos = t0 + lax.broadcasted_iota(jnp.int32, (1, tq), 1)
    q_chunk = q_pos // CHUNK
    k_off = lax.broadcasted_iota(jnp.int32, (kb_len, 1), 0)

    def fold8(m, op):
        r = m.reshape(kb_len // 8, 8, tq)
        n = kb_len // 8
        while n > 1:
            n //= 2
            r = op(r[:n], r[n:2 * n])
        return r[0]

    def col_sum(m):
        return fold8(m, jnp.add)

    wi_t = wi_ref[...].T
    qi_h = [qi_ref[:, h * IDX_DIM:(h + 1) * IDX_DIM] for h in range(IDX_HEADS)]
    w_h = [wi_t[IDX_DIM + h:IDX_DIM + h + 1, :] * ((IDX_HEADS ** -0.5) * (IDX_DIM ** -0.5))
           for h in range(IDX_HEADS)]

    def score_block(kb, carry):
        k0 = pl.multiple_of(kb * kb_len, kb_len)
        ki = ki_ref[pl.ds(k0, kb_len), :]
        score = jnp.zeros((kb_len, tq), F32)
        for h in range(IDX_HEADS):
            score = score + jnp.maximum(_dot_nt(ki, qi_h[h]), 0.0) * w_h[h]
        score = jnp.where(score == 0.0, 0.0, score)
        bits = lax.bitcast_convert_type(score, jnp.int32)
        key = jnp.where(bits < 0, bits ^ jnp.int32(0x7FFFFFFF), bits)
        adm = ((k0 + k_off) // CHUNK) <= q_chunk
        key_ref[pl.ds(k0, kb_len), :] = jnp.where(adm, key, jnp.int32(INT_MIN))
        return carry

    lax.fori_loop(0, n_kb, score_block, 0)

    def count(pred):
        def body(kb, acc):
            k0 = pl.multiple_of(kb * kb_len, kb_len)
            return acc + col_sum(pred(key_ref[pl.ds(k0, kb_len), :]).astype(F32))
        acc = lax.fori_loop(0, n_kb, body, jnp.zeros((8, tq), F32))
        return jnp.sum(acc, axis=0, keepdims=True)

    kf = jnp.float32(topk)

    def bit_step(step, ans):
        trial = ans + jnp.left_shift(jnp.int32(1), 31 - step)
        return jnp.where(count(lambda kblk: kblk >= trial) >= kf, trial, ans)

    ans = lax.fori_loop(0, 32, bit_step, jnp.full((1, tq), INT_MIN, jnp.int32))

    room = kf - count(lambda kblk: kblk > ans)
    tri = tri_ref[...]

    def select_block(kb, carry):
        k0 = pl.multiple_of(kb * kb_len, kb_len)
        kblk = key_ref[pl.ds(k0, kb_len), :]
        eq = kblk == ans
        pre = _dot(tri, eq.astype(BF16)) + carry
        sel = (kblk > ans) | (eq & (pre <= room))
        k_pos = k0 + k_off
        adm = (k_pos // CHUNK) <= q_chunk
        dist = jnp.abs(q_pos - k_pos).astype(F32)
        dm_ref[pl.ds(k0, kb_len), :] = jnp.where(sel & adm, dist, -NEG)
        return carry + jnp.sum(col_sum(eq.astype(F32)), axis=0, keepdims=True)

    lax.fori_loop(0, n_kb, select_block, jnp.zeros((1, tq), F32))

    for h in range(ATT_HEADS):
        hs = slice(h * ATT_HEAD_DIM, (h + 1) * ATT_HEAD_DIM)
        qa_ref[h] = (_dot(q_ref[:, hs], wuk_ref[h]) * (ATT_HEAD_DIM ** -0.5)).astype(BF16)
        acc_refs[h][...] = jnp.zeros((KV_RANK, tq), F32)

    def att_block(kb, carry):
        k0 = pl.multiple_of(kb * kb_len, kb_len)
        ckv = ckv_ref[pl.ds(k0, kb_len), :]
        ckvt = ckvt_ref[:, pl.ds(k0, kb_len)]
        dm = dm_ref[pl.ds(k0, kb_len), :]
        new = []
        qk = _dot_nt(ckv, qa_ref[0])
        for h in range(ATT_HEADS):
            m_run, l_run = carry[2 * h], carry[2 * h + 1]
            slope = 2.0 ** (-8.0 * (h + 1) / ATT_HEADS)
            lg = qk - slope * dm
            if h + 1 < ATT_HEADS:
                qk = _dot_nt(ckv, qa_ref[h + 1])
            m_new = jnp.maximum(m_run, jnp.max(fold8(lg, jnp.maximum), axis=0, keepdims=True))
            alpha = jnp.exp(m_run - m_new)
            p = jnp.exp(lg - m_new)
            new += [m_new, alpha * l_run + jnp.sum(col_sum(p), axis=0, keepdims=True)]
            acc_refs[h][...] = alpha * acc_refs[h][...] + _dot(ckvt, p.astype(BF16))
        return tuple(new)

    init = tuple(jnp.full((1, tq), -jnp.inf if i % 2 == 0 else 0.0, F32) for i in range(2 * ATT_HEADS))
    fin = lax.fori_loop(0, n_kb, att_block, init)
    for h in range(ATT_HEADS):
        o_t = (acc_refs[h][...] / fin[2 * h + 1]).astype(BF16)
        ot_ref[h * ATT_HEAD_DIM:(h + 1) * ATT_HEAD_DIM, :] = _dot(wuvt_ref[h], o_t)
    o_ref[...] = ot_ref[...].T.astype(o_ref.dtype)


def _dsa(q, qi, wi, ki, ckv, ckvt, w_uk, w_uv, bsz, seq, tq=256, kb_len=512):
    n = q.shape[0]
    nqt = seq // tq
    kb_len = min(kb_len, seq)
    topk = min(TOPK_MAX, seq // 4)
    tri = (jnp.arange(kb_len)[:, None] >= jnp.arange(kb_len)[None, :]).astype(BF16)
    qrow = lambda b, i: (b * nqt + i, 0)
    krow = lambda b, i: (b, 0)
    c3 = lambda b, i: (0, 0, 0)
    return pl.pallas_call(
        functools.partial(_dsa_kernel, topk=topk, kb_len=kb_len),
        grid=(bsz, nqt),
        in_specs=[pl.BlockSpec((tq, ATT_WIDTH), qrow),
                  pl.BlockSpec((tq, IDX_HEADS * IDX_DIM), qrow),
                  pl.BlockSpec((tq, LANES), qrow),
                  pl.BlockSpec((seq, IDX_DIM), krow),
                  pl.BlockSpec((seq, KV_RANK), krow),
                  pl.BlockSpec((KV_RANK, seq), lambda b, i: (0, b)),
                  pl.BlockSpec((ATT_HEADS, ATT_HEAD_DIM, KV_RANK), c3),
                  pl.BlockSpec((ATT_HEADS, ATT_HEAD_DIM, KV_RANK), c3),
                  pl.BlockSpec((kb_len, kb_len), lambda b, i: (0, 0))],
        out_specs=pl.BlockSpec((tq, ATT_WIDTH), qrow),
        out_shape=jax.ShapeDtypeStruct((n, ATT_WIDTH), BF16),
        scratch_shapes=[pltpu.VMEM((seq, tq), jnp.int32),
                        pltpu.VMEM((seq, tq), F32),
                        pltpu.VMEM((ATT_HEADS, tq, KV_RANK), BF16),
                        pltpu.VMEM((ATT_WIDTH, tq), F32)]
        + [pltpu.VMEM((KV_RANK, tq), F32) for _ in range(ATT_HEADS)],
        compiler_params=_cparams("parallel", "parallel"),
        name="dsa_attention",
    )(q, qi, wi, ki, ckv, ckvt, w_uk.astype(BF16), jnp.swapaxes(w_uv, 1, 2).astype(BF16), tri)


def _out_kernel(ys_ref, ua_ref, yb_ref, yc_ref, x_ref, d_ref, wglu_ref, bglu_ref, wo_ref, g1_ref,
                n2_ref, sc_ref, sh_ref, xo_ref, h2_ref):
    y = ys_ref[...] + d_ref[...] * ua_ref[...]
    yg = jax.nn.gelu(y)
    z = _dot(yg.astype(BF16), wglu_ref[...]) + bglu_ref[...]
    ya = yg * jax.nn.sigmoid(z)
    a_w, b_w = SSM_WIDTH, SSM_WIDTH + GMLP_WIDTH
    mix = (_dot(ya.astype(BF16), wo_ref[0:a_w, :])
           + _dot(yb_ref[...], wo_ref[a_w:b_w, :])
           + _dot(yc_ref[...], wo_ref[b_w:, :]))
    xn = x_ref[...] + g1_ref[0] * mix
    xo_ref[...] = xn
    ms = jnp.mean(xn * xn, axis=-1, keepdims=True)
    h = xn * lax.rsqrt(ms + RMS_EPS) * n2_ref[...]
    h2_ref[...] = (h * (1.0 + sc_ref[0]) + sh_ref[0]).astype(BF16)


def _output_stage(ys, ua, yb, yc, x2, seq, d_skip, w_glu, b_glu, w_out, g1, n2g, sc2, sh2, tm=512):
    n, d = x2.shape
    tpb = seq // tm
    row = lambda i: (i, 0)
    const2 = lambda i: (0, 0)
    per_b = lambda i: (i // tpb, 0, 0)
    return pl.pallas_call(
        _out_kernel,
        grid=(n // tm,),
        in_specs=[pl.BlockSpec((tm, SSM_WIDTH), row),
                  pl.BlockSpec((tm, SSM_WIDTH), row),
                  pl.BlockSpec((tm, GMLP_WIDTH), row),
                  pl.BlockSpec((tm, ATT_WIDTH), row),
                  pl.BlockSpec((tm, d), row),
                  pl.BlockSpec((1, SSM_WIDTH), const2),
                  pl.BlockSpec((SSM_WIDTH, SSM_WIDTH), const2),
                  pl.BlockSpec((1, SSM_WIDTH), const2),
                  pl.BlockSpec((d, d), const2),
                  pl.BlockSpec((1, 1, d), per_b),
                  pl.BlockSpec((1, d), const2),
                  pl.BlockSpec((1, 1, d), per_b),
                  pl.BlockSpec((1, 1, d), per_b)],
        out_specs=[pl.BlockSpec((tm, d), row), pl.BlockSpec((tm, d), row)],
        out_shape=[jax.ShapeDtypeStruct((n, d), F32), jax.ShapeDtypeStruct((n, d), BF16)],
        compiler_params=_cparams("parallel"),
        name="out_proj_norm2",
    )(ys, ua, yb, yc, x2, d_skip, w_glu, b_glu, w_out, g1, n2g, sc2, sh2)


def _extract_top(x, n_top):
    tops = []
    for _ in range(n_top):
        mx = jnp.max(x, axis=0, keepdims=True)
        tops.append(mx)
        x = jnp.where(x == mx, -jnp.inf, x)
    return tops


def _peer_kernel(h2_ref, x_ref, g2_ref, fg_ref, wq_ref, kbd_ref, u_ref, vt_ref, o_ref,
                 st_ref, e1_ref, e2_ref, acc_ref, at_ref, ga_ref, v1_ref, v2_ref, *, final_norm):
    j = pl.program_id(1)
    tm = h2_ref.shape[0]
    te = u_ref.shape[0]
    n1 = te // PEER_N_KEYS
    nk = PEER_N_KEYS
    h2 = h2_ref[...]

    @pl.when(j == 0)
    def _():
        q = _dot(h2, wq_ref[...]).astype(BF16)
        st_ref[...] = _dot_nt(kbd_ref[...], q)

        def per_head(h, carry):
            base = pl.multiple_of(h * 2 * nk, 2 * nk)
            for lc in range(tm // LANES):
                ls = slice(lc * LANES, (lc + 1) * LANES)
                t1 = _extract_top(st_ref[pl.ds(base, nk), ls], PEER_TOPK + 1)
                t2 = _extract_top(st_ref[pl.ds(base + nk, nk), ls], PEER_TOPK + 1)
                for k in range(PEER_TOPK):
                    v1_ref[k:k + 1, :] = t1[k]
                    v2_ref[k:k + 1, :] = t2[k]
                v2_all = v2_ref[0:PEER_TOPK, :]
                v2_top = v2_ref[0:8, :]
                cand = [t1[0] + v2_all]
                cand += [t1[a] + v2_top for a in range(1, 8)]
                cand.append(v1_ref[8:PEER_TOPK, :] + t2[0])
                top = _extract_top(jnp.concatenate(cand, axis=0), PEER_TOPK + 1)
                z = jnp.zeros_like(top[0])
                for k in range(PEER_TOPK):
                    z = z + jnp.exp(top[k] - top[0])
                nxt = jnp.maximum(top[PEER_TOPK], jnp.maximum(t1[PEER_TOPK] + t2[0], t1[0] + t2[PEER_TOPK]))
                thr = 0.5 * (top[PEER_TOPK - 1] + nxt)
                s1 = st_ref[pl.ds(base, nk), ls]
                s2 = st_ref[pl.ds(base + nk, nk), ls]
                e1_ref[h, :, ls] = jnp.exp(s1 - t1[0]) * (0.5 / z)
                e2_ref[h, :, ls] = jnp.exp(s2 - t2[0])
                st_ref[pl.ds(base, nk), ls] = thr - s1
            return carry

        lax.fori_loop(0, PEER_HEADS, per_head, 0)
        acc_ref[...] = jnp.zeros_like(acc_ref)
        ga_ref[1] = jnp.zeros(ga_ref.shape[1:], BF16)

    n_blocks = pl.num_programs(1) - 1
    cur = j % 2

    @pl.when(j < n_blocks)
    def _():
        at_ref[...] = _dot_nt(u_ref[...], h2)
        acc_ref[...] += _dot(vt_ref[...], ga_ref[1 - cur])
        r0 = pl.multiple_of(j * n1, n1)
        for lc in range(tm // LANES):
            ls = slice(lc * LANES, (lc + 1) * LANES)
            for il in range(n1):
                g = jnp.zeros((nk, LANES), F32)
                for h in range(PEER_HEADS):
                    base = h * 2 * nk
                    c_row = st_ref[pl.ds(base + r0, n1), ls][il:il + 1, :]
                    e_row = e1_ref[h, pl.ds(r0, n1), ls][il:il + 1, :]
                    s2 = st_ref[base + nk:base + 2 * nk, ls]
                    g = g + jnp.where(s2 >= c_row, e2_ref[h, :, ls], 0.0) * e_row
                a = at_ref[il * nk:(il + 1) * nk, ls]
                inner = a * (0.7978845608028654 + 0.035677408136300125 * (a * a))
                act2 = a + a * jnp.tanh(inner)
                ga_ref[cur, il * nk:(il + 1) * nk, ls] = (g * act2).astype(BF16)

    @pl.when(j == n_blocks)
    def _():
        acc = acc_ref[...] + _dot(vt_ref[...], ga_ref[1 - cur])
        out = x_ref[...] + g2_ref[0] * acc.T
        if final_norm:
            ms = jnp.mean(out * out, axis=-1, keepdims=True)
            out = out * lax.rsqrt(ms + RMS_EPS) * fg_ref[...]
        o_ref[...] = out


def _peer(h2, x2, seq, g2, fg, wq, kbd, u_bf, vt_bf, final_norm, tm=512, te=1024):
    n, d = x2.shape
    n_blk = u_bf.shape[0] // te
    tpb = seq // tm
    trow = lambda i, j: (i, 0)
    c2 = lambda i, j: (0, 0)
    hs = (PEER_HEADS, PEER_N_KEYS, tm)
    return pl.pallas_call(
        functools.partial(_peer_kernel, final_norm=final_norm),
        grid=(n // tm, n_blk + 1),
        in_specs=[pl.BlockSpec((tm, d), trow),
                  pl.BlockSpec((tm, d), trow),
                  pl.BlockSpec((1, 1, d), lambda i, j: (i // tpb, 0, 0)),
                  pl.BlockSpec((1, d), c2),
                  pl.BlockSpec((d, PEER_HEADS * 2 * PEER_HALF), c2),
                  pl.BlockSpec((PEER_HEADS * 2 * PEER_N_KEYS, PEER_HEADS * 2 * PEER_HALF), c2),
                  pl.BlockSpec((te, d), lambda i, j: (jnp.minimum(j, n_blk - 1), 0)),
                  pl.BlockSpec((d, te), lambda i, j: (0, jnp.maximum(j - 1, 0)))],
        out_specs=pl.BlockSpec((tm, d), trow),
        out_shape=jax.ShapeDtypeStruct((n, d), F32),
        scratch_shapes=[pltpu.VMEM((PEER_HEADS * 2 * PEER_N_KEYS, tm), F32),
                        pltpu.VMEM(hs, F32), pltpu.VMEM(hs, F32),
                        pltpu.VMEM((d, tm), F32),
                        pltpu.VMEM((te, tm), F32),
                        pltpu.VMEM((2, te, tm), BF16),
                        pltpu.VMEM((PEER_TOPK, LANES), F32), pltpu.VMEM((PEER_TOPK, LANES), F32)],
        compiler_params=_cparams("parallel", "arbitrary"),
        name="peer_dense",
    )(h2, x2, g2, fg, wq, kbd, u_bf, vt_bf)


def _peer_key_matrix(k1, k2):
    blocks = []
    for h in range(PEER_HEADS):
        for half, kk in enumerate((k1, k2)):
            col = (2 * h + half) * PEER_HALF
            blocks.append(jnp.pad(kk, ((0, 0), (col, PEER_HEADS * 2 * PEER_HALF - col - PEER_HALF))))
    return jnp.concatenate(blocks, axis=0).astype(BF16)


def kernel(x, c, norm1_g, norm2_g, w_mod, b_mod, w_in, ssm_a_re_log, ssm_a_im, ssm_b_re, ssm_b_im, ssm_c_re, ssm_c_im, ssm_d, ssm_log_dt, ssm_w_glu, ssm_b_glu, gmlp_w_sp, gmlp_b_sp, kv_norm_g, w_uk, w_uv, w_out, peer_w_q, peer_k1, peer_k2, peer_u, peer_v, final_g):
    bsz, seq, d = x.shape
    depth = w_mod.shape[0]
    x2 = x.reshape(bsz * seq, d)
    head_of = jnp.arange(GMLP_WIDTH) // GMLP_HEAD_DIM
    pavg = ((head_of[:, None] == head_of[None, :]).astype(F32) / GMLP_HEAD_DIM).astype(BF16)
    chunk_of = jnp.arange(GMLP_BLOCK) // CHUNK
    sp_mask = chunk_of[:, None] >= chunk_of[None, :]
    for l in range(depth):
        mod = _modulation(c, w_mod[l], b_mod[l])
        sh1, sc1, g1, sh2, sc2, g2 = [mod[:, i * d:(i + 1) * d].reshape(bsz, 1, d) for i in range(6)]
        w_in_pad = jnp.pad(w_in[l], ((0, 0), (0, IN_PAD - IN_WIDTH))).astype(BF16)
        wsp = jnp.where(sp_mask[None], gmlp_w_sp[l], 0.0).astype(BF16)
        bsp = jnp.repeat(gmlp_b_sp[l].T, GMLP_HEAD_DIM, axis=1)
        ua, yb, q, ckv, qi, ki, wi, ckvt = _input_stage(
            x2, seq, norm1_g[l].reshape(1, d), sc1, sh1, w_in_pad, kv_norm_g[l].reshape(1, KV_RANK),
            pavg, wsp, bsp)
        prep = _ssm_prep(ssm_a_re_log[l], ssm_a_im[l], ssm_b_re[l], ssm_b_im[l], ssm_c_re[l], ssm_c_im[l],
                         ssm_log_dt[l], SSM_T)
        ys = _ssm_scan(ua, bsz, seq, prep)
        yc = _dsa(q, qi, wi, ki, ckv, ckvt, w_uk[l], w_uv[l], bsz, seq)
        x2, h2 = _output_stage(
            ys, ua, yb, yc, x2, seq, ssm_d[l].reshape(1, SSM_WIDTH), ssm_w_glu[l].astype(BF16),
            ssm_b_glu[l].reshape(1, SSM_WIDTH), w_out[l].astype(BF16), g1, norm2_g[l].reshape(1, d), sc2, sh2)
        x2 = _peer(h2, x2, seq, g2, final_g.reshape(1, d), peer_w_q[l].astype(BF16),
                   _peer_key_matrix(peer_k1[l], peer_k2[l]), peer_u[l].astype(BF16),
                   peer_v[l].T.astype(BF16), final_norm=(l == depth - 1))
    return x2.reshape(bsz, seq, d)
```

```python
import functools
import math

import jax
import jax.numpy as jnp
from jax import lax
from jax.experimental import pallas as pl
from jax.experimental.pallas import tpu as pltpu

CHUNK = 64
RMS_EPS = 1e-6

SSM_GROUP = 16
SSM_STATE = 64
SSM_WIDTH = 256
SSM_GROUPS = SSM_WIDTH // SSM_GROUP
SSM_T = 32
GMLP_HEADS = 4
GMLP_HEAD_DIM = 64
GMLP_WIDTH = GMLP_HEADS * GMLP_HEAD_DIM
GMLP_BLOCK = 128
ATT_HEADS = 8
ATT_HEAD_DIM = 64
ATT_WIDTH = ATT_HEADS * ATT_HEAD_DIM
KV_RANK = 128
IDX_HEADS = 4
IDX_DIM = 64
TOPK_MAX = 256

IN_SIZES = (SSM_WIDTH, GMLP_WIDTH, GMLP_WIDTH, ATT_WIDTH, KV_RANK, IDX_HEADS * IDX_DIM, IDX_DIM, IDX_HEADS)
IN_WIDTH = sum(IN_SIZES)
IN_PAD = 1792
COL_UB, COL_VB, COL_Q, COL_CKV, COL_QI, COL_KW = 256, 512, 768, 1280, 1408, 1664

PEER_HEADS = 8
PEER_N_KEYS = 128
PEER_HALF = 64
PEER_TOPK = 16

LANES = 128
VMEM_LIMIT = 56 * 1024 * 1024
NEG = -1e30
INT_MIN = -(2 ** 31)

F32 = jnp.float32
BF16 = jnp.bfloat16


def _cparams(*sem):
    return pltpu.CompilerParams(dimension_semantics=sem, vmem_limit_bytes=VMEM_LIMIT)


def _dot(a, b):
    return jnp.dot(a, b, preferred_element_type=F32)


def _dot_nt(a, b):
    return lax.dot_general(a, b, (((1,), (1,)), ((), ())), preferred_element_type=F32)


def _mod_kernel(c_ref, w_ref, b_ref, o_ref):
    c = c_ref[...]
    ca = c * jax.nn.sigmoid(c)
    o_ref[...] = _dot(ca.astype(BF16), w_ref[...].astype(BF16)) + b_ref[...]


def _modulation(c, w_mod, b_mod):
    bsz, d = c.shape
    n6 = w_mod.shape[1]
    return pl.pallas_call(
        _mod_kernel,
        grid=(n6 // d,),
        in_specs=[pl.BlockSpec((bsz, d), lambda j: (0, 0)),
                  pl.BlockSpec((d, d), lambda j: (0, j)),
                  pl.BlockSpec((1, d), lambda j: (0, j))],
        out_specs=pl.BlockSpec((bsz, d), lambda j: (0, j)),
        out_shape=jax.ShapeDtypeStruct((bsz, n6), F32),
        compiler_params=_cparams("parallel"),
        name="modulation",
    )(c, w_mod, b_mod.reshape(1, n6))


def _in_kernel(x_ref, g_ref, sc_ref, sh_ref, w_ref, kvg_ref, pavg_ref, wsp_ref, bsp_ref,
               ua_ref, yb_ref, q_ref, ckv_ref, qi_ref, ki_ref, wi_ref, ckvt_ref):
    tm = x_ref.shape[0]
    x = x_ref[...]
    ms = jnp.mean(x * x, axis=-1, keepdims=True)
    h = x * lax.rsqrt(ms + RMS_EPS) * g_ref[...]
    h = h * (1.0 + sc_ref[0]) + sh_ref[0]
    proj = _dot(h.astype(BF16), w_ref[...])

    ua_ref[...] = proj[:, 0:COL_UB]
    q_ref[...] = proj[:, COL_Q:COL_CKV].astype(BF16)
    ckv = proj[:, COL_CKV:COL_QI]
    ckv_ms = jnp.mean(ckv * ckv, axis=-1, keepdims=True)
    ckv_n = ckv * lax.rsqrt(ckv_ms + RMS_EPS) * kvg_ref[...]
    ckv_ref[...] = ckv_n.astype(BF16)
    ckvt_ref[...] = ckv_n.T.astype(BF16)
    qi_ref[...] = proj[:, COL_QI:COL_KW].astype(BF16)
    kw = proj[:, COL_KW:IN_PAD]
    ki_ref[...] = kw[:, 0:IDX_DIM].astype(BF16)
    wi_ref[...] = kw

    u_b = proj[:, COL_UB:COL_VB]
    v_b = proj[:, COL_VB:COL_Q]
    pavg = pavg_ref[...]

    def head_mean(a):
        hi = a.astype(BF16)
        lo = (a - hi.astype(F32)).astype(BF16)
        return _dot(hi, pavg) + _dot(lo, pavg)

    mu = head_mean(v_b)
    dv = v_b - mu
    var = head_mean(dv * dv)
    vn = (dv * lax.rsqrt(var + RMS_EPS)).astype(BF16)
    lane_head = lax.broadcasted_iota(jnp.int32, (GMLP_BLOCK, GMLP_WIDTH), 1) // GMLP_HEAD_DIM
    for blk in range(tm // GMLP_BLOCK):
        rows = slice(blk * GMLP_BLOCK, (blk + 1) * GMLP_BLOCK)
        vblk = vn[rows, :]
        mixed = bsp_ref[...]
        for hh in range(GMLP_HEADS):
            mixed = mixed + jnp.where(lane_head == hh, _dot(wsp_ref[hh], vblk), 0.0)
        yb_ref[rows, :] = (u_b[rows, :] * mixed).astype(BF16)


def _input_stage(x2, seq, g, sc, sh, w_in_pad, kvg, pavg, wsp, bsp, tm=512):
    n, d = x2.shape
    tpb = seq // tm
    row = lambda i: (i, 0)
    const2 = lambda i: (0, 0)
    per_b = lambda i: (i // tpb, 0, 0)
    outs = [(SSM_WIDTH, F32), (GMLP_WIDTH, BF16), (ATT_WIDTH, BF16), (KV_RANK, BF16),
            (IDX_HEADS * IDX_DIM, BF16), (IDX_DIM, BF16), (LANES, F32)]
    return pl.pallas_call(
        _in_kernel,
        grid=(n // tm,),
        in_specs=[pl.BlockSpec((tm, d), row),
                  pl.BlockSpec((1, d), const2),
                  pl.BlockSpec((1, 1, d), per_b),
                  pl.BlockSpec((1, 1, d), per_b),
                  pl.BlockSpec((d, IN_PAD), const2),
                  pl.BlockSpec((1, KV_RANK), const2),
                  pl.BlockSpec((GMLP_WIDTH, GMLP_WIDTH), const2),
                  pl.BlockSpec((GMLP_HEADS, GMLP_BLOCK, GMLP_BLOCK), lambda i: (0, 0, 0)),
                  pl.BlockSpec((GMLP_BLOCK, GMLP_WIDTH), const2)],
        out_specs=[pl.BlockSpec((tm, w), row) for w, _ in outs] + [pl.BlockSpec((KV_RANK, tm), lambda i: (0, i))],
        out_shape=[jax.ShapeDtypeStruct((n, w), dt) for w, dt in outs] + [jax.ShapeDtypeStruct((KV_RANK, n), BF16)],
        compiler_params=_cparams("parallel"),
        name="input_proj_gmlp",
    )(x2, g, sc, sh, w_in_pad, kvg, pavg, wsp, bsp)


def _ssm_prep(a_re_log, a_im, b_re, b_im, c_re, c_im, log_dt, t_len):
    hp = lax.Precision.HIGHEST
    lam_re = -jnp.exp(a_re_log)
    lam_im = a_im
    dt = jnp.exp(log_dt)[:, None]
    mag = jnp.exp(lam_re * dt)
    abar_re = mag * jnp.cos(lam_im * dt)
    abar_im = mag * jnp.sin(lam_im * dt)
    den = lam_re * lam_re + lam_im * lam_im
    p = abar_re - 1.0
    qq = abar_im
    f_re = (p * lam_re + qq * lam_im) / den
    f_im = (qq * lam_re - p * lam_im) / den
    bb_re = f_re[..., None] * b_re - f_im[..., None] * b_im
    bb_im = f_re[..., None] * b_im + f_im[..., None] * b_re
    k = jnp.arange(t_len + 1, dtype=F32)[:, None, None]
    pmag = jnp.exp(k * (lam_re * dt))
    ang = k * (lam_im * dt)
    pw_re = pmag * jnp.cos(ang)
    pw_im = pmag * jnp.sin(ang)
    cp_re = c_re[None] * pw_re[:, :, None, :] - c_im[None] * pw_im[:, :, None, :]
    cp_im = c_re[None] * pw_im[:, :, None, :] + c_im[None] * pw_re[:, :, None, :]
    kern = (jnp.einsum('kghp,gpj->kghj', cp_re[:t_len], bb_re, precision=hp)
            - jnp.einsum('kghp,gpj->kghj', cp_im[:t_len], bb_im, precision=hp))
    tt = jnp.arange(t_len)
    lag = tt[None, :] - tt[:, None]
    toep = jnp.where((lag >= 0)[:, :, None, None, None], kern[jnp.clip(lag, 0)], 0.0)
    g_n, h_n = SSM_GROUPS, SSM_GROUP
    m = toep.transpose(2, 0, 4, 1, 3).reshape(g_n, t_len * h_n, t_len * h_n)
    rev_re = pw_re[t_len - 1 - tt]
    rev_im = pw_im[t_len - 1 - tt]
    w_re = rev_re[..., None] * bb_re[None] - rev_im[..., None] * bb_im[None]
    w_im = rev_re[..., None] * bb_im[None] + rev_im[..., None] * bb_re[None]
    w_re = w_re.transpose(1, 0, 3, 2).reshape(g_n, t_len * h_n, SSM_STATE)
    w_im = w_im.transpose(1, 0, 3, 2).reshape(g_n, t_len * h_n, SSM_STATE)
    v_re = cp_re[1:].transpose(1, 3, 0, 2).reshape(g_n, SSM_STATE, t_len * h_n)
    v_im = (-cp_im[1:]).transpose(1, 3, 0, 2).reshape(g_n, SSM_STATE, t_len * h_n)
    at_re = pw_re[t_len][:, None, :]
    at_im = pw_im[t_len][:, None, :]
    return (m.astype(BF16), w_re.astype(BF16), w_im.astype(BF16), v_re.astype(BF16), v_im.astype(BF16),
            at_re, at_im)


def _ssm_kernel(u_ref, m_ref, wre_ref, wim_ref, vre_ref, vim_ref, are_ref, aim_ref, y_ref,
                lre_ref, lim_ref, pre_ref, pim_ref, *, nb, nchunks):
    u = u_ref[...]
    lre_ref[...] = _dot(u, wre_ref[...])
    lim_ref[...] = _dot(u, wim_ref[...])
    ar = are_ref[...]
    ai = aim_ref[...]
    s_re = jnp.zeros((nb, SSM_STATE), F32)
    s_im = jnp.zeros((nb, SSM_STATE), F32)
    for c in range(nchunks):
        rows = slice(c * nb, (c + 1) * nb)
        pre_ref[rows, :] = s_re
        pim_ref[rows, :] = s_im
        s_re, s_im = (ar * s_re - ai * s_im + lre_ref[rows, :],
                      ar * s_im + ai * s_re + lim_ref[rows, :])
    y_ref[...] = (_dot(u, m_ref[...])
                  + _dot(pre_ref[...].astype(BF16), vre_ref[...])
                  + _dot(pim_ref[...].astype(BF16), vim_ref[...]))


def _ssm_scan(u_a, bsz, seq, prep):
    m, w_re, w_im, v_re, v_im, at_re, at_im = prep
    t_len = SSM_T
    nchunks = seq // t_len
    r = nchunks * bsz
    tw = t_len * SSM_GROUP
    u = u_a.reshape(bsz, nchunks, t_len, SSM_GROUPS, SSM_GROUP).transpose(3, 1, 0, 2, 4)
    u = u.reshape(SSM_GROUPS, r, tw).astype(BF16)
    grp = lambda g: (g, 0, 0)
    y = pl.pallas_call(
        functools.partial(_ssm_kernel, nb=bsz, nchunks=nchunks),
        grid=(SSM_GROUPS,),
        in_specs=[pl.BlockSpec((None, r, tw), grp),
                  pl.BlockSpec((None, tw, tw), grp),
                  pl.BlockSpec((None, tw, SSM_STATE), grp),
                  pl.BlockSpec((None, tw, SSM_STATE), grp),
                  pl.BlockSpec((None, SSM_STATE, tw), grp),
                  pl.BlockSpec((None, SSM_STATE, tw), grp),
                  pl.BlockSpec((None, 1, SSM_STATE), grp),
                  pl.BlockSpec((None, 1, SSM_STATE), grp)],
        out_specs=pl.BlockSpec((None, r, tw), grp),
        out_shape=jax.ShapeDtypeStruct((SSM_GROUPS, r, tw), F32),
        scratch_shapes=[pltpu.VMEM((r, SSM_STATE), F32) for _ in range(4)],
        compiler_params=_cparams("parallel"),
        name="ssm_scan",
    )(u, m, w_re, w_im, v_re, v_im, at_re, at_im)
    y = y.reshape(SSM_GROUPS, nchunks, bsz, t_len, SSM_GROUP).transpose(2, 1, 3, 0, 4)
    return y.reshape(bsz * seq, SSM_WIDTH)


def _dsa_kernel(q_ref, qi_ref, wi_ref, ki_ref, ckv_ref, ckvt_ref, wuk_ref, wuvt_ref, tri_ref, o_ref,
                key_ref, dm_ref, qa_ref, ot_ref, *acc_refs, topk, kb_len):
    tq = q_ref.shape[0]
    t0 = pl.program_id(1) * tq
    n_kb = (t0 + tq + kb_len - 1) // kb_len
    q_pos = t0 + lax.broadcasted_iota(jnp.int32, (1, tq), 1)
    q_chunk = q_pos // CHUNK
    k_off = lax.broadcasted_iota(jnp.int32, (kb_len, 1), 0)

    def fold8(m, op):
        r = m.reshape(kb_len // 8, 8, tq)
        n = kb_len // 8
        while n > 1:
            n //= 2
            r = op(r[:n], r[n:2 * n])
        return r[0]

    def col_sum(m):
        return fold8(m, jnp.add)

    wi_t = wi_ref[...].T
    qi_h = [qi_ref[:, h * IDX_DIM:(h + 1) * IDX_DIM] for h in range(IDX_HEADS)]
    w_h = [wi_t[IDX_DIM + h:IDX_DIM + h + 1, :] * ((IDX_HEADS ** -0.5) * (IDX_DIM ** -0.5))
           for h in range(IDX_HEADS)]

    def score_block(kb, carry):
        k0 = pl.multiple_of(kb * kb_len, kb_len)
        ki = ki_ref[pl.ds(k0, kb_len), :]
        score = jnp.zeros((kb_len, tq), F32)
        for h in range(IDX_HEADS):
            score = score + jnp.maximum(_dot_nt(ki, qi_h[h]), 0.0) * w_h[h]
        score = jnp.where(score == 0.0, 0.0, score)
        bits = lax.bitcast_convert_type(score, jnp.int32)
        key = jnp.where(bits < 0, bits ^ jnp.int32(0x7FFFFFFF), bits)
        adm = ((k0 + k_off) // CHUNK) <= q_chunk
        key_ref[pl.ds(k0, kb_len), :] = jnp.where(adm, key, jnp.int32(INT_MIN))
        return carry

    lax.fori_loop(0, n_kb, score_block, 0)

    def count(pred):
        def body(kb, acc):
            k0 = pl.multiple_of(kb * kb_len, kb_len)
            return acc + col_sum(pred(key_ref[pl.ds(k0, kb_len), :]).astype(F32))
        acc = lax.fori_loop(0, n_kb, body, jnp.zeros((8, tq), F32))
        return jnp.sum(acc, axis=0, keepdims=True)

    kf = jnp.float32(topk)

    def bit_step(step, ans):
        trial = ans + jnp.left_shift(jnp.int32(1), 31 - step)
        return jnp.where(count(lambda kblk: kblk >= trial) >= kf, trial, ans)

    ans = lax.fori_loop(0, 32, bit_step, jnp.full((1, tq), INT_MIN, jnp.int32))

    room = kf - count(lambda kblk: kblk > ans)
    tri = tri_ref[...]

    def select_block(kb, carry):
        k0 = pl.multiple_of(kb * kb_len, kb_len)
        kblk = key_ref[pl.ds(k0, kb_len), :]
        eq = kblk == ans
        pre = _dot(tri, eq.astype(BF16)) + carry
        sel = (kblk > ans) | (eq & (pre <= room))
        k_pos = k0 + k_off
        adm = (k_pos // CHUNK) <= q_chunk
        dist = jnp.abs(q_pos - k_pos).astype(F32)
        dm_ref[pl.ds(k0, kb_len), :] = jnp.where(sel & adm, dist, -NEG)
        return carry + jnp.sum(col_sum(eq.astype(F32)), axis=0, keepdims=True)

    lax.fori_loop(0, n_kb, select_block, jnp.zeros((1, tq), F32))

    for h in range(ATT_HEADS):
        hs = slice(h * ATT_HEAD_DIM, (h + 1) * ATT_HEAD_DIM)
        qa_ref[h] = (_dot(q_ref[:, hs], wuk_ref[h]) * (ATT_HEAD_DIM ** -0.5)).astype(BF16)
        acc_refs[h][...] = jnp.zeros((KV_RANK, tq), F32)

    def att_block(kb, carry):
        k0 = pl.multiple_of(kb * kb_len, kb_len)
        ckv = ckv_ref[pl.ds(k0, kb_len), :]
        ckvt = ckvt_ref[:, pl.ds(k0, kb_len)]
        dm = dm_ref[pl.ds(k0, kb_len), :]
        new = []
        qk = _dot_nt(ckv, qa_ref[0])
        for h in range(ATT_HEADS):
            m_run, l_run = carry[2 * h], carry[2 * h + 1]
            slope = 2.0 ** (-8.0 * (h + 1) / ATT_HEADS)
            lg = qk - slope * dm
            if h + 1 < ATT_HEADS:
                qk = _dot_nt(ckv, qa_ref[h + 1])
            m_new = jnp.maximum(m_run, jnp.max(fold8(lg, jnp.maximum), axis=0, keepdims=True))
            alpha = jnp.exp(m_run - m_new)
            p = jnp.exp(lg - m_new)
            new += [m_new, alpha * l_run + jnp.sum(col_sum(p), axis=0, keepdims=True)]
            acc_refs[h][...] = alpha * acc_refs[h][...] + _dot(ckvt, p.astype(BF16))
        return tuple(new)

    init = tuple(jnp.full((1, tq), -jnp.inf if i % 2 == 0 else 0.0, F32) for i in range(2 * ATT_HEADS))
    fin = lax.fori_loop(0, n_kb, att_block, init)
    for h in range(ATT_HEADS):
        o_t = (acc_refs[h][...] / fin[2 * h + 1]).astype(BF16)
        ot_ref[h * ATT_HEAD_DIM:(h + 1) * ATT_HEAD_DIM, :] = _dot(wuvt_ref[h], o_t)
    o_ref[...] = ot_ref[...].T.astype(o_ref.dtype)


def _dsa(q, qi, wi, ki, ckv, ckvt, w_uk, w_uv, bsz, seq, tq=256, kb_len=512):
    n = q.shape[0]
    nqt = seq // tq
    kb_len = min(kb_len, seq)
    topk = min(TOPK_MAX, seq // 4)
    tri = (jnp.arange(kb_len)[:, None] >= jnp.arange(kb_len)[None, :]).astype(BF16)
    qrow = lambda b, i: (b * nqt + i, 0)
    krow = lambda b, i: (b, 0)
    c3 = lambda b, i: (0, 0, 0)
    return pl.pallas_call(
        functools.partial(_dsa_kernel, topk=topk, kb_len=kb_len),
        grid=(bsz, nqt),
        in_specs=[pl.BlockSpec((tq, ATT_WIDTH), qrow),
                  pl.BlockSpec((tq, IDX_HEADS * IDX_DIM), qrow),
                  pl.BlockSpec((tq, LANES), qrow),
                  pl.BlockSpec((seq, IDX_DIM), krow),
                  pl.BlockSpec((seq, KV_RANK), krow),
                  pl.BlockSpec((KV_RANK, seq), lambda b, i: (0, b)),
                  pl.BlockSpec((ATT_HEADS, ATT_HEAD_DIM, KV_RANK), c3),
                  pl.BlockSpec((ATT_HEADS, ATT_HEAD_DIM, KV_RANK), c3),
                  pl.BlockSpec((kb_len, kb_len), lambda b, i: (0, 0))],
        out_specs=pl.BlockSpec((tq, ATT_WIDTH), qrow),
        out_shape=jax.ShapeDtypeStruct((n, ATT_WIDTH), BF16),
        scratch_shapes=[pltpu.VMEM((seq, tq), jnp.int32),
                        pltpu.VMEM((seq, tq), F32),
                        pltpu.VMEM((ATT_HEADS, tq, KV_RANK), BF16),
                        pltpu.VMEM((ATT_WIDTH, tq), F32)]
        + [pltpu.VMEM((KV_RANK, tq), F32) for _ in range(ATT_HEADS)],
        compiler_params=_cparams("parallel", "parallel"),
        name="dsa_attention",
    )(q, qi, wi, ki, ckv, ckvt, w_uk.astype(BF16), jnp.swapaxes(w_uv, 1, 2).astype(BF16), tri)


def _out_kernel(ys_ref, ua_ref, yb_ref, yc_ref, x_ref, d_ref, wglu_ref, bglu_ref, wo_ref, g1_ref,
                n2_ref, sc_ref, sh_ref, xo_ref, h2_ref):
    y = ys_ref[...] + d_ref[...] * ua_ref[...]
    yg = jax.nn.gelu(y)
    z = _dot(yg.astype(BF16), wglu_ref[...]) + bglu_ref[...]
    ya = yg * jax.nn.sigmoid(z)
    a_w, b_w = SSM_WIDTH, SSM_WIDTH + GMLP_WIDTH
    mix = (_dot(ya.astype(BF16), wo_ref[0:a_w, :])
           + _dot(yb_ref[...], wo_ref[a_w:b_w, :])
           + _dot(yc_ref[...], wo_ref[b_w:, :]))
    xn = x_ref[...] + g1_ref[0] * mix
    xo_ref[...] = xn
    ms = jnp.mean(xn * xn, axis=-1, keepdims=True)
    h = xn * lax.rsqrt(ms + RMS_EPS) * n2_ref[...]
    h2_ref[...] = (h * (1.0 + sc_ref[0]) + sh_ref[0]).astype(BF16)


def _output_stage(ys, ua, yb, yc, x2, seq, d_skip, w_glu, b_glu, w_out, g1, n2g, sc2, sh2, tm=512):
    n, d = x2.shape
    tpb = seq // tm
    row = lambda i: (i, 0)
    const2 = lambda i: (0, 0)
    per_b = lambda i: (i // tpb, 0, 0)
    return pl.pallas_call(
        _out_kernel,
        grid=(n // tm,),
        in_specs=[pl.BlockSpec((tm, SSM_WIDTH), row),
                  pl.BlockSpec((tm, SSM_WIDTH), row),
                  pl.BlockSpec((tm, GMLP_WIDTH), row),
                  pl.BlockSpec((tm, ATT_WIDTH), row),
                  pl.BlockSpec((tm, d), row),
                  pl.BlockSpec((1, SSM_WIDTH), const2),
                  pl.BlockSpec((SSM_WIDTH, SSM_WIDTH), const2),
                  pl.BlockSpec((1, SSM_WIDTH), const2),
                  pl.BlockSpec((d, d), const2),
                  pl.BlockSpec((1, 1, d), per_b),
                  pl.BlockSpec((1, d), const2),
                  pl.BlockSpec((1, 1, d), per_b),
                  pl.BlockSpec((1, 1, d), per_b)],
        out_specs=[pl.BlockSpec((tm, d), row), pl.BlockSpec((tm, d), row)],
        out_shape=[jax.ShapeDtypeStruct((n, d), F32), jax.ShapeDtypeStruct((n, d), BF16)],
        compiler_params=_cparams("parallel"),
        name="out_proj_norm2",
    )(ys, ua, yb, yc, x2, d_skip, w_glu, b_glu, w_out, g1, n2g, sc2, sh2)


def _extract_top(x, n_top, with_rank=False):
    tops = []
    rank = jnp.full(x.shape, LANES - 1, F32) if with_rank else None
    for k in range(n_top):
        mx = jnp.max(x, axis=0, keepdims=True)
        tops.append(mx)
        hit = x == mx
        if with_rank:
            rank = jnp.where(hit, float(k), rank)
        x = jnp.where(hit, -jnp.inf, x)
    return (tops, rank) if with_rank else tops


def _peer_kernel(h2_ref, x_ref, g2_ref, fg_ref, wq_ref, kbd_ref, u_ref, vt_ref, o_ref,
                 st_ref, e1_ref, e2_ref, r2_ref, acc_ref, at_ref, ga_ref, v1_ref, v2_ref, *, final_norm):
    j = pl.program_id(1)
    tm = h2_ref.shape[0]
    te = u_ref.shape[0]
    n1 = te // PEER_N_KEYS
    nk = PEER_N_KEYS
    h2 = h2_ref[...]

    @pl.when(j == 0)
    def _():
        q = _dot(h2, wq_ref[...]).astype(BF16)
        st_ref[...] = _dot_nt(kbd_ref[...], q)

        def per_head(h, carry):
            base = pl.multiple_of(h * 2 * nk, 2 * nk)
            for lc in range(tm // LANES):
                ls = slice(lc * LANES, (lc + 1) * LANES)
                t1 = _extract_top(st_ref[pl.ds(base, nk), ls], PEER_TOPK + 1)
                t2, rank2 = _extract_top(st_ref[pl.ds(base + nk, nk), ls], PEER_TOPK + 1, with_rank=True)
                for k in range(PEER_TOPK):
                    v1_ref[k:k + 1, :] = t1[k]
                    v2_ref[k:k + 1, :] = t2[k]
                v2_all = v2_ref[0:PEER_TOPK, :]
                v2_top = v2_ref[0:8, :]
                cand = [t1[0] + v2_all]
                cand += [t1[a] + v2_top for a in range(1, 8)]
                cand.append(v1_ref[8:PEER_TOPK, :] + t2[0])
                top = _extract_top(jnp.concatenate(cand, axis=0), PEER_TOPK + 1)
                z = jnp.zeros_like(top[0])
                for k in range(PEER_TOPK):
                    z = z + jnp.exp(top[k] - top[0])
                nxt = jnp.maximum(top[PEER_TOPK], jnp.maximum(t1[PEER_TOPK] + t2[0], t1[0] + t2[PEER_TOPK]))
                thr = 0.5 * (top[PEER_TOPK - 1] + nxt)
                s1 = st_ref[pl.ds(base, nk), ls]
                s2 = st_ref[pl.ds(base + nk, nk), ls]
                e1_ref[h, :, ls] = jnp.exp(s1 - t1[0]) * (0.5 / z)
                e2_ref[h, :, ls] = jnp.exp(s2 - t2[0]).astype(BF16)
                r2_ref[h, :, ls] = rank2.astype(BF16)
                n_sel = jnp.zeros((nk, LANES), F32)
                for a in range(PEER_TOPK):
                    n_a = jnp.zeros_like(thr)
                    for b in range(PEER_TOPK // (a + 1)):
                        n_a = n_a + jnp.where(t1[a] + t2[b] >= thr, 1.0, 0.0)
                    n_sel = jnp.where(s1 == t1[a], n_a, n_sel)
                st_ref[pl.ds(base, nk), ls] = n_sel
            return carry

        lax.fori_loop(0, PEER_HEADS, per_head, 0)
        acc_ref[...] = jnp.zeros_like(acc_ref)
        ga_ref[1] = jnp.zeros(ga_ref.shape[1:], BF16)
        at_ref[0] = _dot_nt(u_ref[...], h2)

    n_blocks = pl.num_programs(1) - 2
    rd = j % 2
    wr = 1 - rd

    @pl.when((j >= 1) & (j <= n_blocks))
    def _():
        at_ref[rd] = _dot_nt(u_ref[...], h2)
        acc_ref[...] += _dot(vt_ref[...], ga_ref[rd])
        r0 = pl.multiple_of((j - 1) * n1, n1)
        for lc in range(tm // LANES):
            ls = slice(lc * LANES, (lc + 1) * LANES)
            for il in range(n1):
                g = jnp.zeros((nk, LANES), BF16)
                for h in range(PEER_HEADS):
                    n_row = st_ref[pl.ds(h * 2 * nk + r0, n1), ls][il:il + 1, :].astype(BF16)
                    e_row = e1_ref[h, pl.ds(r0, n1), ls][il:il + 1, :].astype(BF16)
                    picked = jnp.where(r2_ref[h, :, ls] < n_row, e2_ref[h, :, ls], jnp.zeros((), BF16))
                    g = g + picked * e_row
                a = at_ref[wr, il * nk:(il + 1) * nk, ls]
                inner = a * (0.7978845608028654 + 0.035677408136300125 * (a * a))
                act2 = a + a * jnp.tanh(inner)
                ga_ref[wr, il * nk:(il + 1) * nk, ls] = g * act2.astype(BF16)

    @pl.when(j == n_blocks + 1)
    def _():
        acc = acc_ref[...] + _dot(vt_ref[...], ga_ref[rd])
        out = x_ref[...] + g2_ref[0] * acc.T
        if final_norm:
            ms = jnp.mean(out * out, axis=-1, keepdims=True)
            out = out * lax.rsqrt(ms + RMS_EPS) * fg_ref[...]
        o_ref[...] = out


def _peer(h2, x2, seq, g2, fg, wq, kbd, u_bf, vt_bf, final_norm, tm=512, te=1024):
    n, d = x2.shape
    n_blk = u_bf.shape[0] // te
    tpb = seq // tm
    trow = lambda i, j: (i, 0)
    c2 = lambda i, j: (0, 0)
    hs = (PEER_HEADS, PEER_N_KEYS, tm)
    return pl.pallas_call(
        functools.partial(_peer_kernel, final_norm=final_norm),
        grid=(n // tm, n_blk + 2),
        in_specs=[pl.BlockSpec((tm, d), trow),
                  pl.BlockSpec((tm, d), trow),
                  pl.BlockSpec((1, 1, d), lambda i, j: (i // tpb, 0, 0)),
                  pl.BlockSpec((1, d), c2),
                  pl.BlockSpec((d, PEER_HEADS * 2 * PEER_HALF), c2),
                  pl.BlockSpec((PEER_HEADS * 2 * PEER_N_KEYS, PEER_HEADS * 2 * PEER_HALF), c2),
                  pl.BlockSpec((te, d), lambda i, j: (jnp.minimum(j, n_blk - 1), 0)),
                  pl.BlockSpec((d, te), lambda i, j: (0, jnp.clip(j - 2, 0, n_blk - 1)))],
        out_specs=pl.BlockSpec((tm, d), trow),
        out_shape=jax.ShapeDtypeStruct((n, d), F32),
        scratch_shapes=[pltpu.VMEM((PEER_HEADS * 2 * PEER_N_KEYS, tm), F32),
                        pltpu.VMEM(hs, F32), pltpu.VMEM(hs, BF16), pltpu.VMEM(hs, BF16),
                        pltpu.VMEM((d, tm), F32),
                        pltpu.VMEM((2, te, tm), F32),
                        pltpu.VMEM((2, te, tm), BF16),
                        pltpu.VMEM((PEER_TOPK, LANES), F32), pltpu.VMEM((PEER_TOPK, LANES), F32)],
        compiler_params=_cparams("parallel", "arbitrary"),
        name="peer_dense",
    )(h2, x2, g2, fg, wq, kbd, u_bf, vt_bf)


def _peer_key_matrix(k1, k2):
    blocks = []
    for h in range(PEER_HEADS):
        for half, kk in enumerate((k1, k2)):
            col = (2 * h + half) * PEER_HALF
            blocks.append(jnp.pad(kk, ((0, 0), (col, PEER_HEADS * 2 * PEER_HALF - col - PEER_HALF))))
    return jnp.concatenate(blocks, axis=0).astype(BF16)


def kernel(x, c, norm1_g, norm2_g, w_mod, b_mod, w_in, ssm_a_re_log, ssm_a_im, ssm_b_re, ssm_b_im, ssm_c_re, ssm_c_im, ssm_d, ssm_log_dt, ssm_w_glu, ssm_b_glu, gmlp_w_sp, gmlp_b_sp, kv_norm_g, w_uk, w_uv, w_out, peer_w_q, peer_k1, peer_k2, peer_u, peer_v, final_g):
    bsz, seq, d = x.shape
    depth = w_mod.shape[0]
    x2 = x.reshape(bsz * seq, d)
    head_of = jnp.arange(GMLP_WIDTH) // GMLP_HEAD_DIM
    pavg = ((head_of[:, None] == head_of[None, :]).astype(F32) / GMLP_HEAD_DIM).astype(BF16)
    chunk_of = jnp.arange(GMLP_BLOCK) // CHUNK
    sp_mask = chunk_of[:, None] >= chunk_of[None, :]
    for l in range(depth):
        mod = _modulation(c, w_mod[l], b_mod[l])
        sh1, sc1, g1, sh2, sc2, g2 = [mod[:, i * d:(i + 1) * d].reshape(bsz, 1, d) for i in range(6)]
        w_in_pad = jnp.pad(w_in[l], ((0, 0), (0, IN_PAD - IN_WIDTH))).astype(BF16)
        wsp = jnp.where(sp_mask[None], gmlp_w_sp[l], 0.0).astype(BF16)
        bsp = jnp.repeat(gmlp_b_sp[l].T, GMLP_HEAD_DIM, axis=1)
        ua, yb, q, ckv, qi, ki, wi, ckvt = _input_stage(
            x2, seq, norm1_g[l].reshape(1, d), sc1, sh1, w_in_pad, kv_norm_g[l].reshape(1, KV_RANK),
            pavg, wsp, bsp)
        prep = _ssm_prep(ssm_a_re_log[l], ssm_a_im[l], ssm_b_re[l], ssm_b_im[l], ssm_c_re[l], ssm_c_im[l],
                         ssm_log_dt[l], SSM_T)
        ys = _ssm_scan(ua, bsz, seq, prep)
        yc = _dsa(q, qi, wi, ki, ckv, ckvt, w_uk[l], w_uv[l], bsz, seq)
        x2, h2 = _output_stage(
            ys, ua, yb, yc, x2, seq, ssm_d[l].reshape(1, SSM_WIDTH), ssm_w_glu[l].astype(BF16),
            ssm_b_glu[l].reshape(1, SSM_WIDTH), w_out[l].astype(BF16), g1, norm2_g[l].reshape(1, d), sc2, sh2)
        x2 = _peer(h2, x2, seq, g2, final_g.reshape(1, d), peer_w_q[l].astype(BF16),
                   _peer_key_matrix(peer_k1[l], peer_k2[l]), peer_u[l].astype(BF16),
                   peer_v[l].T.astype(BF16), final_norm=(l == depth - 1))
    return x2.reshape(bsz, seq, d)
```

```python
import functools
import math

import jax
import jax.numpy as jnp
from jax import lax
from jax.experimental import pallas as pl
from jax.experimental.pallas import tpu as pltpu

CHUNK = 64
RMS_EPS = 1e-6

SSM_GROUP = 16
SSM_STATE = 64
SSM_WIDTH = 256
SSM_GROUPS = SSM_WIDTH // SSM_GROUP
SSM_T = 32
GMLP_HEADS = 4
GMLP_HEAD_DIM = 64
GMLP_WIDTH = GMLP_HEADS * GMLP_HEAD_DIM
GMLP_BLOCK = 128
ATT_HEADS = 8
ATT_HEAD_DIM = 64
ATT_WIDTH = ATT_HEADS * ATT_HEAD_DIM
KV_RANK = 128
IDX_HEADS = 4
IDX_DIM = 64
TOPK_MAX = 256

IN_SIZES = (SSM_WIDTH, GMLP_WIDTH, GMLP_WIDTH, ATT_WIDTH, KV_RANK, IDX_HEADS * IDX_DIM, IDX_DIM, IDX_HEADS)
IN_WIDTH = sum(IN_SIZES)
IN_PAD = 1792
COL_UB, COL_VB, COL_Q, COL_CKV, COL_QI, COL_KW = 256, 512, 768, 1280, 1408, 1664

PEER_HEADS = 8
PEER_N_KEYS = 128
PEER_HALF = 64
PEER_TOPK = 16

LANES = 128
VMEM_LIMIT = 56 * 1024 * 1024
NEG = -1e30
INT_MIN = -(2 ** 31)

F32 = jnp.float32
BF16 = jnp.bfloat16


def _cparams(*sem):
    return pltpu.CompilerParams(dimension_semantics=sem, vmem_limit_bytes=VMEM_LIMIT)


def _dot(a, b):
    return jnp.dot(a, b, preferred_element_type=F32)


def _dot_nt(a, b):
    return lax.dot_general(a, b, (((1,), (1,)), ((), ())), preferred_element_type=F32)


def _mod_kernel(c_ref, w_ref, b_ref, o_ref):
    c = c_ref[...]
    ca = c * jax.nn.sigmoid(c)
    o_ref[...] = _dot(ca.astype(BF16), w_ref[...].astype(BF16)) + b_ref[...]


def _modulation(c, w_mod, b_mod):
    bsz, d = c.shape
    n6 = w_mod.shape[1]
    return pl.pallas_call(
        _mod_kernel,
        grid=(n6 // d,),
        in_specs=[pl.BlockSpec((bsz, d), lambda j: (0, 0)),
                  pl.BlockSpec((d, d), lambda j: (0, j)),
                  pl.BlockSpec((1, d), lambda j: (0, j))],
        out_specs=pl.BlockSpec((bsz, d), lambda j: (0, j)),
        out_shape=jax.ShapeDtypeStruct((bsz, n6), F32),
        compiler_params=_cparams("parallel"),
        name="modulation",
    )(c, w_mod, b_mod.reshape(1, n6))


def _in_kernel(x_ref, g_ref, sc_ref, sh_ref, w_ref, kvg_ref, pavg_ref, wsp_ref, bsp_ref,
               ua_ref, yb_ref, q_ref, ckv_ref, qi_ref, ki_ref, wi_ref, ckvt_ref):
    tm = x_ref.shape[0]
    x = x_ref[...]
    ms = jnp.mean(x * x, axis=-1, keepdims=True)
    h = x * lax.rsqrt(ms + RMS_EPS) * g_ref[...]
    h = h * (1.0 + sc_ref[0]) + sh_ref[0]
    proj = _dot(h.astype(BF16), w_ref[...])

    ua_ref[...] = proj[:, 0:COL_UB]
    q_ref[...] = proj[:, COL_Q:COL_CKV].astype(BF16)
    ckv = proj[:, COL_CKV:COL_QI]
    ckv_ms = jnp.mean(ckv * ckv, axis=-1, keepdims=True)
    ckv_n = ckv * lax.rsqrt(ckv_ms + RMS_EPS) * kvg_ref[...]
    ckv_ref[...] = ckv_n.astype(BF16)
    ckvt_ref[...] = ckv_n.T.astype(BF16)
    qi_ref[...] = proj[:, COL_QI:COL_KW].astype(BF16)
    kw = proj[:, COL_KW:IN_PAD]
    ki_ref[...] = kw[:, 0:IDX_DIM].astype(BF16)
    wi_ref[...] = kw

    u_b = proj[:, COL_UB:COL_VB]
    v_b = proj[:, COL_VB:COL_Q]
    pavg = pavg_ref[...]

    def head_mean(a):
        hi = a.astype(BF16)
        lo = (a - hi.astype(F32)).astype(BF16)
        return _dot(hi, pavg) + _dot(lo, pavg)

    mu = head_mean(v_b)
    dv = v_b - mu
    var = head_mean(dv * dv)
    vn = (dv * lax.rsqrt(var + RMS_EPS)).astype(BF16)
    lane_head = lax.broadcasted_iota(jnp.int32, (GMLP_BLOCK, GMLP_WIDTH), 1) // GMLP_HEAD_DIM
    for blk in range(tm // GMLP_BLOCK):
        rows = slice(blk * GMLP_BLOCK, (blk + 1) * GMLP_BLOCK)
        vblk = vn[rows, :]
        mixed = bsp_ref[...]
        for hh in range(GMLP_HEADS):
            mixed = mixed + jnp.where(lane_head == hh, _dot(wsp_ref[hh], vblk), 0.0)
        yb_ref[rows, :] = (u_b[rows, :] * mixed).astype(BF16)


def _input_stage(x2, seq, g, sc, sh, w_in_pad, kvg, pavg, wsp, bsp, tm=512):
    n, d = x2.shape
    tpb = seq // tm
    row = lambda i: (i, 0)
    const2 = lambda i: (0, 0)
    per_b = lambda i: (i // tpb, 0, 0)
    outs = [(SSM_WIDTH, F32), (GMLP_WIDTH, BF16), (ATT_WIDTH, BF16), (KV_RANK, BF16),
            (IDX_HEADS * IDX_DIM, BF16), (IDX_DIM, BF16), (LANES, F32)]
    return pl.pallas_call(
        _in_kernel,
        grid=(n // tm,),
        in_specs=[pl.BlockSpec((tm, d), row),
                  pl.BlockSpec((1, d), const2),
                  pl.BlockSpec((1, 1, d), per_b),
                  pl.BlockSpec((1, 1, d), per_b),
                  pl.BlockSpec((d, IN_PAD), const2),
                  pl.BlockSpec((1, KV_RANK), const2),
                  pl.BlockSpec((GMLP_WIDTH, GMLP_WIDTH), const2),
                  pl.BlockSpec((GMLP_HEADS, GMLP_BLOCK, GMLP_BLOCK), lambda i: (0, 0, 0)),
                  pl.BlockSpec((GMLP_BLOCK, GMLP_WIDTH), const2)],
        out_specs=[pl.BlockSpec((tm, w), row) for w, _ in outs] + [pl.BlockSpec((KV_RANK, tm), lambda i: (0, i))],
        out_shape=[jax.ShapeDtypeStruct((n, w), dt) for w, dt in outs] + [jax.ShapeDtypeStruct((KV_RANK, n), BF16)],
        compiler_params=_cparams("parallel"),
        name="input_proj_gmlp",
    )(x2, g, sc, sh, w_in_pad, kvg, pavg, wsp, bsp)


def _ssm_prep(a_re_log, a_im, b_re, b_im, c_re, c_im, log_dt, t_len):
    hp = lax.Precision.HIGHEST
    lam_re = -jnp.exp(a_re_log)
    lam_im = a_im
    dt = jnp.exp(log_dt)[:, None]
    mag = jnp.exp(lam_re * dt)
    abar_re = mag * jnp.cos(lam_im * dt)
    abar_im = mag * jnp.sin(lam_im * dt)
    den = lam_re * lam_re + lam_im * lam_im
    p = abar_re - 1.0
    qq = abar_im
    f_re = (p * lam_re + qq * lam_im) / den
    f_im = (qq * lam_re - p * lam_im) / den
    bb_re = f_re[..., None] * b_re - f_im[..., None] * b_im
    bb_im = f_re[..., None] * b_im + f_im[..., None] * b_re
    k = jnp.arange(t_len + 1, dtype=F32)[:, None, None]
    pmag = jnp.exp(k * (lam_re * dt))
    ang = k * (lam_im * dt)
    pw_re = pmag * jnp.cos(ang)
    pw_im = pmag * jnp.sin(ang)
    cp_re = c_re[None] * pw_re[:, :, None, :] - c_im[None] * pw_im[:, :, None, :]
    cp_im = c_re[None] * pw_im[:, :, None, :] + c_im[None] * pw_re[:, :, None, :]
    kern = (jnp.einsum('kghp,gpj->kghj', cp_re[:t_len], bb_re, precision=hp)
            - jnp.einsum('kghp,gpj->kghj', cp_im[:t_len], bb_im, precision=hp))
    tt = jnp.arange(t_len)
    lag = tt[None, :] - tt[:, None]
    toep = jnp.where((lag >= 0)[:, :, None, None, None], kern[jnp.clip(lag, 0)], 0.0)
    g_n, h_n = SSM_GROUPS, SSM_GROUP
    m = toep.transpose(2, 0, 4, 1, 3).reshape(g_n, t_len * h_n, t_len * h_n)
    rev_re = pw_re[t_len - 1 - tt]
    rev_im = pw_im[t_len - 1 - tt]
    w_re = rev_re[..., None] * bb_re[None] - rev_im[..., None] * bb_im[None]
    w_im = rev_re[..., None] * bb_im[None] + rev_im[..., None] * bb_re[None]
    w_re = w_re.transpose(1, 0, 3, 2).reshape(g_n, t_len * h_n, SSM_STATE)
    w_im = w_im.transpose(1, 0, 3, 2).reshape(g_n, t_len * h_n, SSM_STATE)
    v_re = cp_re[1:].transpose(1, 3, 0, 2).reshape(g_n, SSM_STATE, t_len * h_n)
    v_im = (-cp_im[1:]).transpose(1, 3, 0, 2).reshape(g_n, SSM_STATE, t_len * h_n)
    at_re = pw_re[t_len][:, None, :]
    at_im = pw_im[t_len][:, None, :]
    return (m.astype(BF16), w_re.astype(BF16), w_im.astype(BF16), v_re.astype(BF16), v_im.astype(BF16),
            at_re, at_im)


def _ssm_kernel(u_ref, m_ref, wre_ref, wim_ref, vre_ref, vim_ref, are_ref, aim_ref, y_ref,
                lre_ref, lim_ref, pre_ref, pim_ref, *, nb, nchunks):
    u = u_ref[...]
    lre_ref[...] = _dot(u, wre_ref[...])
    lim_ref[...] = _dot(u, wim_ref[...])
    ar = are_ref[...]
    ai = aim_ref[...]
    s_re = jnp.zeros((nb, SSM_STATE), F32)
    s_im = jnp.zeros((nb, SSM_STATE), F32)
    for c in range(nchunks):
        rows = slice(c * nb, (c + 1) * nb)
        pre_ref[rows, :] = s_re
        pim_ref[rows, :] = s_im
        s_re, s_im = (ar * s_re - ai * s_im + lre_ref[rows, :],
                      ar * s_im + ai * s_re + lim_ref[rows, :])
    y_ref[...] = (_dot(u, m_ref[...])
                  + _dot(pre_ref[...].astype(BF16), vre_ref[...])
                  + _dot(pim_ref[...].astype(BF16), vim_ref[...]))


def _ssm_scan(u_a, bsz, seq, prep):
    m, w_re, w_im, v_re, v_im, at_re, at_im = prep
    t_len = SSM_T
    nchunks = seq // t_len
    r = nchunks * bsz
    tw = t_len * SSM_GROUP
    u = u_a.reshape(bsz, nchunks, t_len, SSM_GROUPS, SSM_GROUP).transpose(3, 1, 0, 2, 4)
    u = u.reshape(SSM_GROUPS, r, tw).astype(BF16)
    grp = lambda g: (g, 0, 0)
    y = pl.pallas_call(
        functools.partial(_ssm_kernel, nb=bsz, nchunks=nchunks),
        grid=(SSM_GROUPS,),
        in_specs=[pl.BlockSpec((None, r, tw), grp),
                  pl.BlockSpec((None, tw, tw), grp),
                  pl.BlockSpec((None, tw, SSM_STATE), grp),
                  pl.BlockSpec((None, tw, SSM_STATE), grp),
                  pl.BlockSpec((None, SSM_STATE, tw), grp),
                  pl.BlockSpec((None, SSM_STATE, tw), grp),
                  pl.BlockSpec((None, 1, SSM_STATE), grp),
                  pl.BlockSpec((None, 1, SSM_STATE), grp)],
        out_specs=pl.BlockSpec((None, r, tw), grp),
        out_shape=jax.ShapeDtypeStruct((SSM_GROUPS, r, tw), F32),
        scratch_shapes=[pltpu.VMEM((r, SSM_STATE), F32) for _ in range(4)],
        compiler_params=_cparams("parallel"),
        name="ssm_scan",
    )(u, m, w_re, w_im, v_re, v_im, at_re, at_im)
    y = y.reshape(SSM_GROUPS, nchunks, bsz, t_len, SSM_GROUP).transpose(2, 1, 3, 0, 4)
    return y.reshape(bsz * seq, SSM_WIDTH)


def _dsa_kernel(q_ref, qi_ref, wi_ref, ki_ref, ckv_ref, ckvt_ref, wuk_ref, wuvt_ref, tri_ref, o_ref,
                key_ref, dm_ref, qa_ref, ot_ref, *acc_refs, topk, kb_len):
    tq = q_ref.shape[0]
    t0 = pl.program_id(1) * tq
    n_kb = (t0 + tq + kb_len - 1) // kb_len
    q_pos = t0 + lax.broadcasted_iota(jnp.int32, (1, tq), 1)
    q_chunk = q_pos // CHUNK
    k_off = lax.broadcasted_iota(jnp.int32, (kb_len, 1), 0)

    def fold8(m, op):
        r = m.reshape(kb_len // 8, 8, tq)
        n = kb_len // 8
        while n > 1:
            n //= 2
            r = op(r[:n], r[n:2 * n])
        return r[0]

    def col_sum(m):
        return fold8(m, jnp.add)

    wi_t = wi_ref[...].T
    qi_h = [qi_ref[:, h * IDX_DIM:(h + 1) * IDX_DIM] for h in range(IDX_HEADS)]
    w_h = [wi_t[IDX_DIM + h:IDX_DIM + h + 1, :] * ((IDX_HEADS ** -0.5) * (IDX_DIM ** -0.5))
           for h in range(IDX_HEADS)]

    def score_block(kb, carry):
        k0 = pl.multiple_of(kb * kb_len, kb_len)
        ki = ki_ref[pl.ds(k0, kb_len), :]
        score = jnp.zeros((kb_len, tq), F32)
        for h in range(IDX_HEADS):
            score = score + jnp.maximum(_dot_nt(ki, qi_h[h]), 0.0) * w_h[h]
        score = jnp.where(score == 0.0, 0.0, score)
        bits = lax.bitcast_convert_type(score, jnp.int32)
        key = jnp.where(bits < 0, bits ^ jnp.int32(0x7FFFFFFF), bits)
        adm = ((k0 + k_off) // CHUNK) <= q_chunk
        key_ref[pl.ds(k0, kb_len), :] = jnp.where(adm, key, jnp.int32(INT_MIN))
        return carry

    lax.fori_loop(0, n_kb, score_block, 0)

    def count(pred):
        def body(kb, acc):
            k0 = pl.multiple_of(kb * kb_len, kb_len)
            return acc + col_sum(pred(key_ref[pl.ds(k0, kb_len), :]).astype(F32))
        acc = lax.fori_loop(0, n_kb, body, jnp.zeros((8, tq), F32))
        return jnp.sum(acc, axis=0, keepdims=True)

    kf = jnp.float32(topk)

    def bit_step(step, ans):
        trial = ans + jnp.left_shift(jnp.int32(1), 31 - step)
        return jnp.where(count(lambda kblk: kblk >= trial) >= kf, trial, ans)

    ans = lax.fori_loop(0, 32, bit_step, jnp.full((1, tq), INT_MIN, jnp.int32))

    room = kf - count(lambda kblk: kblk > ans)
    tri = tri_ref[...]

    def select_block(kb, carry):
        k0 = pl.multiple_of(kb * kb_len, kb_len)
        kblk = key_ref[pl.ds(k0, kb_len), :]
        eq = kblk == ans
        pre = _dot(tri, eq.astype(BF16)) + carry
        sel = (kblk > ans) | (eq & (pre <= room))
        k_pos = k0 + k_off
        adm = (k_pos // CHUNK) <= q_chunk
        dist = jnp.abs(q_pos - k_pos).astype(F32)
        dm_ref[pl.ds(k0, kb_len), :] = jnp.where(sel & adm, dist, -NEG)
        return carry + jnp.sum(col_sum(eq.astype(F32)), axis=0, keepdims=True)

    lax.fori_loop(0, n_kb, select_block, jnp.zeros((1, tq), F32))

    for h in range(ATT_HEADS):
        hs = slice(h * ATT_HEAD_DIM, (h + 1) * ATT_HEAD_DIM)
        qa_ref[h] = (_dot(q_ref[:, hs], wuk_ref[h]) * (ATT_HEAD_DIM ** -0.5)).astype(BF16)
        acc_refs[h][...] = jnp.zeros((KV_RANK, tq), F32)

    def att_block(kb, carry):
        k0 = pl.multiple_of(kb * kb_len, kb_len)
        ckv = ckv_ref[pl.ds(k0, kb_len), :]
        ckvt = ckvt_ref[:, pl.ds(k0, kb_len)]
        dm = dm_ref[pl.ds(k0, kb_len), :]
        new = []
        qk = _dot_nt(ckv, qa_ref[0])
        for h in range(ATT_HEADS):
            m_run, l_run = carry[2 * h], carry[2 * h + 1]
            slope = 2.0 ** (-8.0 * (h + 1) / ATT_HEADS)
            lg = qk - slope * dm
            if h + 1 < ATT_HEADS:
                qk = _dot_nt(ckv, qa_ref[h + 1])
            m_new = jnp.maximum(m_run, jnp.max(fold8(lg, jnp.maximum), axis=0, keepdims=True))
            alpha = jnp.exp(m_run - m_new)
            p = jnp.exp(lg - m_new)
            new += [m_new, alpha * l_run + jnp.sum(col_sum(p), axis=0, keepdims=True)]
            acc_refs[h][...] = alpha * acc_refs[h][...] + _dot(ckvt, p.astype(BF16))
        return tuple(new)

    init = tuple(jnp.full((1, tq), -jnp.inf if i % 2 == 0 else 0.0, F32) for i in range(2 * ATT_HEADS))
    fin = lax.fori_loop(0, n_kb, att_block, init)
    for h in range(ATT_HEADS):
        o_t = (acc_refs[h][...] / fin[2 * h + 1]).astype(BF16)
        ot_ref[h * ATT_HEAD_DIM:(h + 1) * ATT_HEAD_DIM, :] = _dot(wuvt_ref[h], o_t)
    o_ref[...] = ot_ref[...].T.astype(o_ref.dtype)


def _dsa(q, qi, wi, ki, ckv, ckvt, w_uk, w_uv, bsz, seq, tq=256, kb_len=512):
    n = q.shape[0]
    nqt = seq // tq
    kb_len = min(kb_len, seq)
    topk = min(TOPK_MAX, seq // 4)
    tri = (jnp.arange(kb_len)[:, None] >= jnp.arange(kb_len)[None, :]).astype(BF16)
    qrow = lambda b, i: (b * nqt + i, 0)
    krow = lambda b, i: (b, 0)
    c3 = lambda b, i: (0, 0, 0)
    return pl.pallas_call(
        functools.partial(_dsa_kernel, topk=topk, kb_len=kb_len),
        grid=(bsz, nqt),
        in_specs=[pl.BlockSpec((tq, ATT_WIDTH), qrow),
                  pl.BlockSpec((tq, IDX_HEADS * IDX_DIM), qrow),
                  pl.BlockSpec((tq, LANES), qrow),
                  pl.BlockSpec((seq, IDX_DIM), krow),
                  pl.BlockSpec((seq, KV_RANK), krow),
                  pl.BlockSpec((KV_RANK, seq), lambda b, i: (0, b)),
                  pl.BlockSpec((ATT_HEADS, ATT_HEAD_DIM, KV_RANK), c3),
                  pl.BlockSpec((ATT_HEADS, ATT_HEAD_DIM, KV_RANK), c3),
                  pl.BlockSpec((kb_len, kb_len), lambda b, i: (0, 0))],
        out_specs=pl.BlockSpec((tq, ATT_WIDTH), qrow),
        out_shape=jax.ShapeDtypeStruct((n, ATT_WIDTH), BF16),
        scratch_shapes=[pltpu.VMEM((seq, tq), jnp.int32),
                        pltpu.VMEM((seq, tq), F32),
                        pltpu.VMEM((ATT_HEADS, tq, KV_RANK), BF16),
                        pltpu.VMEM((ATT_WIDTH, tq), F32)]
        + [pltpu.VMEM((KV_RANK, tq), F32) for _ in range(ATT_HEADS)],
        compiler_params=_cparams("parallel", "parallel"),
        name="dsa_attention",
    )(q, qi, wi, ki, ckv, ckvt, w_uk.astype(BF16), jnp.swapaxes(w_uv, 1, 2).astype(BF16), tri)


def _out_kernel(ys_ref, ua_ref, yb_ref, yc_ref, x_ref, d_ref, wglu_ref, bglu_ref, wo_ref, g1_ref,
                n2_ref, sc_ref, sh_ref, xo_ref, h2_ref):
    y = ys_ref[...] + d_ref[...] * ua_ref[...]
    yg = jax.nn.gelu(y)
    z = _dot(yg.astype(BF16), wglu_ref[...]) + bglu_ref[...]
    ya = yg * jax.nn.sigmoid(z)
    a_w, b_w = SSM_WIDTH, SSM_WIDTH + GMLP_WIDTH
    mix = (_dot(ya.astype(BF16), wo_ref[0:a_w, :])
           + _dot(yb_ref[...], wo_ref[a_w:b_w, :])
           + _dot(yc_ref[...], wo_ref[b_w:, :]))
    xn = x_ref[...] + g1_ref[0] * mix
    xo_ref[...] = xn
    ms = jnp.mean(xn * xn, axis=-1, keepdims=True)
    h = xn * lax.rsqrt(ms + RMS_EPS) * n2_ref[...]
    h2_ref[...] = (h * (1.0 + sc_ref[0]) + sh_ref[0]).astype(BF16)


def _output_stage(ys, ua, yb, yc, x2, seq, d_skip, w_glu, b_glu, w_out, g1, n2g, sc2, sh2, tm=512):
    n, d = x2.shape
    tpb = seq // tm
    row = lambda i: (i, 0)
    const2 = lambda i: (0, 0)
    per_b = lambda i: (i // tpb, 0, 0)
    return pl.pallas_call(
        _out_kernel,
        grid=(n // tm,),
        in_specs=[pl.BlockSpec((tm, SSM_WIDTH), row),
                  pl.BlockSpec((tm, SSM_WIDTH), row),
                  pl.BlockSpec((tm, GMLP_WIDTH), row),
                  pl.BlockSpec((tm, ATT_WIDTH), row),
                  pl.BlockSpec((tm, d), row),
                  pl.BlockSpec((1, SSM_WIDTH), const2),
                  pl.BlockSpec((SSM_WIDTH, SSM_WIDTH), const2),
                  pl.BlockSpec((1, SSM_WIDTH), const2),
                  pl.BlockSpec((d, d), const2),
                  pl.BlockSpec((1, 1, d), per_b),
                  pl.BlockSpec((1, d), const2),
                  pl.BlockSpec((1, 1, d), per_b),
                  pl.BlockSpec((1, 1, d), per_b)],
        out_specs=[pl.BlockSpec((tm, d), row), pl.BlockSpec((tm, d), row)],
        out_shape=[jax.ShapeDtypeStruct((n, d), F32), jax.ShapeDtypeStruct((n, d), BF16)],
        compiler_params=_cparams("parallel"),
        name="out_proj_norm2",
    )(ys, ua, yb, yc, x2, d_skip, w_glu, b_glu, w_out, g1, n2g, sc2, sh2)


def _extract_top(x, n_top, with_rank=False):
    tops = []
    rank = jnp.full(x.shape, LANES - 1, F32) if with_rank else None
    for k in range(n_top):
        mx = jnp.max(x, axis=0, keepdims=True)
        tops.append(mx)
        hit = x == mx
        if with_rank:
            rank = jnp.where(hit, float(k), rank)
        x = jnp.where(hit, -jnp.inf, x)
    return (tops, rank) if with_rank else tops


def _peer_kernel(h2_ref, x_ref, g2_ref, fg_ref, wq_ref, kbd_ref, *rest, final_norm, n_split):
    u_refs, vt_refs = rest[:n_split], rest[n_split:2 * n_split]
    o_ref, st_ref, e1_ref, e2_ref, r2_ref, acc_ref, at_ref, ga_ref, v1_ref, v2_ref = rest[2 * n_split:]
    j = pl.program_id(1)
    tm = h2_ref.shape[0]
    u_rows = u_refs[0].shape[0]
    v_rows = vt_refs[0].shape[0]
    te = u_rows * n_split
    n1 = te // PEER_N_KEYS
    nk = PEER_N_KEYS
    h2 = h2_ref[...]

    @pl.when(j == 0)
    def _():
        q = _dot(h2, wq_ref[...]).astype(BF16)
        st_ref[...] = _dot_nt(kbd_ref[...], q)

        def per_head(h, carry):
            base = pl.multiple_of(h * 2 * nk, 2 * nk)
            for lc in range(tm // LANES):
                ls = slice(lc * LANES, (lc + 1) * LANES)
                t1 = _extract_top(st_ref[pl.ds(base, nk), ls], PEER_TOPK + 1)
                t2, rank2 = _extract_top(st_ref[pl.ds(base + nk, nk), ls], PEER_TOPK + 1, with_rank=True)
                for k in range(PEER_TOPK):
                    v1_ref[k:k + 1, :] = t1[k]
                    v2_ref[k:k + 1, :] = t2[k]
                v2_all = v2_ref[0:PEER_TOPK, :]
                v2_top = v2_ref[0:8, :]
                cand = [t1[0] + v2_all]
                cand += [t1[a] + v2_top for a in range(1, 8)]
                cand.append(v1_ref[8:PEER_TOPK, :] + t2[0])
                top = _extract_top(jnp.concatenate(cand, axis=0), PEER_TOPK + 1)
                z = jnp.zeros_like(top[0])
                for k in range(PEER_TOPK):
                    z = z + jnp.exp(top[k] - top[0])
                nxt = jnp.maximum(top[PEER_TOPK], jnp.maximum(t1[PEER_TOPK] + t2[0], t1[0] + t2[PEER_TOPK]))
                thr = 0.5 * (top[PEER_TOPK - 1] + nxt)
                s1 = st_ref[pl.ds(base, nk), ls]
                s2 = st_ref[pl.ds(base + nk, nk), ls]
                e1_ref[h, :, ls] = jnp.exp(s1 - t1[0]) * (0.5 / z)
                e2_ref[h, :, ls] = jnp.exp(s2 - t2[0]).astype(BF16)
                r2_ref[h, :, ls] = rank2.astype(BF16)
                n_sel = jnp.zeros((nk, LANES), F32)
                for a in range(PEER_TOPK):
                    n_a = jnp.zeros_like(thr)
                    for b in range(PEER_TOPK // (a + 1)):
                        n_a = n_a + jnp.where(t1[a] + t2[b] >= thr, 1.0, 0.0)
                    n_sel = jnp.where(s1 == t1[a], n_a, n_sel)
                st_ref[pl.ds(base, nk), ls] = n_sel
            return carry

        lax.fori_loop(0, PEER_HEADS, per_head, 0)
        acc_ref[...] = jnp.zeros_like(acc_ref)
        ga_ref[1] = jnp.zeros(ga_ref.shape[1:], BF16)
        for k in range(n_split):
            at_ref[0, k * u_rows:(k + 1) * u_rows, :] = _dot_nt(u_refs[k][...], h2)

    n_blocks = pl.num_programs(1) - 2
    rd = j % 2
    wr = 1 - rd

    @pl.when((j >= 1) & (j <= n_blocks))
    def _():
        for k in range(n_split):
            at_ref[rd, k * u_rows:(k + 1) * u_rows, :] = _dot_nt(u_refs[k][...], h2)
        for k in range(n_split):
            acc_ref[k * v_rows:(k + 1) * v_rows, :] += _dot(vt_refs[k][...], ga_ref[rd])
        r0 = pl.multiple_of((j - 1) * n1, n1)
        for lc in range(tm // LANES):
            ls = slice(lc * LANES, (lc + 1) * LANES)
            for il in range(n1):
                g = jnp.zeros((nk, LANES), BF16)
                for h in range(PEER_HEADS):
                    n_row = st_ref[pl.ds(h * 2 * nk + r0, n1), ls][il:il + 1, :].astype(BF16)
                    e_row = e1_ref[h, pl.ds(r0, n1), ls][il:il + 1, :].astype(BF16)
                    picked = jnp.where(r2_ref[h, :, ls] < n_row, e2_ref[h, :, ls], jnp.zeros((), BF16))
                    g = g + picked * e_row
                a = at_ref[wr, il * nk:(il + 1) * nk, ls]
                inner = a * (0.7978845608028654 + 0.035677408136300125 * (a * a))
                act2 = a + a * jnp.tanh(inner)
                ga_ref[wr, il * nk:(il + 1) * nk, ls] = g * act2.astype(BF16)

    @pl.when(j == n_blocks + 1)
    def _():
        for k in range(n_split):
            acc_ref[k * v_rows:(k + 1) * v_rows, :] += _dot(vt_refs[k][...], ga_ref[rd])
        out = x_ref[...] + g2_ref[0] * acc_ref[...].T
        if final_norm:
            ms = jnp.mean(out * out, axis=-1, keepdims=True)
            out = out * lax.rsqrt(ms + RMS_EPS) * fg_ref[...]
        o_ref[...] = out


def _peer(h2, x2, seq, g2, fg, wq, kbd, u_bf, vt_bf, final_norm, tm=512, te=1024, n_split=4):
    n, d = x2.shape
    n_blk = u_bf.shape[0] // te
    tpb = seq // tm
    trow = lambda i, j: (i, 0)
    c2 = lambda i, j: (0, 0)
    hs = (PEER_HEADS, PEER_N_KEYS, tm)
    u_map = lambda k, i, j: (jnp.minimum(j, n_blk - 1) * n_split + k, 0)
    v_map = lambda k, i, j: (k, jnp.clip(j - 2, 0, n_blk - 1))
    return pl.pallas_call(
        functools.partial(_peer_kernel, final_norm=final_norm, n_split=n_split),
        grid=(n // tm, n_blk + 2),
        in_specs=[pl.BlockSpec((tm, d), trow),
                  pl.BlockSpec((tm, d), trow),
                  pl.BlockSpec((1, 1, d), lambda i, j: (i // tpb, 0, 0)),
                  pl.BlockSpec((1, d), c2),
                  pl.BlockSpec((d, PEER_HEADS * 2 * PEER_HALF), c2),
                  pl.BlockSpec((PEER_HEADS * 2 * PEER_N_KEYS, PEER_HEADS * 2 * PEER_HALF), c2)]
        + [pl.BlockSpec((te // n_split, d), functools.partial(u_map, k)) for k in range(n_split)]
        + [pl.BlockSpec((d // n_split, te), functools.partial(v_map, k)) for k in range(n_split)],
        out_specs=pl.BlockSpec((tm, d), trow),
        out_shape=jax.ShapeDtypeStruct((n, d), F32),
        scratch_shapes=[pltpu.VMEM((PEER_HEADS * 2 * PEER_N_KEYS, tm), F32),
                        pltpu.VMEM(hs, F32), pltpu.VMEM(hs, BF16), pltpu.VMEM(hs, BF16),
                        pltpu.VMEM((d, tm), F32),
                        pltpu.VMEM((2, te, tm), F32),
                        pltpu.VMEM((2, te, tm), BF16),
                        pltpu.VMEM((PEER_TOPK, LANES), F32), pltpu.VMEM((PEER_TOPK, LANES), F32)],
        compiler_params=_cparams("parallel", "arbitrary"),
        name="peer_dense",
    )(h2, x2, g2, fg, wq, kbd, *([u_bf] * n_split), *([vt_bf] * n_split))


def _peer_key_matrix(k1, k2):
    blocks = []
    for h in range(PEER_HEADS):
        for half, kk in enumerate((k1, k2)):
            col = (2 * h + half) * PEER_HALF
            blocks.append(jnp.pad(kk, ((0, 0), (col, PEER_HEADS * 2 * PEER_HALF - col - PEER_HALF))))
    return jnp.concatenate(blocks, axis=0).astype(BF16)


def kernel(x, c, norm1_g, norm2_g, w_mod, b_mod, w_in, ssm_a_re_log, ssm_a_im, ssm_b_re, ssm_b_im, ssm_c_re, ssm_c_im, ssm_d, ssm_log_dt, ssm_w_glu, ssm_b_glu, gmlp_w_sp, gmlp_b_sp, kv_norm_g, w_uk, w_uv, w_out, peer_w_q, peer_k1, peer_k2, peer_u, peer_v, final_g):
    bsz, seq, d = x.shape
    depth = w_mod.shape[0]
    x2 = x.reshape(bsz * seq, d)
    head_of = jnp.arange(GMLP_WIDTH) // GMLP_HEAD_DIM
    pavg = ((head_of[:, None] == head_of[None, :]).astype(F32) / GMLP_HEAD_DIM).astype(BF16)
    chunk_of = jnp.arange(GMLP_BLOCK) // CHUNK
    sp_mask = chunk_of[:, None] >= chunk_of[None, :]
    for l in range(depth):
        mod = _modulation(c, w_mod[l], b_mod[l])
        sh1, sc1, g1, sh2, sc2, g2 = [mod[:, i * d:(i + 1) * d].reshape(bsz, 1, d) for i in range(6)]
        w_in_pad = jnp.pad(w_in[l], ((0, 0), (0, IN_PAD - IN_WIDTH))).astype(BF16)
        wsp = jnp.where(sp_mask[None], gmlp_w_sp[l], 0.0).astype(BF16)
        bsp = jnp.repeat(gmlp_b_sp[l].T, GMLP_HEAD_DIM, axis=1)
        ua, yb, q, ckv, qi, ki, wi, ckvt = _input_stage(
            x2, seq, norm1_g[l].reshape(1, d), sc1, sh1, w_in_pad, kv_norm_g[l].reshape(1, KV_RANK),
            pavg, wsp, bsp)
        prep = _ssm_prep(ssm_a_re_log[l], ssm_a_im[l], ssm_b_re[l], ssm_b_im[l], ssm_c_re[l], ssm_c_im[l],
                         ssm_log_dt[l], SSM_T)
        ys = _ssm_scan(ua, bsz, seq, prep)
        yc = _dsa(q, qi, wi, ki, ckv, ckvt, w_uk[l], w_uv[l], bsz, seq)
        x2, h2 = _output_stage(
            ys, ua, yb, yc, x2, seq, ssm_d[l].reshape(1, SSM_WIDTH), ssm_w_glu[l].astype(BF16),
            ssm_b_glu[l].reshape(1, SSM_WIDTH), w_out[l].astype(BF16), g1, norm2_g[l].reshape(1, d), sc2, sh2)
        x2 = _peer(h2, x2, seq, g2, final_g.reshape(1, d), peer_w_q[l].astype(BF16),
                   _peer_key_matrix(peer_k1[l], peer_k2[l]), peer_u[l].astype(BF16),
                   peer_v[l].T.astype(BF16), final_norm=(l == depth - 1))
    return x2.reshape(bsz, seq, d)
```

```python
import functools
import math

import jax
import jax.numpy as jnp
from jax import lax
from jax.experimental import pallas as pl
from jax.experimental.pallas import tpu as pltpu

CHUNK = 64
RMS_EPS = 1e-6

SSM_GROUP = 16
SSM_STATE = 64
SSM_WIDTH = 256
SSM_GROUPS = SSM_WIDTH // SSM_GROUP
SSM_T = 32
GMLP_HEADS = 4
GMLP_HEAD_DIM = 64
GMLP_WIDTH = GMLP_HEADS * GMLP_HEAD_DIM
GMLP_BLOCK = 128
ATT_HEADS = 8
ATT_HEAD_DIM = 64
ATT_WIDTH = ATT_HEADS * ATT_HEAD_DIM
KV_RANK = 128
IDX_HEADS = 4
IDX_DIM = 64
TOPK_MAX = 256

IN_SIZES = (SSM_WIDTH, GMLP_WIDTH, GMLP_WIDTH, ATT_WIDTH, KV_RANK, IDX_HEADS * IDX_DIM, IDX_DIM, IDX_HEADS)
IN_WIDTH = sum(IN_SIZES)
IN_PAD = 1792
COL_UB, COL_VB, COL_Q, COL_CKV, COL_QI, COL_KW = 256, 512, 768, 1280, 1408, 1664

PEER_HEADS = 8
PEER_N_KEYS = 128
PEER_HALF = 64
PEER_TOPK = 16

LANES = 128
VMEM_LIMIT = 56 * 1024 * 1024
NEG = -1e30
INT_MIN = -(2 ** 31)

F32 = jnp.float32
BF16 = jnp.bfloat16


def _cparams(*sem):
    return pltpu.CompilerParams(dimension_semantics=sem, vmem_limit_bytes=VMEM_LIMIT)


def _dot(a, b):
    return jnp.dot(a, b, preferred_element_type=F32)


def _dot_nt(a, b):
    return lax.dot_general(a, b, (((1,), (1,)), ((), ())), preferred_element_type=F32)


def _mod_kernel(c_ref, w_ref, b_ref, o_ref):
    c = c_ref[...]
    ca = c * jax.nn.sigmoid(c)
    o_ref[...] = _dot(ca.astype(BF16), w_ref[...].astype(BF16)) + b_ref[...]


def _modulation(c, w_mod, b_mod):
    bsz, d = c.shape
    n6 = w_mod.shape[1]
    return pl.pallas_call(
        _mod_kernel,
        grid=(n6 // d,),
        in_specs=[pl.BlockSpec((bsz, d), lambda j: (0, 0)),
                  pl.BlockSpec((d, d), lambda j: (0, j)),
                  pl.BlockSpec((1, d), lambda j: (0, j))],
        out_specs=pl.BlockSpec((bsz, d), lambda j: (0, j)),
        out_shape=jax.ShapeDtypeStruct((bsz, n6), F32),
        compiler_params=_cparams("parallel"),
        name="modulation",
    )(c, w_mod, b_mod.reshape(1, n6))


def _in_kernel(x_ref, g_ref, sc_ref, sh_ref, w_ref, kvg_ref, pavg_ref, wsp_ref, bsp_ref,
               ua_ref, yb_ref, q_ref, ckv_ref, qi_ref, ki_ref, wi_ref, ckvt_ref):
    tm = x_ref.shape[0]
    x = x_ref[...]
    ms = jnp.mean(x * x, axis=-1, keepdims=True)
    h = x * lax.rsqrt(ms + RMS_EPS) * g_ref[...]
    h = h * (1.0 + sc_ref[0]) + sh_ref[0]
    proj = _dot(h.astype(BF16), w_ref[...])

    ua_ref[...] = proj[:, 0:COL_UB]
    q_ref[...] = proj[:, COL_Q:COL_CKV].astype(BF16)
    ckv = proj[:, COL_CKV:COL_QI]
    ckv_ms = jnp.mean(ckv * ckv, axis=-1, keepdims=True)
    ckv_n = ckv * lax.rsqrt(ckv_ms + RMS_EPS) * kvg_ref[...]
    ckv_ref[...] = ckv_n.astype(BF16)
    ckvt_ref[...] = ckv_n.T.astype(BF16)
    qi_ref[...] = proj[:, COL_QI:COL_KW].astype(BF16)
    kw = proj[:, COL_KW:IN_PAD]
    ki_ref[...] = kw[:, 0:IDX_DIM].astype(BF16)
    wi_ref[...] = kw

    u_b = proj[:, COL_UB:COL_VB]
    v_b = proj[:, COL_VB:COL_Q]
    pavg = pavg_ref[...]

    def head_mean(a):
        hi = a.astype(BF16)
        lo = (a - hi.astype(F32)).astype(BF16)
        return _dot(hi, pavg) + _dot(lo, pavg)

    mu = head_mean(v_b)
    dv = v_b - mu
    var = head_mean(dv * dv)
    vn = (dv * lax.rsqrt(var + RMS_EPS)).astype(BF16)
    lane_head = lax.broadcasted_iota(jnp.int32, (GMLP_BLOCK, GMLP_WIDTH), 1) // GMLP_HEAD_DIM
    for blk in range(tm // GMLP_BLOCK):
        rows = slice(blk * GMLP_BLOCK, (blk + 1) * GMLP_BLOCK)
        vblk = vn[rows, :]
        mixed = bsp_ref[...]
        for hh in range(GMLP_HEADS):
            mixed = mixed + jnp.where(lane_head == hh, _dot(wsp_ref[hh], vblk), 0.0)
        yb_ref[rows, :] = (u_b[rows, :] * mixed).astype(BF16)


def _input_stage(x2, seq, g, sc, sh, w_in_pad, kvg, pavg, wsp, bsp, tm=512):
    n, d = x2.shape
    tpb = seq // tm
    row = lambda i: (i, 0)
    const2 = lambda i: (0, 0)
    per_b = lambda i: (i // tpb, 0, 0)
    outs = [(SSM_WIDTH, F32), (GMLP_WIDTH, BF16), (ATT_WIDTH, BF16), (KV_RANK, BF16),
            (IDX_HEADS * IDX_DIM, BF16), (IDX_DIM, BF16), (LANES, F32)]
    return pl.pallas_call(
        _in_kernel,
        grid=(n // tm,),
        in_specs=[pl.BlockSpec((tm, d), row),
                  pl.BlockSpec((1, d), const2),
                  pl.BlockSpec((1, 1, d), per_b),
                  pl.BlockSpec((1, 1, d), per_b),
                  pl.BlockSpec((d, IN_PAD), const2),
                  pl.BlockSpec((1, KV_RANK), const2),
                  pl.BlockSpec((GMLP_WIDTH, GMLP_WIDTH), const2),
                  pl.BlockSpec((GMLP_HEADS, GMLP_BLOCK, GMLP_BLOCK), lambda i: (0, 0, 0)),
                  pl.BlockSpec((GMLP_BLOCK, GMLP_WIDTH), const2)],
        out_specs=[pl.BlockSpec((tm, w), row) for w, _ in outs] + [pl.BlockSpec((KV_RANK, tm), lambda i: (0, i))],
        out_shape=[jax.ShapeDtypeStruct((n, w), dt) for w, dt in outs] + [jax.ShapeDtypeStruct((KV_RANK, n), BF16)],
        compiler_params=_cparams("parallel"),
        name="input_proj_gmlp",
    )(x2, g, sc, sh, w_in_pad, kvg, pavg, wsp, bsp)


def _ssm_prep(a_re_log, a_im, b_re, b_im, c_re, c_im, log_dt, t_len):
    hp = lax.Precision.HIGHEST
    lam_re = -jnp.exp(a_re_log)
    lam_im = a_im
    dt = jnp.exp(log_dt)[:, None]
    mag = jnp.exp(lam_re * dt)
    abar_re = mag * jnp.cos(lam_im * dt)
    abar_im = mag * jnp.sin(lam_im * dt)
    den = lam_re * lam_re + lam_im * lam_im
    p = abar_re - 1.0
    qq = abar_im
    f_re = (p * lam_re + qq * lam_im) / den
    f_im = (qq * lam_re - p * lam_im) / den
    bb_re = f_re[..., None] * b_re - f_im[..., None] * b_im
    bb_im = f_re[..., None] * b_im + f_im[..., None] * b_re
    k = jnp.arange(t_len + 1, dtype=F32)[:, None, None]
    pmag = jnp.exp(k * (lam_re * dt))
    ang = k * (lam_im * dt)
    pw_re = pmag * jnp.cos(ang)
    pw_im = pmag * jnp.sin(ang)
    cp_re = c_re[None] * pw_re[:, :, None, :] - c_im[None] * pw_im[:, :, None, :]
    cp_im = c_re[None] * pw_im[:, :, None, :] + c_im[None] * pw_re[:, :, None, :]
    kern = (jnp.einsum('kghp,gpj->kghj', cp_re[:t_len], bb_re, precision=hp)
            - jnp.einsum('kghp,gpj->kghj', cp_im[:t_len], bb_im, precision=hp))
    tt = jnp.arange(t_len)
    g_n, h_n = SSM_GROUPS, SSM_GROUP
    tw = t_len * h_n
    k_rows = jnp.pad(kern.transpose(1, 3, 0, 2).reshape(g_n, h_n, tw), ((0, 0), (0, 0), (tw, 0)))
    m = jnp.stack([k_rows[:, :, tw - j * h_n:2 * tw - j * h_n] for j in range(t_len)], axis=1)
    m = m.reshape(g_n, tw, tw)
    rev_re = pw_re[t_len - 1 - tt]
    rev_im = pw_im[t_len - 1 - tt]
    w_re = rev_re[..., None] * bb_re[None] - rev_im[..., None] * bb_im[None]
    w_im = rev_re[..., None] * bb_im[None] + rev_im[..., None] * bb_re[None]
    w_re = w_re.transpose(1, 0, 3, 2).reshape(g_n, t_len * h_n, SSM_STATE)
    w_im = w_im.transpose(1, 0, 3, 2).reshape(g_n, t_len * h_n, SSM_STATE)
    v_re = cp_re[1:].transpose(1, 3, 0, 2).reshape(g_n, SSM_STATE, t_len * h_n)
    v_im = (-cp_im[1:]).transpose(1, 3, 0, 2).reshape(g_n, SSM_STATE, t_len * h_n)
    at_re = pw_re[t_len][:, None, :]
    at_im = pw_im[t_len][:, None, :]
    return (m.astype(BF16), w_re.astype(BF16), w_im.astype(BF16), v_re.astype(BF16), v_im.astype(BF16),
            at_re, at_im)


def _ssm_kernel(u_ref, m_ref, wre_ref, wim_ref, vre_ref, vim_ref, are_ref, aim_ref, y_ref,
                lre_ref, lim_ref, pre_ref, pim_ref, *, nb, nchunks):
    u = u_ref[...]
    lre_ref[...] = _dot(u, wre_ref[...])
    lim_ref[...] = _dot(u, wim_ref[...])
    ar = are_ref[...]
    ai = aim_ref[...]
    s_re = jnp.zeros((nb, SSM_STATE), F32)
    s_im = jnp.zeros((nb, SSM_STATE), F32)
    for c in range(nchunks):
        rows = slice(c * nb, (c + 1) * nb)
        pre_ref[rows, :] = s_re
        pim_ref[rows, :] = s_im
        s_re, s_im = (ar * s_re - ai * s_im + lre_ref[rows, :],
                      ar * s_im + ai * s_re + lim_ref[rows, :])
    y_ref[...] = (_dot(u, m_ref[...])
                  + _dot(pre_ref[...].astype(BF16), vre_ref[...])
                  + _dot(pim_ref[...].astype(BF16), vim_ref[...]))


def _ssm_scan(u_a, bsz, seq, prep):
    m, w_re, w_im, v_re, v_im, at_re, at_im = prep
    t_len = SSM_T
    nchunks = seq // t_len
    r = nchunks * bsz
    tw = t_len * SSM_GROUP
    u = u_a.reshape(bsz, nchunks, t_len, SSM_GROUPS, SSM_GROUP).transpose(3, 1, 0, 2, 4)
    u = u.reshape(SSM_GROUPS, r, tw).astype(BF16)
    grp = lambda g: (g, 0, 0)
    y = pl.pallas_call(
        functools.partial(_ssm_kernel, nb=bsz, nchunks=nchunks),
        grid=(SSM_GROUPS,),
        in_specs=[pl.BlockSpec((None, r, tw), grp),
                  pl.BlockSpec((None, tw, tw), grp),
                  pl.BlockSpec((None, tw, SSM_STATE), grp),
                  pl.BlockSpec((None, tw, SSM_STATE), grp),
                  pl.BlockSpec((None, SSM_STATE, tw), grp),
                  pl.BlockSpec((None, SSM_STATE, tw), grp),
                  pl.BlockSpec((None, 1, SSM_STATE), grp),
                  pl.BlockSpec((None, 1, SSM_STATE), grp)],
        out_specs=pl.BlockSpec((None, r, tw), grp),
        out_shape=jax.ShapeDtypeStruct((SSM_GROUPS, r, tw), F32),
        scratch_shapes=[pltpu.VMEM((r, SSM_STATE), F32) for _ in range(4)],
        compiler_params=_cparams("parallel"),
        name="ssm_scan",
    )(u, m, w_re, w_im, v_re, v_im, at_re, at_im)
    y = y.reshape(SSM_GROUPS, nchunks, bsz, t_len, SSM_GROUP).transpose(2, 1, 3, 0, 4)
    return y.reshape(bsz * seq, SSM_WIDTH)


def _dsa_kernel(q_ref, qi_ref, wi_ref, ki_ref, ckv_ref, ckvt_ref, wuk_ref, wuvt_ref, tri_ref, o_ref,
                key_ref, dm_ref, qa_ref, ot_ref, *acc_refs, topk, kb_len):
    tq = q_ref.shape[0]
    t0 = pl.program_id(1) * tq
    n_kb = (t0 + tq + kb_len - 1) // kb_len
    q_pos = t0 + lax.broadcasted_iota(jnp.int32, (1, tq), 1)
    q_chunk = q_pos // CHUNK
    k_off = lax.broadcasted_iota(jnp.int32, (kb_len, 1), 0)

    def fold8(m, op):
        r = m.reshape(kb_len // 8, 8, tq)
        n = kb_len // 8
        while n > 1:
            n //= 2
            r = op(r[:n], r[n:2 * n])
        return r[0]

    def col_sum(m):
        return fold8(m, jnp.add)

    wi_t = wi_ref[...].T
    qi_h = [qi_ref[:, h * IDX_DIM:(h + 1) * IDX_DIM] for h in range(IDX_HEADS)]
    w_h = [wi_t[IDX_DIM + h:IDX_DIM + h + 1, :] * ((IDX_HEADS ** -0.5) * (IDX_DIM ** -0.5))
           for h in range(IDX_HEADS)]

    def score_block(kb, carry):
        k0 = pl.multiple_of(kb * kb_len, kb_len)
        ki = ki_ref[pl.ds(k0, kb_len), :]
        score = jnp.zeros((kb_len, tq), F32)
        for h in range(IDX_HEADS):
            score = score + jnp.maximum(_dot_nt(ki, qi_h[h]), 0.0) * w_h[h]
        score = jnp.where(score == 0.0, 0.0, score)
        bits = lax.bitcast_convert_type(score, jnp.int32)
        key = jnp.where(bits < 0, bits ^ jnp.int32(0x7FFFFFFF), bits)
        adm = ((k0 + k_off) // CHUNK) <= q_chunk
        key_ref[pl.ds(k0, kb_len), :] = jnp.where(adm, key, jnp.int32(INT_MIN))
        return carry

    lax.fori_loop(0, n_kb, score_block, 0)

    def count(pred):
        def body(kb, acc):
            k0 = pl.multiple_of(kb * kb_len, kb_len)
            return acc + col_sum(pred(key_ref[pl.ds(k0, kb_len), :]).astype(F32))
        acc = lax.fori_loop(0, n_kb, body, jnp.zeros((8, tq), F32))
        return jnp.sum(acc, axis=0, keepdims=True)

    kf = jnp.float32(topk)

    def bit_step(carry):
        step, ans, settled, _ = carry
        trial = ans + jnp.left_shift(jnp.int32(1), 31 - step)
        cnt = count(lambda kblk: kblk >= trial)
        ans = jnp.where(cnt >= kf, trial, ans)
        settled = jnp.maximum(settled, jnp.where(cnt == kf, 1.0, 0.0))
        return step + 1, ans, settled, jnp.min(settled)

    def unsettled(carry):
        return (carry[0] < 32) & (carry[3] < 0.5)

    _, ans, _, _ = lax.while_loop(
        unsettled, bit_step,
        (jnp.int32(0), jnp.full((1, tq), INT_MIN, jnp.int32), jnp.zeros((1, tq), F32), jnp.float32(0.0)))

    room = kf - count(lambda kblk: kblk > ans)
    tri = tri_ref[...]

    def select_block(kb, carry):
        k0 = pl.multiple_of(kb * kb_len, kb_len)
        kblk = key_ref[pl.ds(k0, kb_len), :]
        eq = kblk == ans
        pre = _dot(tri, eq.astype(BF16)) + carry
        sel = (kblk > ans) | (eq & (pre <= room))
        k_pos = k0 + k_off
        adm = (k_pos // CHUNK) <= q_chunk
        dist = jnp.abs(q_pos - k_pos).astype(F32)
        dm_ref[pl.ds(k0, kb_len), :] = jnp.where(sel & adm, dist, -NEG)
        return carry + jnp.sum(col_sum(eq.astype(F32)), axis=0, keepdims=True)

    lax.fori_loop(0, n_kb, select_block, jnp.zeros((1, tq), F32))

    for h in range(ATT_HEADS):
        hs = slice(h * ATT_HEAD_DIM, (h + 1) * ATT_HEAD_DIM)
        qa_ref[h] = (_dot(q_ref[:, hs], wuk_ref[h]) * (ATT_HEAD_DIM ** -0.5)).astype(BF16)
        acc_refs[h][...] = jnp.zeros((KV_RANK, tq), F32)

    def att_block(kb, carry):
        k0 = pl.multiple_of(kb * kb_len, kb_len)
        ckv = ckv_ref[pl.ds(k0, kb_len), :]
        ckvt = ckvt_ref[:, pl.ds(k0, kb_len)]
        dm = dm_ref[pl.ds(k0, kb_len), :]
        new = []
        qk = _dot_nt(ckv, qa_ref[0])
        for h in range(ATT_HEADS):
            m_run, l_run = carry[2 * h], carry[2 * h + 1]
            slope = 2.0 ** (-8.0 * (h + 1) / ATT_HEADS)
            lg = qk - slope * dm
            if h + 1 < ATT_HEADS:
                qk = _dot_nt(ckv, qa_ref[h + 1])
            m_new = jnp.maximum(m_run, jnp.max(fold8(lg, jnp.maximum), axis=0, keepdims=True))
            alpha = jnp.exp(m_run - m_new)
            p = jnp.exp(lg - m_new)
            new += [m_new, alpha * l_run + jnp.sum(col_sum(p), axis=0, keepdims=True)]
            acc_refs[h][...] = alpha * acc_refs[h][...] + _dot(ckvt, p.astype(BF16))
        return tuple(new)

    init = tuple(jnp.full((1, tq), -jnp.inf if i % 2 == 0 else 0.0, F32) for i in range(2 * ATT_HEADS))
    fin = lax.fori_loop(0, n_kb, att_block, init)
    for h in range(ATT_HEADS):
        o_t = (acc_refs[h][...] / fin[2 * h + 1]).astype(BF16)
        ot_ref[h * ATT_HEAD_DIM:(h + 1) * ATT_HEAD_DIM, :] = _dot(wuvt_ref[h], o_t)
    o_ref[...] = ot_ref[...].T.astype(o_ref.dtype)


def _dsa(q, qi, wi, ki, ckv, ckvt, w_uk, w_uv, bsz, seq, tq=256, kb_len=512):
    n = q.shape[0]
    nqt = seq // tq
    kb_len = min(kb_len, seq)
    topk = min(TOPK_MAX, seq // 4)
    tri = (jnp.arange(kb_len)[:, None] >= jnp.arange(kb_len)[None, :]).astype(BF16)
    qrow = lambda b, i: (b * nqt + i, 0)
    krow = lambda b, i: (b, 0)
    c3 = lambda b, i: (0, 0, 0)
    return pl.pallas_call(
        functools.partial(_dsa_kernel, topk=topk, kb_len=kb_len),
        grid=(bsz, nqt),
        in_specs=[pl.BlockSpec((tq, ATT_WIDTH), qrow),
                  pl.BlockSpec((tq, IDX_HEADS * IDX_DIM), qrow),
                  pl.BlockSpec((tq, LANES), qrow),
                  pl.BlockSpec((seq, IDX_DIM), krow),
                  pl.BlockSpec((seq, KV_RANK), krow),
                  pl.BlockSpec((KV_RANK, seq), lambda b, i: (0, b)),
                  pl.BlockSpec((ATT_HEADS, ATT_HEAD_DIM, KV_RANK), c3),
                  pl.BlockSpec((ATT_HEADS, ATT_HEAD_DIM, KV_RANK), c3),
                  pl.BlockSpec((kb_len, kb_len), lambda b, i: (0, 0))],
        out_specs=pl.BlockSpec((tq, ATT_WIDTH), qrow),
        out_shape=jax.ShapeDtypeStruct((n, ATT_WIDTH), BF16),
        scratch_shapes=[pltpu.VMEM((seq, tq), jnp.int32),
                        pltpu.VMEM((seq, tq), F32),
                        pltpu.VMEM((ATT_HEADS, tq, KV_RANK), BF16),
                        pltpu.VMEM((ATT_WIDTH, tq), F32)]
        + [pltpu.VMEM((KV_RANK, tq), F32) for _ in range(ATT_HEADS)],
        compiler_params=_cparams("parallel", "parallel"),
        name="dsa_attention",
    )(q, qi, wi, ki, ckv, ckvt, w_uk.astype(BF16), jnp.swapaxes(w_uv, 1, 2).astype(BF16), tri)


def _out_kernel(ys_ref, ua_ref, yb_ref, yc_ref, x_ref, d_ref, wglu_ref, bglu_ref, wo_ref, g1_ref,
                n2_ref, sc_ref, sh_ref, xo_ref, h2_ref):
    y = ys_ref[...] + d_ref[...] * ua_ref[...]
    yg = jax.nn.gelu(y)
    z = _dot(yg.astype(BF16), wglu_ref[...]) + bglu_ref[...]
    ya = yg * jax.nn.sigmoid(z)
    a_w, b_w = SSM_WIDTH, SSM_WIDTH + GMLP_WIDTH
    mix = (_dot(ya.astype(BF16), wo_ref[0:a_w, :])
           + _dot(yb_ref[...], wo_ref[a_w:b_w, :])
           + _dot(yc_ref[...], wo_ref[b_w:, :]))
    xn = x_ref[...] + g1_ref[0] * mix
    xo_ref[...] = xn
    ms = jnp.mean(xn * xn, axis=-1, keepdims=True)
    h = xn * lax.rsqrt(ms + RMS_EPS) * n2_ref[...]
    h2_ref[...] = (h * (1.0 + sc_ref[0]) + sh_ref[0]).astype(BF16)


def _output_stage(ys, ua, yb, yc, x2, seq, d_skip, w_glu, b_glu, w_out, g1, n2g, sc2, sh2, tm=512):
    n, d = x2.shape
    tpb = seq // tm
    row = lambda i: (i, 0)
    const2 = lambda i: (0, 0)
    per_b = lambda i: (i // tpb, 0, 0)
    return pl.pallas_call(
        _out_kernel,
        grid=(n // tm,),
        in_specs=[pl.BlockSpec((tm, SSM_WIDTH), row),
                  pl.BlockSpec((tm, SSM_WIDTH), row),
                  pl.BlockSpec((tm, GMLP_WIDTH), row),
                  pl.BlockSpec((tm, ATT_WIDTH), row),
                  pl.BlockSpec((tm, d), row),
                  pl.BlockSpec((1, SSM_WIDTH), const2),
                  pl.BlockSpec((SSM_WIDTH, SSM_WIDTH), const2),
                  pl.BlockSpec((1, SSM_WIDTH), const2),
                  pl.BlockSpec((d, d), const2),
                  pl.BlockSpec((1, 1, d), per_b),
                  pl.BlockSpec((1, d), const2),
                  pl.BlockSpec((1, 1, d), per_b),
                  pl.BlockSpec((1, 1, d), per_b)],
        out_specs=[pl.BlockSpec((tm, d), row), pl.BlockSpec((tm, d), row)],
        out_shape=[jax.ShapeDtypeStruct((n, d), F32), jax.ShapeDtypeStruct((n, d), BF16)],
        compiler_params=_cparams("parallel"),
        name="out_proj_norm2",
    )(ys, ua, yb, yc, x2, d_skip, w_glu, b_glu, w_out, g1, n2g, sc2, sh2)


def _extract_top(x, n_top, with_rank=False):
    tops = []
    rank = jnp.full(x.shape, LANES - 1, F32) if with_rank else None
    for k in range(n_top):
        mx = jnp.max(x, axis=0, keepdims=True)
        tops.append(mx)
        hit = x == mx
        if with_rank:
            rank = jnp.where(hit, float(k), rank)
        x = jnp.where(hit, -jnp.inf, x)
    return (tops, rank) if with_rank else tops


def _peer_kernel(h2_ref, x_ref, g2_ref, fg_ref, wq_ref, kbd_ref, *rest, final_norm, n_split):
    u_refs, vt_refs = rest[:n_split], rest[n_split:2 * n_split]
    o_ref, st_ref, e1_ref, e2_ref, r2_ref, acc_ref, at_ref, ga_ref, v1_ref, v2_ref = rest[2 * n_split:]
    j = pl.program_id(1)
    tm = h2_ref.shape[0]
    u_rows = u_refs[0].shape[0]
    v_rows = vt_refs[0].shape[0]
    te = u_rows * n_split
    n1 = te // PEER_N_KEYS
    nk = PEER_N_KEYS
    h2 = h2_ref[...]

    @pl.when(j == 0)
    def _():
        q = _dot(h2, wq_ref[...]).astype(BF16)
        st_ref[...] = _dot_nt(kbd_ref[...], q)

        def per_head(h, carry):
            base = pl.multiple_of(h * 2 * nk, 2 * nk)
            for lc in range(tm // LANES):
                ls = slice(lc * LANES, (lc + 1) * LANES)
                t1 = _extract_top(st_ref[pl.ds(base, nk), ls], PEER_TOPK + 1)
                t2, rank2 = _extract_top(st_ref[pl.ds(base + nk, nk), ls], PEER_TOPK + 1, with_rank=True)
                for k in range(PEER_TOPK):
                    v1_ref[k:k + 1, :] = t1[k]
                    v2_ref[k:k + 1, :] = t2[k]
                v2_all = v2_ref[0:PEER_TOPK, :]
                v2_top = v2_ref[0:8, :]
                cand = [t1[0] + v2_all]
                cand += [t1[a] + v2_top for a in range(1, 8)]
                cand.append(v1_ref[8:PEER_TOPK, :] + t2[0])
                top = _extract_top(jnp.concatenate(cand, axis=0), PEER_TOPK + 1)
                z = jnp.zeros_like(top[0])
                for k in range(PEER_TOPK):
                    z = z + jnp.exp(top[k] - top[0])
                nxt = jnp.maximum(top[PEER_TOPK], jnp.maximum(t1[PEER_TOPK] + t2[0], t1[0] + t2[PEER_TOPK]))
                thr = 0.5 * (top[PEER_TOPK - 1] + nxt)
                s1 = st_ref[pl.ds(base, nk), ls]
                s2 = st_ref[pl.ds(base + nk, nk), ls]
                e1_ref[h, :, ls] = jnp.exp(s1 - t1[0]) * (0.5 / z)
                e2_ref[h, :, ls] = jnp.exp(s2 - t2[0]).astype(BF16)
                r2_ref[h, :, ls] = rank2.astype(BF16)
                n_sel = jnp.zeros((nk, LANES), F32)
                for a in range(PEER_TOPK):
                    n_a = jnp.zeros_like(thr)
                    for b in range(PEER_TOPK // (a + 1)):
                        n_a = n_a + jnp.where(t1[a] + t2[b] >= thr, 1.0, 0.0)
                    n_sel = jnp.where(s1 == t1[a], n_a, n_sel)
                st_ref[pl.ds(base, nk), ls] = n_sel
            return carry

        lax.fori_loop(0, PEER_HEADS, per_head, 0)
        acc_ref[...] = jnp.zeros_like(acc_ref)
        ga_ref[1] = jnp.zeros(ga_ref.shape[1:], BF16)
        for k in range(n_split):
            at_ref[0, k * u_rows:(k + 1) * u_rows, :] = _dot_nt(u_refs[k][...], h2)

    n_blocks = pl.num_programs(1) - 2
    rd = j % 2
    wr = 1 - rd

    @pl.when((j >= 1) & (j <= n_blocks))
    def _():
        for k in range(n_split):
            at_ref[rd, k * u_rows:(k + 1) * u_rows, :] = _dot_nt(u_refs[k][...], h2)
        for k in range(n_split):
            acc_ref[k * v_rows:(k + 1) * v_rows, :] += _dot(vt_refs[k][...], ga_ref[rd])
        r0 = pl.multiple_of((j - 1) * n1, n1)
        for lc in range(tm // LANES):
            ls = slice(lc * LANES, (lc + 1) * LANES)
            for il in range(n1):
                g = jnp.zeros((nk, LANES), BF16)
                for h in range(PEER_HEADS):
                    n_row = st_ref[pl.ds(h * 2 * nk + r0, n1), ls][il:il + 1, :].astype(BF16)
                    e_row = e1_ref[h, pl.ds(r0, n1), ls][il:il + 1, :].astype(BF16)
                    picked = jnp.where(r2_ref[h, :, ls] < n_row, e2_ref[h, :, ls], jnp.zeros((), BF16))
                    g = g + picked * e_row
                a = at_ref[wr, il * nk:(il + 1) * nk, ls]
                inner = a * (0.7978845608028654 + 0.035677408136300125 * (a * a))
                act2 = a + a * jnp.tanh(inner)
                ga_ref[wr, il * nk:(il + 1) * nk, ls] = g * act2.astype(BF16)

    @pl.when(j == n_blocks + 1)
    def _():
        for k in range(n_split):
            acc_ref[k * v_rows:(k + 1) * v_rows, :] += _dot(vt_refs[k][...], ga_ref[rd])
        out = x_ref[...] + g2_ref[0] * acc_ref[...].T
        if final_norm:
            ms = jnp.mean(out * out, axis=-1, keepdims=True)
            out = out * lax.rsqrt(ms + RMS_EPS) * fg_ref[...]
        o_ref[...] = out


def _peer(h2, x2, seq, g2, fg, wq, kbd, u_bf, vt_bf, final_norm, tm=512, te=1024, n_split=4):
    n, d = x2.shape
    n_blk = u_bf.shape[0] // te
    tpb = seq // tm
    trow = lambda i, j: (i, 0)
    c2 = lambda i, j: (0, 0)
    hs = (PEER_HEADS, PEER_N_KEYS, tm)
    u_map = lambda k, i, j: (jnp.minimum(j, n_blk - 1) * n_split + k, 0)
    v_map = lambda k, i, j: (k, jnp.clip(j - 2, 0, n_blk - 1))
    return pl.pallas_call(
        functools.partial(_peer_kernel, final_norm=final_norm, n_split=n_split),
        grid=(n // tm, n_blk + 2),
        in_specs=[pl.BlockSpec((tm, d), trow),
                  pl.BlockSpec((tm, d), trow),
                  pl.BlockSpec((1, 1, d), lambda i, j: (i // tpb, 0, 0)),
                  pl.BlockSpec((1, d), c2),
                  pl.BlockSpec((d, PEER_HEADS * 2 * PEER_HALF), c2),
                  pl.BlockSpec((PEER_HEADS * 2 * PEER_N_KEYS, PEER_HEADS * 2 * PEER_HALF), c2)]
        + [pl.BlockSpec((te // n_split, d), functools.partial(u_map, k)) for k in range(n_split)]
        + [pl.BlockSpec((d // n_split, te), functools.partial(v_map, k)) for k in range(n_split)],
        out_specs=pl.BlockSpec((tm, d), trow),
        out_shape=jax.ShapeDtypeStruct((n, d), F32),
        scratch_shapes=[pltpu.VMEM((PEER_HEADS * 2 * PEER_N_KEYS, tm), F32),
                        pltpu.VMEM(hs, F32), pltpu.VMEM(hs, BF16), pltpu.VMEM(hs, BF16),
                        pltpu.VMEM((d, tm), F32),
                        pltpu.VMEM((2, te, tm), F32),
                        pltpu.VMEM((2, te, tm), BF16),
                        pltpu.VMEM((PEER_TOPK, LANES), F32), pltpu.VMEM((PEER_TOPK, LANES), F32)],
        compiler_params=_cparams("parallel", "arbitrary"),
        name="peer_dense",
    )(h2, x2, g2, fg, wq, kbd, *([u_bf] * n_split), *([vt_bf] * n_split))


def _peer_key_matrix(k1, k2):
    blocks = []
    for h in range(PEER_HEADS):
        for half, kk in enumerate((k1, k2)):
            col = (2 * h + half) * PEER_HALF
            blocks.append(jnp.pad(kk, ((0, 0), (col, PEER_HEADS * 2 * PEER_HALF - col - PEER_HALF))))
    return jnp.concatenate(blocks, axis=0).astype(BF16)


def kernel(x, c, norm1_g, norm2_g, w_mod, b_mod, w_in, ssm_a_re_log, ssm_a_im, ssm_b_re, ssm_b_im, ssm_c_re, ssm_c_im, ssm_d, ssm_log_dt, ssm_w_glu, ssm_b_glu, gmlp_w_sp, gmlp_b_sp, kv_norm_g, w_uk, w_uv, w_out, peer_w_q, peer_k1, peer_k2, peer_u, peer_v, final_g):
    bsz, seq, d = x.shape
    depth = w_mod.shape[0]
    x2 = x.reshape(bsz * seq, d)
    head_of = jnp.arange(GMLP_WIDTH) // GMLP_HEAD_DIM
    pavg = ((head_of[:, None] == head_of[None, :]).astype(F32) / GMLP_HEAD_DIM).astype(BF16)
    chunk_of = jnp.arange(GMLP_BLOCK) // CHUNK
    sp_mask = chunk_of[:, None] >= chunk_of[None, :]
    for l in range(depth):
        mod = _modulation(c, w_mod[l], b_mod[l])
        sh1, sc1, g1, sh2, sc2, g2 = [mod[:, i * d:(i + 1) * d].reshape(bsz, 1, d) for i in range(6)]
        w_in_pad = jnp.pad(w_in[l], ((0, 0), (0, IN_PAD - IN_WIDTH))).astype(BF16)
        wsp = jnp.where(sp_mask[None], gmlp_w_sp[l], 0.0).astype(BF16)
        bsp = jnp.repeat(gmlp_b_sp[l].T, GMLP_HEAD_DIM, axis=1)
        ua, yb, q, ckv, qi, ki, wi, ckvt = _input_stage(
            x2, seq, norm1_g[l].reshape(1, d), sc1, sh1, w_in_pad, kv_norm_g[l].reshape(1, KV_RANK),
            pavg, wsp, bsp)
        prep = _ssm_prep(ssm_a_re_log[l], ssm_a_im[l], ssm_b_re[l], ssm_b_im[l], ssm_c_re[l], ssm_c_im[l],
                         ssm_log_dt[l], SSM_T)
        ys = _ssm_scan(ua, bsz, seq, prep)
        yc = _dsa(q, qi, wi, ki, ckv, ckvt, w_uk[l], w_uv[l], bsz, seq)
        x2, h2 = _output_stage(
            ys, ua, yb, yc, x2, seq, ssm_d[l].reshape(1, SSM_WIDTH), ssm_w_glu[l].astype(BF16),
            ssm_b_glu[l].reshape(1, SSM_WIDTH), w_out[l].astype(BF16), g1, norm2_g[l].reshape(1, d), sc2, sh2)
        x2 = _peer(h2, x2, seq, g2, final_g.reshape(1, d), peer_w_q[l].astype(BF16),
                   _peer_key_matrix(peer_k1[l], peer_k2[l]), peer_u[l].astype(BF16),
                   peer_v[l].T.astype(BF16), final_norm=(l == depth - 1))
    return x2.reshape(bsz, seq, d)
```

```python
import functools
import math

import jax
import jax.numpy as jnp
from jax import lax
from jax.experimental import pallas as pl
from jax.experimental.pallas import tpu as pltpu

CHUNK = 64
RMS_EPS = 1e-6

SSM_GROUP = 16
SSM_STATE = 64
SSM_WIDTH = 256
SSM_GROUPS = SSM_WIDTH // SSM_GROUP
SSM_T = 32
GMLP_HEADS = 4
GMLP_HEAD_DIM = 64
GMLP_WIDTH = GMLP_HEADS * GMLP_HEAD_DIM
GMLP_BLOCK = 128
ATT_HEADS = 8
ATT_HEAD_DIM = 64
ATT_WIDTH = ATT_HEADS * ATT_HEAD_DIM
KV_RANK = 128
IDX_HEADS = 4
IDX_DIM = 64
TOPK_MAX = 256

IN_SIZES = (SSM_WIDTH, GMLP_WIDTH, GMLP_WIDTH, ATT_WIDTH, KV_RANK, IDX_HEADS * IDX_DIM, IDX_DIM, IDX_HEADS)
IN_WIDTH = sum(IN_SIZES)
IN_PAD = 1792
COL_UB, COL_VB, COL_Q, COL_CKV, COL_QI, COL_KW = 256, 512, 768, 1280, 1408, 1664

PEER_HEADS = 8
PEER_N_KEYS = 128
PEER_HALF = 64
PEER_TOPK = 16

LANES = 128
VMEM_LIMIT = 56 * 1024 * 1024
NEG = -1e30
INT_MIN = -(2 ** 31)

F32 = jnp.float32
BF16 = jnp.bfloat16


def _cparams(*sem):
    return pltpu.CompilerParams(dimension_semantics=sem, vmem_limit_bytes=VMEM_LIMIT)


def _dot(a, b):
    return jnp.dot(a, b, preferred_element_type=F32)


def _dot_nt(a, b):
    return lax.dot_general(a, b, (((1,), (1,)), ((), ())), preferred_element_type=F32)


def _mod_kernel(c_ref, w_ref, b_ref, o_ref):
    c = c_ref[...]
    ca = c * jax.nn.sigmoid(c)
    o_ref[...] = _dot(ca.astype(BF16), w_ref[...].astype(BF16)) + b_ref[...]


def _modulation(c, w_mod, b_mod):
    bsz, d = c.shape
    n6 = w_mod.shape[1]
    return pl.pallas_call(
        _mod_kernel,
        grid=(n6 // d,),
        in_specs=[pl.BlockSpec((bsz, d), lambda j: (0, 0)),
                  pl.BlockSpec((d, d), lambda j: (0, j)),
                  pl.BlockSpec((1, d), lambda j: (0, j))],
        out_specs=pl.BlockSpec((bsz, d), lambda j: (0, j)),
        out_shape=jax.ShapeDtypeStruct((bsz, n6), F32),
        compiler_params=_cparams("parallel"),
        name="modulation",
    )(c, w_mod, b_mod.reshape(1, n6))


def _in_kernel(x_ref, g_ref, sc_ref, sh_ref, w_ref, kvg_ref, pavg_ref, wsp_ref, bsp_ref,
               ua_ref, yb_ref, q_ref, ckv_ref, qi_ref, ki_ref, wi_ref, ckvt_ref):
    tm = x_ref.shape[0]
    x = x_ref[...]
    ms = jnp.mean(x * x, axis=-1, keepdims=True)
    h = x * lax.rsqrt(ms + RMS_EPS) * g_ref[...]
    h = h * (1.0 + sc_ref[0]) + sh_ref[0]
    proj = _dot(h.astype(BF16), w_ref[...])

    ua_ref[...] = proj[:, 0:COL_UB]
    q_ref[...] = proj[:, COL_Q:COL_CKV].astype(BF16)
    ckv = proj[:, COL_CKV:COL_QI]
    ckv_ms = jnp.mean(ckv * ckv, axis=-1, keepdims=True)
    ckv_n = ckv * lax.rsqrt(ckv_ms + RMS_EPS) * kvg_ref[...]
    ckv_ref[...] = ckv_n.astype(BF16)
    ckvt_ref[...] = ckv_n.T.astype(BF16)
    qi_ref[...] = proj[:, COL_QI:COL_KW].astype(BF16)
    kw = proj[:, COL_KW:IN_PAD]
    ki_ref[...] = kw[:, 0:IDX_DIM].astype(BF16)
    wi_ref[...] = kw

    u_b = proj[:, COL_UB:COL_VB]
    v_b = proj[:, COL_VB:COL_Q]
    pavg = pavg_ref[...]

    def head_mean(a):
        hi = a.astype(BF16)
        lo = (a - hi.astype(F32)).astype(BF16)
        return _dot(hi, pavg) + _dot(lo, pavg)

    mu = head_mean(v_b)
    dv = v_b - mu
    var = head_mean(dv * dv)
    vn = (dv * lax.rsqrt(var + RMS_EPS)).astype(BF16)
    lane_head = lax.broadcasted_iota(jnp.int32, (GMLP_BLOCK, GMLP_WIDTH), 1) // GMLP_HEAD_DIM
    for blk in range(tm // GMLP_BLOCK):
        rows = slice(blk * GMLP_BLOCK, (blk + 1) * GMLP_BLOCK)
        vblk = vn[rows, :]
        mixed = bsp_ref[...]
        for hh in range(GMLP_HEADS):
            mixed = mixed + jnp.where(lane_head == hh, _dot(wsp_ref[hh], vblk), 0.0)
        yb_ref[rows, :] = (u_b[rows, :] * mixed).astype(BF16)


def _input_stage(x2, seq, g, sc, sh, w_in_pad, kvg, pavg, wsp, bsp, tm=512):
    n, d = x2.shape
    tpb = seq // tm
    row = lambda i: (i, 0)
    const2 = lambda i: (0, 0)
    per_b = lambda i: (i // tpb, 0, 0)
    outs = [(SSM_WIDTH, F32), (GMLP_WIDTH, BF16), (ATT_WIDTH, BF16), (KV_RANK, BF16),
            (IDX_HEADS * IDX_DIM, BF16), (IDX_DIM, BF16), (LANES, F32)]
    return pl.pallas_call(
        _in_kernel,
        grid=(n // tm,),
        in_specs=[pl.BlockSpec((tm, d), row),
                  pl.BlockSpec((1, d), const2),
                  pl.BlockSpec((1, 1, d), per_b),
                  pl.BlockSpec((1, 1, d), per_b),
                  pl.BlockSpec((d, IN_PAD), const2),
                  pl.BlockSpec((1, KV_RANK), const2),
                  pl.BlockSpec((GMLP_WIDTH, GMLP_WIDTH), const2),
                  pl.BlockSpec((GMLP_HEADS, GMLP_BLOCK, GMLP_BLOCK), lambda i: (0, 0, 0)),
                  pl.BlockSpec((GMLP_BLOCK, GMLP_WIDTH), const2)],
        out_specs=[pl.BlockSpec((tm, w), row) for w, _ in outs] + [pl.BlockSpec((KV_RANK, tm), lambda i: (0, i))],
        out_shape=[jax.ShapeDtypeStruct((n, w), dt) for w, dt in outs] + [jax.ShapeDtypeStruct((KV_RANK, n), BF16)],
        compiler_params=_cparams("parallel"),
        name="input_proj_gmlp",
    )(x2, g, sc, sh, w_in_pad, kvg, pavg, wsp, bsp)


def _ssm_prep(a_re_log, a_im, b_re, b_im, c_re, c_im, log_dt, t_len):
    hp = lax.Precision.HIGHEST
    lam_re = -jnp.exp(a_re_log)
    lam_im = a_im
    dt = jnp.exp(log_dt)[:, None]
    mag = jnp.exp(lam_re * dt)
    abar_re = mag * jnp.cos(lam_im * dt)
    abar_im = mag * jnp.sin(lam_im * dt)
    den = lam_re * lam_re + lam_im * lam_im
    p = abar_re - 1.0
    qq = abar_im
    f_re = (p * lam_re + qq * lam_im) / den
    f_im = (qq * lam_re - p * lam_im) / den
    bb_re = f_re[..., None] * b_re - f_im[..., None] * b_im
    bb_im = f_re[..., None] * b_im + f_im[..., None] * b_re
    k = jnp.arange(t_len + 1, dtype=F32)[:, None, None]
    pmag = jnp.exp(k * (lam_re * dt))
    ang = k * (lam_im * dt)
    pw_re = pmag * jnp.cos(ang)
    pw_im = pmag * jnp.sin(ang)
    cp_re = c_re[None] * pw_re[:, :, None, :] - c_im[None] * pw_im[:, :, None, :]
    cp_im = c_re[None] * pw_im[:, :, None, :] + c_im[None] * pw_re[:, :, None, :]
    kern = (jnp.einsum('kghp,gpj->kghj', cp_re[:t_len], bb_re, precision=hp)
            - jnp.einsum('kghp,gpj->kghj', cp_im[:t_len], bb_im, precision=hp))
    tt = jnp.arange(t_len)
    g_n, h_n = SSM_GROUPS, SSM_GROUP
    tw = t_len * h_n
    k_rows = jnp.pad(kern.transpose(1, 3, 0, 2).reshape(g_n, h_n, tw), ((0, 0), (0, 0), (tw, 0)))
    m = jnp.stack([k_rows[:, :, tw - j * h_n:2 * tw - j * h_n] for j in range(t_len)], axis=1)
    m = m.reshape(g_n, tw, tw)
    rev_re = pw_re[t_len - 1 - tt]
    rev_im = pw_im[t_len - 1 - tt]
    w_re = rev_re[..., None] * bb_re[None] - rev_im[..., None] * bb_im[None]
    w_im = rev_re[..., None] * bb_im[None] + rev_im[..., None] * bb_re[None]
    w_re = w_re.transpose(1, 0, 3, 2).reshape(g_n, t_len * h_n, SSM_STATE)
    w_im = w_im.transpose(1, 0, 3, 2).reshape(g_n, t_len * h_n, SSM_STATE)
    v_re = cp_re[1:].transpose(1, 3, 0, 2).reshape(g_n, SSM_STATE, t_len * h_n)
    v_im = (-cp_im[1:]).transpose(1, 3, 0, 2).reshape(g_n, SSM_STATE, t_len * h_n)
    at_re = pw_re[t_len][:, None, :]
    at_im = pw_im[t_len][:, None, :]
    return (m.astype(BF16), w_re.astype(BF16), w_im.astype(BF16), v_re.astype(BF16), v_im.astype(BF16),
            at_re, at_im)


def _ssm_kernel(u_ref, m_ref, wre_ref, wim_ref, vre_ref, vim_ref, are_ref, aim_ref, y_ref,
                lre_ref, lim_ref, pre_ref, pim_ref, *, nb, nchunks):
    u = u_ref[...]
    lre_ref[...] = _dot(u, wre_ref[...])
    lim_ref[...] = _dot(u, wim_ref[...])
    ar = are_ref[...]
    ai = aim_ref[...]
    s_re = jnp.zeros((nb, SSM_STATE), F32)
    s_im = jnp.zeros((nb, SSM_STATE), F32)
    for c in range(nchunks):
        rows = slice(c * nb, (c + 1) * nb)
        pre_ref[rows, :] = s_re
        pim_ref[rows, :] = s_im
        s_re, s_im = (ar * s_re - ai * s_im + lre_ref[rows, :],
                      ar * s_im + ai * s_re + lim_ref[rows, :])
    y_ref[...] = (_dot(u, m_ref[...])
                  + _dot(pre_ref[...].astype(BF16), vre_ref[...])
                  + _dot(pim_ref[...].astype(BF16), vim_ref[...]))


def _ssm_scan(u_a, bsz, seq, prep):
    m, w_re, w_im, v_re, v_im, at_re, at_im = prep
    t_len = SSM_T
    nchunks = seq // t_len
    r = nchunks * bsz
    tw = t_len * SSM_GROUP
    u = u_a.reshape(bsz, nchunks, t_len, SSM_GROUPS, SSM_GROUP).transpose(3, 1, 0, 2, 4)
    u = u.reshape(SSM_GROUPS, r, tw).astype(BF16)
    grp = lambda g: (g, 0, 0)
    y = pl.pallas_call(
        functools.partial(_ssm_kernel, nb=bsz, nchunks=nchunks),
        grid=(SSM_GROUPS,),
        in_specs=[pl.BlockSpec((None, r, tw), grp),
                  pl.BlockSpec((None, tw, tw), grp),
                  pl.BlockSpec((None, tw, SSM_STATE), grp),
                  pl.BlockSpec((None, tw, SSM_STATE), grp),
                  pl.BlockSpec((None, SSM_STATE, tw), grp),
                  pl.BlockSpec((None, SSM_STATE, tw), grp),
                  pl.BlockSpec((None, 1, SSM_STATE), grp),
                  pl.BlockSpec((None, 1, SSM_STATE), grp)],
        out_specs=pl.BlockSpec((None, r, tw), grp),
        out_shape=jax.ShapeDtypeStruct((SSM_GROUPS, r, tw), F32),
        scratch_shapes=[pltpu.VMEM((r, SSM_STATE), F32) for _ in range(4)],
        compiler_params=_cparams("parallel"),
        name="ssm_scan",
    )(u, m, w_re, w_im, v_re, v_im, at_re, at_im)
    y = y.reshape(SSM_GROUPS, nchunks, bsz, t_len, SSM_GROUP).transpose(2, 1, 3, 0, 4)
    return y.reshape(bsz * seq, SSM_WIDTH)


def _dsa_kernel(q_ref, qi_ref, wi_ref, ki_ref, ckv_ref, ckvt_ref, wuk_ref, wuvt_ref, tri_ref, o_ref,
                key_ref, dm_ref, qa_ref, ot_ref, *acc_refs, topk, kb_len):
    tq = q_ref.shape[0]
    t0 = pl.program_id(1) * tq
    n_kb = (t0 + tq + kb_len - 1) // kb_len
    q_pos = t0 + lax.broadcasted_iota(jnp.int32, (1, tq), 1)
    q_chunk = q_pos // CHUNK
    k_off = lax.broadcasted_iota(jnp.int32, (kb_len, 1), 0)

    def fold8(m, op):
        r = m.reshape(kb_len // 8, 8, tq)
        n = kb_len // 8
        while n > 1:
            n //= 2
            r = op(r[:n], r[n:2 * n])
        return r[0]

    def col_sum(m):
        return fold8(m, jnp.add)

    wi_t = wi_ref[...].T
    qi_h = [qi_ref[:, h * IDX_DIM:(h + 1) * IDX_DIM] for h in range(IDX_HEADS)]
    w_h = [wi_t[IDX_DIM + h:IDX_DIM + h + 1, :] * ((IDX_HEADS ** -0.5) * (IDX_DIM ** -0.5))
           for h in range(IDX_HEADS)]

    def score_block(kb, carry):
        k0 = pl.multiple_of(kb * kb_len, kb_len)
        ki = ki_ref[pl.ds(k0, kb_len), :]
        score = jnp.zeros((kb_len, tq), F32)
        for h in range(IDX_HEADS):
            score = score + jnp.maximum(_dot_nt(ki, qi_h[h]), 0.0) * w_h[h]
        score = jnp.where(score == 0.0, 0.0, score)
        bits = lax.bitcast_convert_type(score, jnp.int32)
        key = jnp.where(bits < 0, bits ^ jnp.int32(0x7FFFFFFF), bits)
        adm = ((k0 + k_off) // CHUNK) <= q_chunk
        key_ref[pl.ds(k0, kb_len), :] = jnp.where(adm, key, jnp.int32(INT_MIN))
        return carry

    lax.fori_loop(0, n_kb, score_block, 0)

    def count(pred):
        def body(kb, acc):
            k0 = pl.multiple_of(kb * kb_len, kb_len)
            return acc + col_sum(pred(key_ref[pl.ds(k0, kb_len), :]).astype(F32))
        acc = lax.fori_loop(0, n_kb, body, jnp.zeros((8, tq), F32))
        return jnp.sum(acc, axis=0, keepdims=True)

    kf = jnp.float32(topk)

    def bit_step(step, ans):
        trial = ans + jnp.left_shift(jnp.int32(1), 31 - step)
        return jnp.where(count(lambda kblk: kblk >= trial) >= kf, trial, ans)

    ans = lax.fori_loop(0, 32, bit_step, jnp.full((1, tq), INT_MIN, jnp.int32))

    room = kf - count(lambda kblk: kblk > ans)
    tri = tri_ref[...]

    def select_block(kb, carry):
        k0 = pl.multiple_of(kb * kb_len, kb_len)
        kblk = key_ref[pl.ds(k0, kb_len), :]
        eq = kblk == ans
        pre = _dot(tri, eq.astype(BF16)) + carry
        sel = (kblk > ans) | (eq & (pre <= room))
        k_pos = k0 + k_off
        adm = (k_pos // CHUNK) <= q_chunk
        dist = jnp.abs(q_pos - k_pos).astype(F32)
        dm_ref[pl.ds(k0, kb_len), :] = jnp.where(sel & adm, dist, -NEG)
        return carry + jnp.sum(col_sum(eq.astype(F32)), axis=0, keepdims=True)

    lax.fori_loop(0, n_kb, select_block, jnp.zeros((1, tq), F32))

    for h in range(ATT_HEADS):
        hs = slice(h * ATT_HEAD_DIM, (h + 1) * ATT_HEAD_DIM)
        qa_ref[h] = (_dot(q_ref[:, hs], wuk_ref[h]) * (ATT_HEAD_DIM ** -0.5)).astype(BF16)
        acc_refs[h][...] = jnp.zeros((KV_RANK, tq), F32)

    def att_block(kb, carry):
        k0 = pl.multiple_of(kb * kb_len, kb_len)
        ckv = ckv_ref[pl.ds(k0, kb_len), :]
        ckvt = ckvt_ref[:, pl.ds(k0, kb_len)]
        dm = dm_ref[pl.ds(k0, kb_len), :]
        new = []
        qk = _dot_nt(ckv, qa_ref[0])
        for h in range(ATT_HEADS):
            m_run, l_run = carry[2 * h], carry[2 * h + 1]
            slope = 2.0 ** (-8.0 * (h + 1) / ATT_HEADS)
            lg = qk - slope * dm
            if h + 1 < ATT_HEADS:
                qk = _dot_nt(ckv, qa_ref[h + 1])
            m_new = jnp.maximum(m_run, jnp.max(fold8(lg, jnp.maximum), axis=0, keepdims=True))
            alpha = jnp.exp(m_run - m_new)
            p = jnp.exp(lg - m_new)
            new += [m_new, alpha * l_run + jnp.sum(col_sum(p), axis=0, keepdims=True)]
            acc_refs[h][...] = alpha * acc_refs[h][...] + _dot(ckvt, p.astype(BF16))
        return tuple(new)

    init = tuple(jnp.full((1, tq), -jnp.inf if i % 2 == 0 else 0.0, F32) for i in range(2 * ATT_HEADS))
    fin = lax.fori_loop(0, n_kb, att_block, init)
    for h in range(ATT_HEADS):
        o_t = (acc_refs[h][...] / fin[2 * h + 1]).astype(BF16)
        ot_ref[h * ATT_HEAD_DIM:(h + 1) * ATT_HEAD_DIM, :] = _dot(wuvt_ref[h], o_t)
    o_ref[...] = ot_ref[...].T.astype(o_ref.dtype)


def _dsa(q, qi, wi, ki, ckv, ckvt, w_uk, w_uv, bsz, seq, tq=256, kb_len=512):
    n = q.shape[0]
    nqt = seq // tq
    kb_len = min(kb_len, seq)
    topk = min(TOPK_MAX, seq // 4)
    tri = (jnp.arange(kb_len)[:, None] >= jnp.arange(kb_len)[None, :]).astype(BF16)
    qrow = lambda b, i: (b * nqt + i, 0)
    krow = lambda b, i: (b, 0)
    c3 = lambda b, i: (0, 0, 0)
    return pl.pallas_call(
        functools.partial(_dsa_kernel, topk=topk, kb_len=kb_len),
        grid=(bsz, nqt),
        in_specs=[pl.BlockSpec((tq, ATT_WIDTH), qrow),
                  pl.BlockSpec((tq, IDX_HEADS * IDX_DIM), qrow),
                  pl.BlockSpec((tq, LANES), qrow),
                  pl.BlockSpec((seq, IDX_DIM), krow),
                  pl.BlockSpec((seq, KV_RANK), krow),
                  pl.BlockSpec((KV_RANK, seq), lambda b, i: (0, b)),
                  pl.BlockSpec((ATT_HEADS, ATT_HEAD_DIM, KV_RANK), c3),
                  pl.BlockSpec((ATT_HEADS, ATT_HEAD_DIM, KV_RANK), c3),
                  pl.BlockSpec((kb_len, kb_len), lambda b, i: (0, 0))],
        out_specs=pl.BlockSpec((tq, ATT_WIDTH), qrow),
        out_shape=jax.ShapeDtypeStruct((n, ATT_WIDTH), BF16),
        scratch_shapes=[pltpu.VMEM((seq, tq), jnp.int32),
                        pltpu.VMEM((seq, tq), F32),
                        pltpu.VMEM((ATT_HEADS, tq, KV_RANK), BF16),
                        pltpu.VMEM((ATT_WIDTH, tq), F32)]
        + [pltpu.VMEM((KV_RANK, tq), F32) for _ in range(ATT_HEADS)],
        compiler_params=_cparams("parallel", "parallel"),
        name="dsa_attention",
    )(q, qi, wi, ki, ckv, ckvt, w_uk.astype(BF16), jnp.swapaxes(w_uv, 1, 2).astype(BF16), tri)


def _out_kernel(ys_ref, ua_ref, yb_ref, yc_ref, x_ref, d_ref, wglu_ref, bglu_ref, wo_ref, g1_ref,
                n2_ref, sc_ref, sh_ref, xo_ref, h2_ref):
    y = ys_ref[...] + d_ref[...] * ua_ref[...]
    yg = jax.nn.gelu(y)
    z = _dot(yg.astype(BF16), wglu_ref[...]) + bglu_ref[...]
    ya = yg * jax.nn.sigmoid(z)
    a_w, b_w = SSM_WIDTH, SSM_WIDTH + GMLP_WIDTH
    mix = (_dot(ya.astype(BF16), wo_ref[0:a_w, :])
           + _dot(yb_ref[...], wo_ref[a_w:b_w, :])
           + _dot(yc_ref[...], wo_ref[b_w:, :]))
    xn = x_ref[...] + g1_ref[0] * mix
    xo_ref[...] = xn
    ms = jnp.mean(xn * xn, axis=-1, keepdims=True)
    h = xn * lax.rsqrt(ms + RMS_EPS) * n2_ref[...]
    h2_ref[...] = (h * (1.0 + sc_ref[0]) + sh_ref[0]).astype(BF16)


def _output_stage(ys, ua, yb, yc, x2, seq, d_skip, w_glu, b_glu, w_out, g1, n2g, sc2, sh2, tm=512):
    n, d = x2.shape
    tpb = seq // tm
    row = lambda i: (i, 0)
    const2 = lambda i: (0, 0)
    per_b = lambda i: (i // tpb, 0, 0)
    return pl.pallas_call(
        _out_kernel,
        grid=(n // tm,),
        in_specs=[pl.BlockSpec((tm, SSM_WIDTH), row),
                  pl.BlockSpec((tm, SSM_WIDTH), row),
                  pl.BlockSpec((tm, GMLP_WIDTH), row),
                  pl.BlockSpec((tm, ATT_WIDTH), row),
                  pl.BlockSpec((tm, d), row),
                  pl.BlockSpec((1, SSM_WIDTH), const2),
                  pl.BlockSpec((SSM_WIDTH, SSM_WIDTH), const2),
                  pl.BlockSpec((1, SSM_WIDTH), const2),
                  pl.BlockSpec((d, d), const2),
                  pl.BlockSpec((1, 1, d), per_b),
                  pl.BlockSpec((1, d), const2),
                  pl.BlockSpec((1, 1, d), per_b),
                  pl.BlockSpec((1, 1, d), per_b)],
        out_specs=[pl.BlockSpec((tm, d), row), pl.BlockSpec((tm, d), row)],
        out_shape=[jax.ShapeDtypeStruct((n, d), F32), jax.ShapeDtypeStruct((n, d), BF16)],
        compiler_params=_cparams("parallel"),
        name="out_proj_norm2",
    )(ys, ua, yb, yc, x2, d_skip, w_glu, b_glu, w_out, g1, n2g, sc2, sh2)


def _sort16_network():
    def merge(lo, hi, r):
        step = r * 2
        if step < hi - lo:
            yield from merge(lo, hi, step)
            yield from merge(lo + r, hi, step)
            yield from [(i, i + r) for i in range(lo + r, hi - r, step)]
        else:
            yield (lo, lo + r)

    def sort(lo, hi):
        if hi - lo >= 1:
            mid = lo + (hi - lo) // 2
            yield from sort(lo, mid)
            yield from sort(mid + 1, hi)
            yield from merge(lo, hi, 1)

    return tuple(sort(0, 15))


def _extract_top(x, n_top):
    tops = []
    for _ in range(n_top):
        mx = jnp.max(x, axis=0, keepdims=True)
        tops.append(mx)
        x = jnp.where(x == mx, -jnp.inf, x)
    return tops


def _extract_top_128(x, n_top):
    sub = x.shape[0] // 16
    rows = [x[i * sub:(i + 1) * sub, :] for i in range(16)]
    for a, b in _sort16_network():
        rows[a], rows[b] = jnp.maximum(rows[a], rows[b]), jnp.minimum(rows[a], rows[b])
    tops = []
    for k in range(n_top):
        mx = jnp.max(rows[0], axis=0, keepdims=True)
        tops.append(mx)
        need = min(16, n_top - 1 - k)
        if need:
            hit = rows[0] == mx
            below = rows[1:need + 1] + ([jnp.full_like(rows[0], -jnp.inf)] if need == 16 else [])
            for d in range(need):
                rows[d] = jnp.where(hit, below[d], rows[d])
    return tops


def _peer_kernel(h2_ref, x_ref, g2_ref, fg_ref, wq_ref, kbd_ref, *rest, final_norm, n_split):
    u_refs, vt_refs = rest[:n_split], rest[n_split:2 * n_split]
    o_ref, st_ref, e1_ref, e2_ref, r2_ref, acc_ref, at_ref, ga_ref, v1_ref, v2_ref = rest[2 * n_split:]
    j = pl.program_id(1)
    tm = h2_ref.shape[0]
    u_rows = u_refs[0].shape[0]
    v_rows = vt_refs[0].shape[0]
    te = u_rows * n_split
    n1 = te // PEER_N_KEYS
    nk = PEER_N_KEYS
    h2 = h2_ref[...]

    @pl.when(j == 0)
    def _():
        q = _dot(h2, wq_ref[...]).astype(BF16)
        st_ref[...] = _dot_nt(kbd_ref[...], q)

        def per_head(h, carry):
            base = pl.multiple_of(h * 2 * nk, 2 * nk)
            for lc in range(tm // LANES):
                ls = slice(lc * LANES, (lc + 1) * LANES)
                t1 = _extract_top_128(st_ref[pl.ds(base, nk), ls], PEER_TOPK + 1)
                t2 = _extract_top_128(st_ref[pl.ds(base + nk, nk), ls], PEER_TOPK + 1)
                for k in range(PEER_TOPK):
                    v1_ref[k:k + 1, :] = t1[k]
                    v2_ref[k:k + 1, :] = t2[k]
                v2_all = v2_ref[0:PEER_TOPK, :]
                v2_top = v2_ref[0:8, :]
                cand = [t1[0] + v2_all]
                cand += [t1[a] + v2_top for a in range(1, 8)]
                cand.append(v1_ref[8:PEER_TOPK, :] + t2[0])
                top = _extract_top(jnp.concatenate(cand, axis=0), PEER_TOPK + 1)
                z = jnp.zeros_like(top[0])
                for k in range(PEER_TOPK):
                    z = z + jnp.exp(top[k] - top[0])
                nxt = jnp.maximum(top[PEER_TOPK], jnp.maximum(t1[PEER_TOPK] + t2[0], t1[0] + t2[PEER_TOPK]))
                thr = 0.5 * (top[PEER_TOPK - 1] + nxt)
                s1 = st_ref[pl.ds(base, nk), ls]
                s2 = st_ref[pl.ds(base + nk, nk), ls]
                e1_ref[h, :, ls] = jnp.exp(s1 - t1[0]) * (0.5 / z)
                e2_ref[h, :, ls] = jnp.exp(s2 - t2[0]).astype(BF16)
                rank2 = jnp.zeros((nk, LANES), F32)
                for b in range(PEER_TOPK):
                    rank2 = rank2 + jnp.where(s2 < t2[b], 1.0, 0.0)
                r2_ref[h, :, ls] = rank2.astype(BF16)
                n_sel = jnp.zeros((nk, LANES), F32)
                for a in range(PEER_TOPK):
                    n_a = jnp.zeros_like(thr)
                    for b in range(PEER_TOPK // (a + 1)):
                        n_a = n_a + jnp.where(t1[a] + t2[b] >= thr, 1.0, 0.0)
                    n_sel = jnp.where(s1 == t1[a], n_a, n_sel)
                st_ref[pl.ds(base, nk), ls] = n_sel
            return carry

        lax.fori_loop(0, PEER_HEADS, per_head, 0)
        acc_ref[...] = jnp.zeros_like(acc_ref)
        ga_ref[1] = jnp.zeros(ga_ref.shape[1:], BF16)
        for k in range(n_split):
            at_ref[0, k * u_rows:(k + 1) * u_rows, :] = _dot_nt(u_refs[k][...], h2)

    n_blocks = pl.num_programs(1) - 2
    rd = j % 2
    wr = 1 - rd

    @pl.when((j >= 1) & (j <= n_blocks))
    def _():
        for k in range(n_split):
            at_ref[rd, k * u_rows:(k + 1) * u_rows, :] = _dot_nt(u_refs[k][...], h2)
        for k in range(n_split):
            acc_ref[k * v_rows:(k + 1) * v_rows, :] += _dot(vt_refs[k][...], ga_ref[rd])
        r0 = pl.multiple_of((j - 1) * n1, n1)
        for lc in range(tm // LANES):
            ls = slice(lc * LANES, (lc + 1) * LANES)
            for il in range(n1):
                g = jnp.zeros((nk, LANES), BF16)
                for h in range(PEER_HEADS):
                    n_row = st_ref[pl.ds(h * 2 * nk + r0, n1), ls][il:il + 1, :].astype(BF16)
                    e_row = e1_ref[h, pl.ds(r0, n1), ls][il:il + 1, :].astype(BF16)
                    picked = jnp.where(r2_ref[h, :, ls] < n_row, e2_ref[h, :, ls], jnp.zeros((), BF16))
                    g = g + picked * e_row
                a = at_ref[wr, il * nk:(il + 1) * nk, ls]
                inner = a * (0.7978845608028654 + 0.035677408136300125 * (a * a))
                act2 = a + a * jnp.tanh(inner)
                ga_ref[wr, il * nk:(il + 1) * nk, ls] = g * act2.astype(BF16)

    @pl.when(j == n_blocks + 1)
    def _():
        for k in range(n_split):
            acc_ref[k * v_rows:(k + 1) * v_rows, :] += _dot(vt_refs[k][...], ga_ref[rd])
        out = x_ref[...] + g2_ref[0] * acc_ref[...].T
        if final_norm:
            ms = jnp.mean(out * out, axis=-1, keepdims=True)
            out = out * lax.rsqrt(ms + RMS_EPS) * fg_ref[...]
        o_ref[...] = out


def _peer(h2, x2, seq, g2, fg, wq, kbd, u_bf, vt_bf, final_norm, tm=512, te=1024, n_split=4):
    n, d = x2.shape
    n_blk = u_bf.shape[0] // te
    tpb = seq // tm
    trow = lambda i, j: (i, 0)
    c2 = lambda i, j: (0, 0)
    hs = (PEER_HEADS, PEER_N_KEYS, tm)
    u_map = lambda k, i, j: (jnp.minimum(j, n_blk - 1) * n_split + k, 0)
    v_map = lambda k, i, j: (k, jnp.clip(j - 2, 0, n_blk - 1))
    return pl.pallas_call(
        functools.partial(_peer_kernel, final_norm=final_norm, n_split=n_split),
        grid=(n // tm, n_blk + 2),
        in_specs=[pl.BlockSpec((tm, d), trow),
                  pl.BlockSpec((tm, d), trow),
                  pl.BlockSpec((1, 1, d), lambda i, j: (i // tpb, 0, 0)),
                  pl.BlockSpec((1, d), c2),
                  pl.BlockSpec((d, PEER_HEADS * 2 * PEER_HALF), c2),
                  pl.BlockSpec((PEER_HEADS * 2 * PEER_N_KEYS, PEER_HEADS * 2 * PEER_HALF), c2)]
        + [pl.BlockSpec((te // n_split, d), functools.partial(u_map, k)) for k in range(n_split)]
        + [pl.BlockSpec((d // n_split, te), functools.partial(v_map, k)) for k in range(n_split)],
        out_specs=pl.BlockSpec((tm, d), trow),
        out_shape=jax.ShapeDtypeStruct((n, d), F32),
        scratch_shapes=[pltpu.VMEM((PEER_HEADS * 2 * PEER_N_KEYS, tm), F32),
                        pltpu.VMEM(hs, F32), pltpu.VMEM(hs, BF16), pltpu.VMEM(hs, BF16),
                        pltpu.VMEM((d, tm), F32),
                        pltpu.VMEM((2, te, tm), F32),
                        pltpu.VMEM((2, te, tm), BF16),
                        pltpu.VMEM((PEER_TOPK, LANES), F32), pltpu.VMEM((PEER_TOPK, LANES), F32)],
        compiler_params=_cparams("parallel", "arbitrary"),
        name="peer_dense",
    )(h2, x2, g2, fg, wq, kbd, *([u_bf] * n_split), *([vt_bf] * n_split))


def _peer_key_matrix(k1, k2):
    blocks = []
    for h in range(PEER_HEADS):
        for half, kk in enumerate((k1, k2)):
            col = (2 * h + half) * PEER_HALF
            blocks.append(jnp.pad(kk, ((0, 0), (col, PEER_HEADS * 2 * PEER_HALF - col - PEER_HALF))))
    return jnp.concatenate(blocks, axis=0).astype(BF16)


def kernel(x, c, norm1_g, norm2_g, w_mod, b_mod, w_in, ssm_a_re_log, ssm_a_im, ssm_b_re, ssm_b_im, ssm_c_re, ssm_c_im, ssm_d, ssm_log_dt, ssm_w_glu, ssm_b_glu, gmlp_w_sp, gmlp_b_sp, kv_norm_g, w_uk, w_uv, w_out, peer_w_q, peer_k1, peer_k2, peer_u, peer_v, final_g):
    bsz, seq, d = x.shape
    depth = w_mod.shape[0]
    x2 = x.reshape(bsz * seq, d)
    head_of = jnp.arange(GMLP_WIDTH) // GMLP_HEAD_DIM
    pavg = ((head_of[:, None] == head_of[None, :]).astype(F32) / GMLP_HEAD_DIM).astype(BF16)
    chunk_of = jnp.arange(GMLP_BLOCK) // CHUNK
    sp_mask = chunk_of[:, None] >= chunk_of[None, :]
    for l in range(depth):
        mod = _modulation(c, w_mod[l], b_mod[l])
        sh1, sc1, g1, sh2, sc2, g2 = [mod[:, i * d:(i + 1) * d].reshape(bsz, 1, d) for i in range(6)]
        w_in_pad = jnp.pad(w_in[l], ((0, 0), (0, IN_PAD - IN_WIDTH))).astype(BF16)
        wsp = jnp.where(sp_mask[None], gmlp_w_sp[l], 0.0).astype(BF16)
        bsp = jnp.repeat(gmlp_b_sp[l].T, GMLP_HEAD_DIM, axis=1)
        ua, yb, q, ckv, qi, ki, wi, ckvt = _input_stage(
            x2, seq, norm1_g[l].reshape(1, d), sc1, sh1, w_in_pad, kv_norm_g[l].reshape(1, KV_RANK),
            pavg, wsp, bsp)
        prep = _ssm_prep(ssm_a_re_log[l], ssm_a_im[l], ssm_b_re[l], ssm_b_im[l], ssm_c_re[l], ssm_c_im[l],
                         ssm_log_dt[l], SSM_T)
        ys = _ssm_scan(ua, bsz, seq, prep)
        yc = _dsa(q, qi, wi, ki, ckv, ckvt, w_uk[l], w_uv[l], bsz, seq)
        x2, h2 = _output_stage(
            ys, ua, yb, yc, x2, seq, ssm_d[l].reshape(1, SSM_WIDTH), ssm_w_glu[l].astype(BF16),
            ssm_b_glu[l].reshape(1, SSM_WIDTH), w_out[l].astype(BF16), g1, norm2_g[l].reshape(1, d), sc2, sh2)
        x2 = _peer(h2, x2, seq, g2, final_g.reshape(1, d), peer_w_q[l].astype(BF16),
                   _peer_key_matrix(peer_k1[l], peer_k2[l]), peer_u[l].astype(BF16),
                   peer_v[l].T.astype(BF16), final_norm=(l == depth - 1))
    return x2.reshape(bsz, seq, d)
```

```python
import functools
import math

import jax
import jax.numpy as jnp
from jax import lax
from jax.experimental import pallas as pl
from jax.experimental.pallas import tpu as pltpu

CHUNK = 64
RMS_EPS = 1e-6

SSM_GROUP = 16
SSM_STATE = 64
SSM_WIDTH = 256
SSM_GROUPS = SSM_WIDTH // SSM_GROUP
SSM_T = 16
GMLP_HEADS = 4
GMLP_HEAD_DIM = 64
GMLP_WIDTH = GMLP_HEADS * GMLP_HEAD_DIM
GMLP_BLOCK = 128
ATT_HEADS = 8
ATT_HEAD_DIM = 64
ATT_WIDTH = ATT_HEADS * ATT_HEAD_DIM
KV_RANK = 128
IDX_HEADS = 4
IDX_DIM = 64
TOPK_MAX = 256

IN_SIZES = (SSM_WIDTH, GMLP_WIDTH, GMLP_WIDTH, ATT_WIDTH, KV_RANK, IDX_HEADS * IDX_DIM, IDX_DIM, IDX_HEADS)
IN_WIDTH = sum(IN_SIZES)
IN_PAD = 1792
COL_UB, COL_VB, COL_Q, COL_CKV, COL_QI, COL_KW = 256, 512, 768, 1280, 1408, 1664

PEER_HEADS = 8
PEER_N_KEYS = 128
PEER_HALF = 64
PEER_TOPK = 16

LANES = 128
VMEM_LIMIT = 56 * 1024 * 1024
NEG = -1e30
INT_MIN = -(2 ** 31)

F32 = jnp.float32
BF16 = jnp.bfloat16


def _cparams(*sem):
    return pltpu.CompilerParams(dimension_semantics=sem, vmem_limit_bytes=VMEM_LIMIT)


def _dot(a, b):
    return jnp.dot(a, b, preferred_element_type=F32)


def _dot_nt(a, b):
    return lax.dot_general(a, b, (((1,), (1,)), ((), ())), preferred_element_type=F32)


def _mod_kernel(c_ref, w_ref, b_ref, o_ref):
    c = c_ref[...]
    ca = c * jax.nn.sigmoid(c)
    o_ref[...] = _dot(ca.astype(BF16), w_ref[...].astype(BF16)) + b_ref[...]


def _modulation(c, w_mod, b_mod):
    bsz, d = c.shape
    n6 = w_mod.shape[1]
    return pl.pallas_call(
        _mod_kernel,
        grid=(n6 // d,),
        in_specs=[pl.BlockSpec((bsz, d), lambda j: (0, 0)),
                  pl.BlockSpec((d, d), lambda j: (0, j)),
                  pl.BlockSpec((1, d), lambda j: (0, j))],
        out_specs=pl.BlockSpec((bsz, d), lambda j: (0, j)),
        out_shape=jax.ShapeDtypeStruct((bsz, n6), F32),
        compiler_params=_cparams("parallel"),
        name="modulation",
    )(c, w_mod, b_mod.reshape(1, n6))


def _in_kernel(x_ref, g_ref, sc_ref, sh_ref, w_ref, kvg_ref, pavg_ref, wsp_ref, bsp_ref,
               ua_ref, yb_ref, q_ref, ckv_ref, qi_ref, ki_ref, wi_ref, ckvt_ref):
    tm = x_ref.shape[0]
    x = x_ref[...]
    ms = jnp.mean(x * x, axis=-1, keepdims=True)
    h = x * lax.rsqrt(ms + RMS_EPS) * g_ref[...]
    h = h * (1.0 + sc_ref[0]) + sh_ref[0]
    proj = _dot(h.astype(BF16), w_ref[...])

    ua_ref[...] = proj[:, 0:COL_UB]
    q_ref[...] = proj[:, COL_Q:COL_CKV].astype(BF16)
    ckv = proj[:, COL_CKV:COL_QI]
    ckv_ms = jnp.mean(ckv * ckv, axis=-1, keepdims=True)
    ckv_n = ckv * lax.rsqrt(ckv_ms + RMS_EPS) * kvg_ref[...]
    ckv_ref[...] = ckv_n.astype(BF16)
    ckvt_ref[...] = ckv_n.T.astype(BF16)
    qi_ref[...] = proj[:, COL_QI:COL_KW].astype(BF16)
    kw = proj[:, COL_KW:IN_PAD]
    ki_ref[...] = kw[:, 0:IDX_DIM].astype(BF16)
    wi_ref[...] = kw

    u_b = proj[:, COL_UB:COL_VB]
    v_b = proj[:, COL_VB:COL_Q]
    pavg = pavg_ref[...]

    def head_mean(a):
        hi = a.astype(BF16)
        lo = (a - hi.astype(F32)).astype(BF16)
        return _dot(hi, pavg) + _dot(lo, pavg)

    mu = head_mean(v_b)
    dv = v_b - mu
    var = head_mean(dv * dv)
    vn = (dv * lax.rsqrt(var + RMS_EPS)).astype(BF16)
    lane_head = lax.broadcasted_iota(jnp.int32, (GMLP_BLOCK, GMLP_WIDTH), 1) // GMLP_HEAD_DIM
    for blk in range(tm // GMLP_BLOCK):
        rows = slice(blk * GMLP_BLOCK, (blk + 1) * GMLP_BLOCK)
        vblk = vn[rows, :]
        mixed = bsp_ref[...]
        for hh in range(GMLP_HEADS):
            mixed = mixed + jnp.where(lane_head == hh, _dot(wsp_ref[hh], vblk), 0.0)
        yb_ref[rows, :] = (u_b[rows, :] * mixed).astype(BF16)


def _input_stage(x2, seq, g, sc, sh, w_in_pad, kvg, pavg, wsp, bsp, tm=512):
    n, d = x2.shape
    tpb = seq // tm
    row = lambda i: (i, 0)
    const2 = lambda i: (0, 0)
    per_b = lambda i: (i // tpb, 0, 0)
    outs = [(SSM_WIDTH, F32), (GMLP_WIDTH, BF16), (ATT_WIDTH, BF16), (KV_RANK, BF16),
            (IDX_HEADS * IDX_DIM, BF16), (IDX_DIM, BF16), (LANES, F32)]
    return pl.pallas_call(
        _in_kernel,
        grid=(n // tm,),
        in_specs=[pl.BlockSpec((tm, d), row),
                  pl.BlockSpec((1, d), const2),
                  pl.BlockSpec((1, 1, d), per_b),
                  pl.BlockSpec((1, 1, d), per_b),
                  pl.BlockSpec((d, IN_PAD), const2),
                  pl.BlockSpec((1, KV_RANK), const2),
                  pl.BlockSpec((GMLP_WIDTH, GMLP_WIDTH), const2),
                  pl.BlockSpec((GMLP_HEADS, GMLP_BLOCK, GMLP_BLOCK), lambda i: (0, 0, 0)),
                  pl.BlockSpec((GMLP_BLOCK, GMLP_WIDTH), const2)],
        out_specs=[pl.BlockSpec((tm, w), row) for w, _ in outs] + [pl.BlockSpec((KV_RANK, tm), lambda i: (0, i))],
        out_shape=[jax.ShapeDtypeStruct((n, w), dt) for w, dt in outs] + [jax.ShapeDtypeStruct((KV_RANK, n), BF16)],
        compiler_params=_cparams("parallel"),
        name="input_proj_gmlp",
    )(x2, g, sc, sh, w_in_pad, kvg, pavg, wsp, bsp)


def _ssm_prep_tm(a_re_log, a_im, b_re, b_im, c_re, c_im, log_dt, t_len):
    hp = lax.Precision.HIGHEST
    g_n, h_n, p_n = SSM_GROUPS, SSM_GROUP, SSM_STATE
    lam_re = -jnp.exp(a_re_log)
    lam_im = a_im
    dt = jnp.exp(log_dt)[:, None]
    mag = jnp.exp(lam_re * dt)
    abar_re = mag * jnp.cos(lam_im * dt)
    abar_im = mag * jnp.sin(lam_im * dt)
    den = lam_re * lam_re + lam_im * lam_im
    p = abar_re - 1.0
    qq = abar_im
    f_re = (p * lam_re + qq * lam_im) / den
    f_im = (qq * lam_re - p * lam_im) / den
    bb_re = f_re[..., None] * b_re - f_im[..., None] * b_im
    bb_im = f_re[..., None] * b_im + f_im[..., None] * b_re
    k = jnp.arange(t_len + 1, dtype=F32)[:, None, None]
    pmag = jnp.exp(k * (lam_re * dt))
    ang = k * (lam_im * dt)
    pw_re = pmag * jnp.cos(ang)
    pw_im = pmag * jnp.sin(ang)
    cp_re = c_re[None] * pw_re[:, :, None, :] - c_im[None] * pw_im[:, :, None, :]
    cp_im = c_re[None] * pw_im[:, :, None, :] + c_im[None] * pw_re[:, :, None, :]
    kern = (jnp.einsum('kghp,gpj->kghj', cp_re[:t_len], bb_re, precision=hp)
            - jnp.einsum('kghp,gpj->kghj', cp_im[:t_len], bb_im, precision=hp))
    grp_w = jnp.arange(g_n * h_n) // h_n
    grp_s = jnp.arange(g_n * p_n) // p_n
    kb = jnp.tile(kern.transpose(0, 1, 3, 2).reshape(t_len, g_n * h_n, h_n), (1, 1, g_n))
    kb = jnp.where((grp_w[:, None] == grp_w[None, :])[None], kb, 0.0)
    in_mask = grp_w[:, None] == grp_s[None, :]
    b_in = jnp.concatenate(
        [jnp.where(in_mask, jnp.tile(m.transpose(0, 2, 1).reshape(g_n * h_n, p_n), (1, g_n)), 0.0)
         for m in (bb_re, bb_im)], axis=1)
    c_out = jnp.concatenate(
        [jnp.where(in_mask.T, jnp.tile(m.transpose(0, 2, 1).reshape(g_n * p_n, h_n), (1, g_n)), 0.0)
         for m in (c_re, -c_im)], axis=0)
    tt = jnp.arange(t_len)
    flat = lambda a: a.reshape(a.shape[0], g_n * p_n)
    pw_end = jnp.concatenate([flat(pw_re[t_len - 1 - tt]), flat(pw_im[t_len - 1 - tt])], axis=1)
    pw_nxt = jnp.concatenate([flat(pw_re[1 + tt]), flat(pw_im[1 + tt])], axis=1)
    a_t = jnp.concatenate([flat(pw_re[t_len:]), flat(pw_im[t_len:])], axis=1)
    return kb.astype(BF16), b_in.astype(BF16), c_out.astype(BF16), pw_end, pw_nxt, a_t


def _ssm_tm_kernel(u_ref, kb_ref, bin_ref, cout_ref, pwe_ref, pwn_ref, at_ref, d_ref, y_ref,
                   ubuf_ref, sp_ref, st_ref, *, t_len):
    tm, width = u_ref.shape
    n_st = at_ref.shape[1] // 2
    nch = tm // t_len

    @pl.when(pl.program_id(1) == 0)
    def _():
        st_ref[...] = jnp.zeros_like(st_ref)

    u = u_ref[...]
    ubuf_ref[0:t_len, :] = jnp.zeros((t_len, width), F32)
    ubuf_ref[t_len:, :] = u
    pos = lax.broadcasted_iota(jnp.int32, (tm, 1), 0) % t_len
    y = _dot(u.astype(BF16), kb_ref[0])
    for k in range(1, t_len):
        shifted = ubuf_ref[t_len - k:t_len - k + tm, :]
        y = y + _dot(jnp.where(pos >= k, shifted, 0.0).astype(BF16), kb_ref[k])

    bu = _dot(u.astype(BF16), bin_ref[...]).reshape(nch, t_len, 2 * n_st)
    b_re, b_im = bu[:, :, :n_st], bu[:, :, n_st:]
    w_re, w_im = pwe_ref[:, :n_st][None], pwe_ref[:, n_st:][None]
    loc_re = jnp.sum(b_re * w_re - b_im * w_im, axis=1)
    loc_im = jnp.sum(b_re * w_im + b_im * w_re, axis=1)

    a_re, a_im = at_ref[:, :n_st], at_ref[:, n_st:]
    s_re, s_im = st_ref[:, :n_st], st_ref[:, n_st:]
    for c in range(nch):
        sp_ref[c:c + 1, :n_st] = s_re
        sp_ref[c:c + 1, n_st:] = s_im
        s_re, s_im = (a_re * s_re - a_im * s_im + loc_re[c:c + 1, :],
                      a_re * s_im + a_im * s_re + loc_im[c:c + 1, :])
    st_ref[:, :n_st] = s_re
    st_ref[:, n_st:] = s_im

    p_re, p_im = sp_ref[:, :n_st][:, None, :], sp_ref[:, n_st:][:, None, :]
    n_re, n_im = pwn_ref[:, :n_st][None], pwn_ref[:, n_st:][None]
    e_re = (p_re * n_re - p_im * n_im).reshape(tm, n_st).astype(BF16)
    e_im = (p_re * n_im + p_im * n_re).reshape(tm, n_st).astype(BF16)
    y = y + _dot(e_re, cout_ref[:n_st, :]) + _dot(e_im, cout_ref[n_st:, :])
    y_ref[...] = y + d_ref[...] * u


def _ssm_scan_tm(u_a, bsz, seq, prep, d_skip, tm=512):
    kb, b_in, c_out, pw_end, pw_nxt, a_t = prep
    t_len = kb.shape[0]
    n, width = u_a.shape
    tpb = seq // tm
    row = lambda b, i: (b * tpb + i, 0)
    c2 = lambda b, i: (0, 0)
    n2 = b_in.shape[1]
    return pl.pallas_call(
        functools.partial(_ssm_tm_kernel, t_len=t_len),
        grid=(bsz, tpb),
        in_specs=[pl.BlockSpec((tm, width), row),
                  pl.BlockSpec((t_len, width, width), lambda b, i: (0, 0, 0)),
                  pl.BlockSpec((width, n2), c2),
                  pl.BlockSpec((n2, width), c2),
                  pl.BlockSpec((t_len, n2), c2),
                  pl.BlockSpec((t_len, n2), c2),
                  pl.BlockSpec((1, n2), c2),
                  pl.BlockSpec((1, width), c2)],
        out_specs=pl.BlockSpec((tm, width), row),
        out_shape=jax.ShapeDtypeStruct((n, width), F32),
        scratch_shapes=[pltpu.VMEM((t_len + tm, width), F32),
                        pltpu.VMEM((tm // t_len, n2), F32),
                        pltpu.VMEM((1, n2), F32)],
        compiler_params=_cparams("parallel", "arbitrary"),
        name="ssm_scan",
    )(u_a, kb, b_in, c_out, pw_end, pw_nxt, a_t, d_skip)


def _dsa_kernel(q_ref, qi_ref, wi_ref, ki_ref, ckv_ref, ckvt_ref, wuk_ref, wuvt_ref, tri_ref, o_ref,
                key_ref, dm_ref, qa_ref, ot_ref, *acc_refs, topk, kb_len):
    tq = q_ref.shape[0]
    t0 = pl.program_id(1) * tq
    n_kb = (t0 + tq + kb_len - 1) // kb_len
    q_pos = t0 + lax.broadcasted_iota(jnp.int32, (1, tq), 1)
    q_chunk = q_pos // CHUNK
    k_off = lax.broadcasted_iota(jnp.int32, (kb_len, 1), 0)

    def fold8(m, op):
        r = m.reshape(kb_len // 8, 8, tq)
        n = kb_len // 8
        while n > 1:
            n //= 2
            r = op(r[:n], r[n:2 * n])
        return r[0]

    def col_sum(m):
        return fold8(m, jnp.add)

    wi_t = wi_ref[...].T
    qi_h = [qi_ref[:, h * IDX_DIM:(h + 1) * IDX_DIM] for h in range(IDX_HEADS)]
    w_h = [wi_t[IDX_DIM + h:IDX_DIM + h + 1, :] * ((IDX_HEADS ** -0.5) * (IDX_DIM ** -0.5))
           for h in range(IDX_HEADS)]

    def score_block(kb, carry):
        k0 = pl.multiple_of(kb * kb_len, kb_len)
        ki = ki_ref[pl.ds(k0, kb_len), :]
        score = jnp.zeros((kb_len, tq), F32)
        for h in range(IDX_HEADS):
            score = score + jnp.maximum(_dot_nt(ki, qi_h[h]), 0.0) * w_h[h]
        score = jnp.where(score == 0.0, 0.0, score)
        bits = lax.bitcast_convert_type(score, jnp.int32)
        key = jnp.where(bits < 0, bits ^ jnp.int32(0x7FFFFFFF), bits)
        adm = ((k0 + k_off) // CHUNK) <= q_chunk
        key_ref[pl.ds(k0, kb_len), :] = jnp.where(adm, key, jnp.int32(INT_MIN))
        return carry

    lax.fori_loop(0, n_kb, score_block, 0)

    def count(pred):
        def body(kb, acc):
            k0 = pl.multiple_of(kb * kb_len, kb_len)
            return acc + col_sum(pred(key_ref[pl.ds(k0, kb_len), :]).astype(F32))
        acc = lax.fori_loop(0, n_kb, body, jnp.zeros((8, tq), F32))
        return jnp.sum(acc, axis=0, keepdims=True)

    kf = jnp.float32(topk)

    def bit_step(step, ans):
        trial = ans + jnp.left_shift(jnp.int32(1), 31 - step)
        return jnp.where(count(lambda kblk: kblk >= trial) >= kf, trial, ans)

    ans = lax.fori_loop(0, 32, bit_step, jnp.full((1, tq), INT_MIN, jnp.int32))

    room = kf - count(lambda kblk: kblk > ans)
    tri = tri_ref[...]

    def select_block(kb, carry):
        k0 = pl.multiple_of(kb * kb_len, kb_len)
        kblk = key_ref[pl.ds(k0, kb_len), :]
        eq = kblk == ans
        pre = _dot(tri, eq.astype(BF16)) + carry
        sel = (kblk > ans) | (eq & (pre <= room))
        k_pos = k0 + k_off
        adm = (k_pos // CHUNK) <= q_chunk
        dist = jnp.abs(q_pos - k_pos).astype(F32)
        dm_ref[pl.ds(k0, kb_len), :] = jnp.where(sel & adm, dist, -NEG)
        return carry + jnp.sum(col_sum(eq.astype(F32)), axis=0, keepdims=True)

    lax.fori_loop(0, n_kb, select_block, jnp.zeros((1, tq), F32))

    for h in range(ATT_HEADS):
        hs = slice(h * ATT_HEAD_DIM, (h + 1) * ATT_HEAD_DIM)
        qa_ref[h] = (_dot(q_ref[:, hs], wuk_ref[h]) * (ATT_HEAD_DIM ** -0.5)).astype(BF16)
        acc_refs[h][...] = jnp.zeros((KV_RANK, tq), F32)

    def att_block(kb, carry):
        k0 = pl.multiple_of(kb * kb_len, kb_len)
        ckv = ckv_ref[pl.ds(k0, kb_len), :]
        ckvt = ckvt_ref[:, pl.ds(k0, kb_len)]
        dm = dm_ref[pl.ds(k0, kb_len), :]
        new = []
        qk = _dot_nt(ckv, qa_ref[0])
        for h in range(ATT_HEADS):
            m_run, l_run = carry[2 * h], carry[2 * h + 1]
            slope = 2.0 ** (-8.0 * (h + 1) / ATT_HEADS)
            lg = qk - slope * dm
            if h + 1 < ATT_HEADS:
                qk = _dot_nt(ckv, qa_ref[h + 1])
            m_new = jnp.maximum(m_run, jnp.max(fold8(lg, jnp.maximum), axis=0, keepdims=True))
            alpha = jnp.exp(m_run - m_new)
            p = jnp.exp(lg - m_new)
            new += [m_new, alpha * l_run + jnp.sum(col_sum(p), axis=0, keepdims=True)]
            acc_refs[h][...] = alpha * acc_refs[h][...] + _dot(ckvt, p.astype(BF16))
        return tuple(new)

    init = tuple(jnp.full((1, tq), -jnp.inf if i % 2 == 0 else 0.0, F32) for i in range(2 * ATT_HEADS))
    fin = lax.fori_loop(0, n_kb, att_block, init)
    for h in range(ATT_HEADS):
        o_t = (acc_refs[h][...] / fin[2 * h + 1]).astype(BF16)
        ot_ref[h * ATT_HEAD_DIM:(h + 1) * ATT_HEAD_DIM, :] = _dot(wuvt_ref[h], o_t)
    o_ref[...] = ot_ref[...].T.astype(o_ref.dtype)


def _dsa(q, qi, wi, ki, ckv, ckvt, w_uk, w_uv, bsz, seq, tq=256, kb_len=512):
    n = q.shape[0]
    nqt = seq // tq
    kb_len = min(kb_len, seq)
    topk = min(TOPK_MAX, seq // 4)
    tri = (jnp.arange(kb_len)[:, None] >= jnp.arange(kb_len)[None, :]).astype(BF16)
    qrow = lambda b, i: (b * nqt + i, 0)
    krow = lambda b, i: (b, 0)
    c3 = lambda b, i: (0, 0, 0)
    return pl.pallas_call(
        functools.partial(_dsa_kernel, topk=topk, kb_len=kb_len),
        grid=(bsz, nqt),
        in_specs=[pl.BlockSpec((tq, ATT_WIDTH), qrow),
                  pl.BlockSpec((tq, IDX_HEADS * IDX_DIM), qrow),
                  pl.BlockSpec((tq, LANES), qrow),
                  pl.BlockSpec((seq, IDX_DIM), krow),
                  pl.BlockSpec((seq, KV_RANK), krow),
                  pl.BlockSpec((KV_RANK, seq), lambda b, i: (0, b)),
                  pl.BlockSpec((ATT_HEADS, ATT_HEAD_DIM, KV_RANK), c3),
                  pl.BlockSpec((ATT_HEADS, ATT_HEAD_DIM, KV_RANK), c3),
                  pl.BlockSpec((kb_len, kb_len), lambda b, i: (0, 0))],
        out_specs=pl.BlockSpec((tq, ATT_WIDTH), qrow),
        out_shape=jax.ShapeDtypeStruct((n, ATT_WIDTH), BF16),
        scratch_shapes=[pltpu.VMEM((seq, tq), jnp.int32),
                        pltpu.VMEM((seq, tq), F32),
                        pltpu.VMEM((ATT_HEADS, tq, KV_RANK), BF16),
                        pltpu.VMEM((ATT_WIDTH, tq), F32)]
        + [pltpu.VMEM((KV_RANK, tq), F32) for _ in range(ATT_HEADS)],
        compiler_params=_cparams("parallel", "parallel"),
        name="dsa_attention",
    )(q, qi, wi, ki, ckv, ckvt, w_uk.astype(BF16), jnp.swapaxes(w_uv, 1, 2).astype(BF16), tri)


def _out_kernel(ys_ref, yb_ref, yc_ref, x_ref, wglu_ref, bglu_ref, wo_ref, g1_ref,
                n2_ref, sc_ref, sh_ref, xo_ref, h2_ref):
    yg = jax.nn.gelu(ys_ref[...])
    z = _dot(yg.astype(BF16), wglu_ref[...]) + bglu_ref[...]
    ya = yg * jax.nn.sigmoid(z)
    a_w, b_w = SSM_WIDTH, SSM_WIDTH + GMLP_WIDTH
    mix = (_dot(ya.astype(BF16), wo_ref[0:a_w, :])
           + _dot(yb_ref[...], wo_ref[a_w:b_w, :])
           + _dot(yc_ref[...], wo_ref[b_w:, :]))
    xn = x_ref[...] + g1_ref[0] * mix
    xo_ref[...] = xn
    ms = jnp.mean(xn * xn, axis=-1, keepdims=True)
    h = xn * lax.rsqrt(ms + RMS_EPS) * n2_ref[...]
    h2_ref[...] = (h * (1.0 + sc_ref[0]) + sh_ref[0]).astype(BF16)


def _output_stage(ys, yb, yc, x2, seq, w_glu, b_glu, w_out, g1, n2g, sc2, sh2, tm=512):
    n, d = x2.shape
    tpb = seq // tm
    row = lambda i: (i, 0)
    const2 = lambda i: (0, 0)
    per_b = lambda i: (i // tpb, 0, 0)
    return pl.pallas_call(
        _out_kernel,
        grid=(n // tm,),
        in_specs=[pl.BlockSpec((tm, SSM_WIDTH), row),
                  pl.BlockSpec((tm, GMLP_WIDTH), row),
                  pl.BlockSpec((tm, ATT_WIDTH), row),
                  pl.BlockSpec((tm, d), row),
                  pl.BlockSpec((SSM_WIDTH, SSM_WIDTH), const2),
                  pl.BlockSpec((1, SSM_WIDTH), const2),
                  pl.BlockSpec((d, d), const2),
                  pl.BlockSpec((1, 1, d), per_b),
                  pl.BlockSpec((1, d), const2),
                  pl.BlockSpec((1, 1, d), per_b),
                  pl.BlockSpec((1, 1, d), per_b)],
        out_specs=[pl.BlockSpec((tm, d), row), pl.BlockSpec((tm, d), row)],
        out_shape=[jax.ShapeDtypeStruct((n, d), F32), jax.ShapeDtypeStruct((n, d), BF16)],
        compiler_params=_cparams("parallel"),
        name="out_proj_norm2",
    )(ys, yb, yc, x2, w_glu, b_glu, w_out, g1, n2g, sc2, sh2)


def _sort16_network():
    def merge(lo, hi, r):
        step = r * 2
        if step < hi - lo:
            yield from merge(lo, hi, step)
            yield from merge(lo + r, hi, step)
            yield from [(i, i + r) for i in range(lo + r, hi - r, step)]
        else:
            yield (lo, lo + r)

    def sort(lo, hi):
        if hi - lo >= 1:
            mid = lo + (hi - lo) // 2
            yield from sort(lo, mid)
            yield from sort(mid + 1, hi)
            yield from merge(lo, hi, 1)

    return tuple(sort(0, 15))


def _extract_top(x, n_top):
    tops = []
    for _ in range(n_top):
        mx = jnp.max(x, axis=0, keepdims=True)
        tops.append(mx)
        x = jnp.where(x == mx, -jnp.inf, x)
    return tops


def _extract_top_128(x, n_top):
    sub = x.shape[0] // 16
    rows = [x[i * sub:(i + 1) * sub, :] for i in range(16)]
    for a, b in _sort16_network():
        rows[a], rows[b] = jnp.maximum(rows[a], rows[b]), jnp.minimum(rows[a], rows[b])
    tops = []
    for k in range(n_top):
        mx = jnp.max(rows[0], axis=0, keepdims=True)
        tops.append(mx)
        need = min(16, n_top - 1 - k)
        if need:
            hit = rows[0] == mx
            below = rows[1:need + 1] + ([jnp.full_like(rows[0], -jnp.inf)] if need == 16 else [])
            for d in range(need):
                rows[d] = jnp.where(hit, below[d], rows[d])
    return tops


def _peer_kernel(h2_ref, x_ref, g2_ref, fg_ref, wq_ref, kbd_ref, *rest, final_norm, n_split):
    u_refs, vt_refs = rest[:n_split], rest[n_split:2 * n_split]
    o_ref, st_ref, e1_ref, e2_ref, r2_ref, acc_ref, at_ref, ga_ref, v1_ref, v2_ref = rest[2 * n_split:]
    j = pl.program_id(1)
    tm = h2_ref.shape[0]
    u_rows = u_refs[0].shape[0]
    v_rows = vt_refs[0].shape[0]
    te = u_rows * n_split
    n1 = te // PEER_N_KEYS
    nk = PEER_N_KEYS
    h2 = h2_ref[...]

    @pl.when(j == 0)
    def _():
        q = _dot(h2, wq_ref[...]).astype(BF16)
        st_ref[...] = _dot_nt(kbd_ref[...], q)

        def per_head(h, carry):
            base = pl.multiple_of(h * 2 * nk, 2 * nk)
            for lc in range(tm // LANES):
                ls = slice(lc * LANES, (lc + 1) * LANES)
                t1 = _extract_top_128(st_ref[pl.ds(base, nk), ls], PEER_TOPK + 1)
                t2 = _extract_top_128(st_ref[pl.ds(base + nk, nk), ls], PEER_TOPK + 1)
                for k in range(PEER_TOPK):
                    v1_ref[k:k + 1, :] = t1[k]
                    v2_ref[k:k + 1, :] = t2[k]
                v2_all = v2_ref[0:PEER_TOPK, :]
                v2_top = v2_ref[0:8, :]
                cand = [t1[0] + v2_all]
                cand += [t1[a] + v2_top for a in range(1, 8)]
                cand.append(v1_ref[8:PEER_TOPK, :] + t2[0])
                top = _extract_top(jnp.concatenate(cand, axis=0), PEER_TOPK + 1)
                z = jnp.zeros_like(top[0])
                for k in range(PEER_TOPK):
                    z = z + jnp.exp(top[k] - top[0])
                nxt = jnp.maximum(top[PEER_TOPK], jnp.maximum(t1[PEER_TOPK] + t2[0], t1[0] + t2[PEER_TOPK]))
                thr = 0.5 * (top[PEER_TOPK - 1] + nxt)
                s1 = st_ref[pl.ds(base, nk), ls]
                s2 = st_ref[pl.ds(base + nk, nk), ls]
                e1_ref[h, :, ls] = jnp.exp(s1 - t1[0]) * (0.5 / z)
                e2_ref[h, :, ls] = jnp.exp(s2 - t2[0]).astype(BF16)
                rank2 = jnp.zeros((nk, LANES), F32)
                for b in range(PEER_TOPK):
                    rank2 = rank2 + jnp.where(s2 < t2[b], 1.0, 0.0)
                r2_ref[h, :, ls] = rank2.astype(BF16)
                n_sel = jnp.zeros((nk, LANES), F32)
                for a in range(PEER_TOPK):
                    n_a = jnp.zeros_like(thr)
                    for b in range(PEER_TOPK // (a + 1)):
                        n_a = n_a + jnp.where(t1[a] + t2[b] >= thr, 1.0, 0.0)
                    n_sel = jnp.where(s1 == t1[a], n_a, n_sel)
                st_ref[pl.ds(base, nk), ls] = n_sel
            return carry

        lax.fori_loop(0, PEER_HEADS, per_head, 0)
        acc_ref[...] = jnp.zeros_like(acc_ref)
        ga_ref[1] = jnp.zeros(ga_ref.shape[1:], BF16)
        for k in range(n_split):
            at_ref[0, k * u_rows:(k + 1) * u_rows, :] = _dot_nt(u_refs[k][...], h2)

    n_blocks = pl.num_programs(1) - 2
    rd = j % 2
    wr = 1 - rd

    @pl.when((j >= 1) & (j <= n_blocks))
    def _():
        for k in range(n_split):
            at_ref[rd, k * u_rows:(k + 1) * u_rows, :] = _dot_nt(u_refs[k][...], h2)
        for k in range(n_split):
            acc_ref[k * v_rows:(k + 1) * v_rows, :] += _dot(vt_refs[k][...], ga_ref[rd])
        r0 = pl.multiple_of((j - 1) * n1, n1)
        for lc in range(tm // LANES):
            ls = slice(lc * LANES, (lc + 1) * LANES)
            for il in range(n1):
                g = jnp.zeros((nk, LANES), BF16)
                for h in range(PEER_HEADS):
                    n_row = st_ref[pl.ds(h * 2 * nk + r0, n1), ls][il:il + 1, :].astype(BF16)
                    e_row = e1_ref[h, pl.ds(r0, n1), ls][il:il + 1, :].astype(BF16)
                    picked = jnp.where(r2_ref[h, :, ls] < n_row, e2_ref[h, :, ls], jnp.zeros((), BF16))
                    g = g + picked * e_row
                a = at_ref[wr, il * nk:(il + 1) * nk, ls]
                inner = a * (0.7978845608028654 + 0.035677408136300125 * (a * a))
                act2 = a + a * jnp.tanh(inner)
                ga_ref[wr, il * nk:(il + 1) * nk, ls] = g * act2.astype(BF16)

    @pl.when(j == n_blocks + 1)
    def _():
        for k in range(n_split):
            acc_ref[k * v_rows:(k + 1) * v_rows, :] += _dot(vt_refs[k][...], ga_ref[rd])
        out = x_ref[...] + g2_ref[0] * acc_ref[...].T
        if final_norm:
            ms = jnp.mean(out * out, axis=-1, keepdims=True)
            out = out * lax.rsqrt(ms + RMS_EPS) * fg_ref[...]
        o_ref[...] = out


def _peer(h2, x2, seq, g2, fg, wq, kbd, u_bf, vt_bf, final_norm, tm=512, te=1024, n_split=4):
    n, d = x2.shape
    n_blk = u_bf.shape[0] // te
    tpb = seq // tm
    trow = lambda i, j: (i, 0)
    c2 = lambda i, j: (0, 0)
    hs = (PEER_HEADS, PEER_N_KEYS, tm)
    u_map = lambda k, i, j: (jnp.minimum(j, n_blk - 1) * n_split + k, 0)
    v_map = lambda k, i, j: (k, jnp.clip(j - 2, 0, n_blk - 1))
    return pl.pallas_call(
        functools.partial(_peer_kernel, final_norm=final_norm, n_split=n_split),
        grid=(n // tm, n_blk + 2),
        in_specs=[pl.BlockSpec((tm, d), trow),
                  pl.BlockSpec((tm, d), trow),
                  pl.BlockSpec((1, 1, d), lambda i, j: (i // tpb, 0, 0)),
                  pl.BlockSpec((1, d), c2),
                  pl.BlockSpec((d, PEER_HEADS * 2 * PEER_HALF), c2),
                  pl.BlockSpec((PEER_HEADS * 2 * PEER_N_KEYS, PEER_HEADS * 2 * PEER_HALF), c2)]
        + [pl.BlockSpec((te // n_split, d), functools.partial(u_map, k)) for k in range(n_split)]
        + [pl.BlockSpec((d // n_split, te), functools.partial(v_map, k)) for k in range(n_split)],
        out_specs=pl.BlockSpec((tm, d), trow),
        out_shape=jax.ShapeDtypeStruct((n, d), F32),
        scratch_shapes=[pltpu.VMEM((PEER_HEADS * 2 * PEER_N_KEYS, tm), F32),
                        pltpu.VMEM(hs, F32), pltpu.VMEM(hs, BF16), pltpu.VMEM(hs, BF16),
                        pltpu.VMEM((d, tm), F32),
                        pltpu.VMEM((2, te, tm), F32),
                        pltpu.VMEM((2, te, tm), BF16),
                        pltpu.VMEM((PEER_TOPK, LANES), F32), pltpu.VMEM((PEER_TOPK, LANES), F32)],
        compiler_params=_cparams("parallel", "arbitrary"),
        name="peer_dense",
    )(h2, x2, g2, fg, wq, kbd, *([u_bf] * n_split), *([vt_bf] * n_split))


def _peer_key_matrix(k1, k2):
    blocks = []
    for h in range(PEER_HEADS):
        for half, kk in enumerate((k1, k2)):
            col = (2 * h + half) * PEER_HALF
            blocks.append(jnp.pad(kk, ((0, 0), (col, PEER_HEADS * 2 * PEER_HALF - col - PEER_HALF))))
    return jnp.concatenate(blocks, axis=0).astype(BF16)


def kernel(x, c, norm1_g, norm2_g, w_mod, b_mod, w_in, ssm_a_re_log, ssm_a_im, ssm_b_re, ssm_b_im, ssm_c_re, ssm_c_im, ssm_d, ssm_log_dt, ssm_w_glu, ssm_b_glu, gmlp_w_sp, gmlp_b_sp, kv_norm_g, w_uk, w_uv, w_out, peer_w_q, peer_k1, peer_k2, peer_u, peer_v, final_g):
    bsz, seq, d = x.shape
    depth = w_mod.shape[0]
    x2 = x.reshape(bsz * seq, d)
    head_of = jnp.arange(GMLP_WIDTH) // GMLP_HEAD_DIM
    pavg = ((head_of[:, None] == head_of[None, :]).astype(F32) / GMLP_HEAD_DIM).astype(BF16)
    chunk_of = jnp.arange(GMLP_BLOCK) // CHUNK
    sp_mask = chunk_of[:, None] >= chunk_of[None, :]
    for l in range(depth):
        mod = _modulation(c, w_mod[l], b_mod[l])
        sh1, sc1, g1, sh2, sc2, g2 = [mod[:, i * d:(i + 1) * d].reshape(bsz, 1, d) for i in range(6)]
        w_in_pad = jnp.pad(w_in[l], ((0, 0), (0, IN_PAD - IN_WIDTH))).astype(BF16)
        wsp = jnp.where(sp_mask[None], gmlp_w_sp[l], 0.0).astype(BF16)
        bsp = jnp.repeat(gmlp_b_sp[l].T, GMLP_HEAD_DIM, axis=1)
        ua, yb, q, ckv, qi, ki, wi, ckvt = _input_stage(
            x2, seq, norm1_g[l].reshape(1, d), sc1, sh1, w_in_pad, kv_norm_g[l].reshape(1, KV_RANK),
            pavg, wsp, bsp)
        prep = _ssm_prep_tm(ssm_a_re_log[l], ssm_a_im[l], ssm_b_re[l], ssm_b_im[l], ssm_c_re[l], ssm_c_im[l],
                            ssm_log_dt[l], SSM_T)
        ys = _ssm_scan_tm(ua, bsz, seq, prep, ssm_d[l].reshape(1, SSM_WIDTH))
        yc = _dsa(q, qi, wi, ki, ckv, ckvt, w_uk[l], w_uv[l], bsz, seq)
        x2, h2 = _output_stage(
            ys, yb, yc, x2, seq, ssm_w_glu[l].astype(BF16),
            ssm_b_glu[l].reshape(1, SSM_WIDTH), w_out[l].astype(BF16), g1, norm2_g[l].reshape(1, d), sc2, sh2)
        x2 = _peer(h2, x2, seq, g2, final_g.reshape(1, d), peer_w_q[l].astype(BF16),
                   _peer_key_matrix(peer_k1[l], peer_k2[l]), peer_u[l].astype(BF16),
                   peer_v[l].T.astype(BF16), final_norm=(l == depth - 1))
    return x2.reshape(bsz, seq, d)
```

```python
import functools
import math

import jax
import jax.numpy as jnp
from jax import lax
from jax.experimental import pallas as pl
from jax.experimental.pallas import tpu as pltpu

CHUNK = 64
RMS_EPS = 1e-6

SSM_GROUP = 16
SSM_STATE = 64
SSM_WIDTH = 256
SSM_GROUPS = SSM_WIDTH // SSM_GROUP
SSM_T = 16
GMLP_HEADS = 4
GMLP_HEAD_DIM = 64
GMLP_WIDTH = GMLP_HEADS * GMLP_HEAD_DIM
GMLP_BLOCK = 128
ATT_HEADS = 8
ATT_HEAD_DIM = 64
ATT_WIDTH = ATT_HEADS * ATT_HEAD_DIM
KV_RANK = 128
IDX_HEADS = 4
IDX_DIM = 64
TOPK_MAX = 256

IN_SIZES = (SSM_WIDTH, GMLP_WIDTH, GMLP_WIDTH, ATT_WIDTH, KV_RANK, IDX_HEADS * IDX_DIM, IDX_DIM, IDX_HEADS)
IN_WIDTH = sum(IN_SIZES)
IN_PAD = 1792
COL_UB, COL_VB, COL_Q, COL_CKV, COL_QI, COL_KW = 256, 512, 768, 1280, 1408, 1664

PEER_HEADS = 8
PEER_N_KEYS = 128
PEER_HALF = 64
PEER_TOPK = 16

LANES = 128
VMEM_LIMIT = 56 * 1024 * 1024
NEG = -1e30
INT_MIN = -(2 ** 31)

F32 = jnp.float32
BF16 = jnp.bfloat16


def _cparams(*sem):
    return pltpu.CompilerParams(dimension_semantics=sem, vmem_limit_bytes=VMEM_LIMIT)


def _dot(a, b):
    return jnp.dot(a, b, preferred_element_type=F32)


def _dot_nt(a, b):
    return lax.dot_general(a, b, (((1,), (1,)), ((), ())), preferred_element_type=F32)


def _mod_kernel(c_ref, w_ref, b_ref, o_ref):
    c = c_ref[...]
    ca = c * jax.nn.sigmoid(c)
    o_ref[...] = _dot(ca.astype(BF16), w_ref[...].astype(BF16)) + b_ref[...]


def _modulation(c, w_mod, b_mod):
    bsz, d = c.shape
    n6 = w_mod.shape[1]
    return pl.pallas_call(
        _mod_kernel,
        grid=(n6 // d,),
        in_specs=[pl.BlockSpec((bsz, d), lambda j: (0, 0)),
                  pl.BlockSpec((d, d), lambda j: (0, j)),
                  pl.BlockSpec((1, d), lambda j: (0, j))],
        out_specs=pl.BlockSpec((bsz, d), lambda j: (0, j)),
        out_shape=jax.ShapeDtypeStruct((bsz, n6), F32),
        compiler_params=_cparams("parallel"),
        name="modulation",
    )(c, w_mod, b_mod.reshape(1, n6))


def _in_kernel(x_ref, g_ref, sc_ref, sh_ref, w_ref, kvg_ref, pavg_ref, wsp_ref, bsp_ref,
               ua_ref, yb_ref, q_ref, ckv_ref, qi_ref, ki_ref, wi_ref, ckvt_ref):
    tm = x_ref.shape[0]
    x = x_ref[...]
    ms = jnp.mean(x * x, axis=-1, keepdims=True)
    h = x * lax.rsqrt(ms + RMS_EPS) * g_ref[...]
    h = h * (1.0 + sc_ref[0]) + sh_ref[0]
    proj = _dot(h.astype(BF16), w_ref[...])

    ua_ref[...] = proj[:, 0:COL_UB]
    q_ref[...] = proj[:, COL_Q:COL_CKV].astype(BF16)
    ckv = proj[:, COL_CKV:COL_QI]
    ckv_ms = jnp.mean(ckv * ckv, axis=-1, keepdims=True)
    ckv_n = ckv * lax.rsqrt(ckv_ms + RMS_EPS) * kvg_ref[...]
    ckv_ref[...] = ckv_n.astype(BF16)
    ckvt_ref[...] = ckv_n.T.astype(BF16)
    qi_ref[...] = proj[:, COL_QI:COL_KW].astype(BF16)
    kw = proj[:, COL_KW:IN_PAD]
    ki_ref[...] = kw[:, 0:IDX_DIM].astype(BF16)
    wi_ref[...] = kw

    u_b = proj[:, COL_UB:COL_VB]
    v_b = proj[:, COL_VB:COL_Q]
    pavg = pavg_ref[...]

    def head_mean(a):
        hi = a.astype(BF16)
        lo = (a - hi.astype(F32)).astype(BF16)
        return _dot(hi, pavg) + _dot(lo, pavg)

    mu = head_mean(v_b)
    dv = v_b - mu
    var = head_mean(dv * dv)
    vn = (dv * lax.rsqrt(var + RMS_EPS)).astype(BF16)
    lane_head = lax.broadcasted_iota(jnp.int32, (GMLP_BLOCK, GMLP_WIDTH), 1) // GMLP_HEAD_DIM
    for blk in range(tm // GMLP_BLOCK):
        rows = slice(blk * GMLP_BLOCK, (blk + 1) * GMLP_BLOCK)
        vblk = vn[rows, :]
        mixed = bsp_ref[...]
        for hh in range(GMLP_HEADS):
            mixed = mixed + jnp.where(lane_head == hh, _dot(wsp_ref[hh], vblk), 0.0)
        yb_ref[rows, :] = (u_b[rows, :] * mixed).astype(BF16)


def _input_stage(x2, seq, g, sc, sh, w_in_pad, kvg, pavg, wsp, bsp, tm=512):
    n, d = x2.shape
    tpb = seq // tm
    row = lambda i: (i, 0)
    const2 = lambda i: (0, 0)
    per_b = lambda i: (i // tpb, 0, 0)
    outs = [(SSM_WIDTH, F32), (GMLP_WIDTH, BF16), (ATT_WIDTH, BF16), (KV_RANK, BF16),
            (IDX_HEADS * IDX_DIM, BF16), (IDX_DIM, BF16), (LANES, F32)]
    return pl.pallas_call(
        _in_kernel,
        grid=(n // tm,),
        in_specs=[pl.BlockSpec((tm, d), row),
                  pl.BlockSpec((1, d), const2),
                  pl.BlockSpec((1, 1, d), per_b),
                  pl.BlockSpec((1, 1, d), per_b),
                  pl.BlockSpec((d, IN_PAD), const2),
                  pl.BlockSpec((1, KV_RANK), const2),
                  pl.BlockSpec((GMLP_WIDTH, GMLP_WIDTH), const2),
                  pl.BlockSpec((GMLP_HEADS, GMLP_BLOCK, GMLP_BLOCK), lambda i: (0, 0, 0)),
                  pl.BlockSpec((GMLP_BLOCK, GMLP_WIDTH), const2)],
        out_specs=[pl.BlockSpec((tm, w), row) for w, _ in outs] + [pl.BlockSpec((KV_RANK, tm), lambda i: (0, i))],
        out_shape=[jax.ShapeDtypeStruct((n, w), dt) for w, dt in outs] + [jax.ShapeDtypeStruct((KV_RANK, n), BF16)],
        compiler_params=_cparams("parallel"),
        name="input_proj_gmlp",
    )(x2, g, sc, sh, w_in_pad, kvg, pavg, wsp, bsp)


def _ssm_prep_tm(a_re_log, a_im, b_re, b_im, c_re, c_im, log_dt, t_len):
    hp = lax.Precision.HIGHEST
    g_n, h_n, p_n = SSM_GROUPS, SSM_GROUP, SSM_STATE
    lam_re = -jnp.exp(a_re_log)
    lam_im = a_im
    dt = jnp.exp(log_dt)[:, None]
    mag = jnp.exp(lam_re * dt)
    abar_re = mag * jnp.cos(lam_im * dt)
    abar_im = mag * jnp.sin(lam_im * dt)
    den = lam_re * lam_re + lam_im * lam_im
    p = abar_re - 1.0
    qq = abar_im
    f_re = (p * lam_re + qq * lam_im) / den
    f_im = (qq * lam_re - p * lam_im) / den
    bb_re = f_re[..., None] * b_re - f_im[..., None] * b_im
    bb_im = f_re[..., None] * b_im + f_im[..., None] * b_re
    k = jnp.arange(t_len + 1, dtype=F32)[:, None, None]
    pmag = jnp.exp(k * (lam_re * dt))
    ang = k * (lam_im * dt)
    pw_re = pmag * jnp.cos(ang)
    pw_im = pmag * jnp.sin(ang)
    cp_re = c_re[None] * pw_re[:, :, None, :] - c_im[None] * pw_im[:, :, None, :]
    cp_im = c_re[None] * pw_im[:, :, None, :] + c_im[None] * pw_re[:, :, None, :]
    kern = (jnp.einsum('kghp,gpj->kghj', cp_re[:t_len], bb_re, precision=hp)
            - jnp.einsum('kghp,gpj->kghj', cp_im[:t_len], bb_im, precision=hp))
    grp_w = jnp.arange(g_n * h_n) // h_n
    grp_s = jnp.arange(g_n * p_n) // p_n
    kb = jnp.tile(kern.transpose(0, 1, 3, 2).reshape(t_len, g_n * h_n, h_n), (1, 1, g_n))
    kb = jnp.where((grp_w[:, None] == grp_w[None, :])[None], kb, 0.0)
    in_mask = grp_w[:, None] == grp_s[None, :]
    b_in = jnp.concatenate(
        [jnp.where(in_mask, jnp.tile(m.transpose(0, 2, 1).reshape(g_n * h_n, p_n), (1, g_n)), 0.0)
         for m in (bb_re, bb_im)], axis=1)
    c_out = jnp.concatenate(
        [jnp.where(in_mask.T, jnp.tile(m.transpose(0, 2, 1).reshape(g_n * p_n, h_n), (1, g_n)), 0.0)
         for m in (c_re, -c_im)], axis=0)
    tt = jnp.arange(t_len)
    flat = lambda a: a.reshape(a.shape[0], g_n * p_n)
    pw_end = jnp.concatenate([flat(pw_re[t_len - 1 - tt]), flat(pw_im[t_len - 1 - tt])], axis=1)
    pw_nxt = jnp.concatenate([flat(pw_re[1 + tt]), flat(pw_im[1 + tt])], axis=1)
    a_t = jnp.concatenate([flat(pw_re[t_len:]), flat(pw_im[t_len:])], axis=1)
    return kb.astype(BF16), b_in.astype(BF16), c_out.astype(BF16), pw_end, pw_nxt, a_t


def _ssm_tm_kernel(u_ref, kb_ref, bin_ref, cout_ref, pwe_ref, pwn_ref, at_ref, d_ref, y_ref,
                   ubuf_ref, sp_ref, st_ref, *, t_len):
    tm, width = u_ref.shape
    n_st = at_ref.shape[1] // 2
    nch = tm // t_len

    @pl.when(pl.program_id(1) == 0)
    def _():
        st_ref[...] = jnp.zeros_like(st_ref)

    u = u_ref[...]
    ubuf_ref[0:t_len, :] = jnp.zeros((t_len, width), F32)
    ubuf_ref[t_len:, :] = u
    pos = lax.broadcasted_iota(jnp.int32, (tm, 1), 0) % t_len
    y = _dot(u.astype(BF16), kb_ref[0])
    for k in range(1, t_len):
        shifted = ubuf_ref[t_len - k:t_len - k + tm, :]
        y = y + _dot(jnp.where(pos >= k, shifted, 0.0).astype(BF16), kb_ref[k])

    bu = _dot(u.astype(BF16), bin_ref[...]).reshape(nch, t_len, 2 * n_st)
    b_re, b_im = bu[:, :, :n_st], bu[:, :, n_st:]
    w_re, w_im = pwe_ref[:, :n_st][None], pwe_ref[:, n_st:][None]
    loc_re = jnp.sum(b_re * w_re - b_im * w_im, axis=1)
    loc_im = jnp.sum(b_re * w_im + b_im * w_re, axis=1)

    a_re, a_im = at_ref[:, :n_st], at_ref[:, n_st:]
    s_re, s_im = st_ref[:, :n_st], st_ref[:, n_st:]
    for c in range(nch):
        sp_ref[c:c + 1, :n_st] = s_re
        sp_ref[c:c + 1, n_st:] = s_im
        s_re, s_im = (a_re * s_re - a_im * s_im + loc_re[c:c + 1, :],
                      a_re * s_im + a_im * s_re + loc_im[c:c + 1, :])
    st_ref[:, :n_st] = s_re
    st_ref[:, n_st:] = s_im

    p_re, p_im = sp_ref[:, :n_st][:, None, :], sp_ref[:, n_st:][:, None, :]
    n_re, n_im = pwn_ref[:, :n_st][None], pwn_ref[:, n_st:][None]
    e_re = (p_re * n_re - p_im * n_im).reshape(tm, n_st).astype(BF16)
    e_im = (p_re * n_im + p_im * n_re).reshape(tm, n_st).astype(BF16)
    y = y + _dot(e_re, cout_ref[:n_st, :]) + _dot(e_im, cout_ref[n_st:, :])
    y_ref[...] = y + d_ref[...] * u


def _ssm_scan_tm(u_a, bsz, seq, prep, d_skip, tm=512):
    kb, b_in, c_out, pw_end, pw_nxt, a_t = prep
    t_len = kb.shape[0]
    n, width = u_a.shape
    tpb = seq // tm
    row = lambda b, i: (b * tpb + i, 0)
    c2 = lambda b, i: (0, 0)
    n2 = b_in.shape[1]
    return pl.pallas_call(
        functools.partial(_ssm_tm_kernel, t_len=t_len),
        grid=(bsz, tpb),
        in_specs=[pl.BlockSpec((tm, width), row),
                  pl.BlockSpec((t_len, width, width), lambda b, i: (0, 0, 0)),
                  pl.BlockSpec((width, n2), c2),
                  pl.BlockSpec((n2, width), c2),
                  pl.BlockSpec((t_len, n2), c2),
                  pl.BlockSpec((t_len, n2), c2),
                  pl.BlockSpec((1, n2), c2),
                  pl.BlockSpec((1, width), c2)],
        out_specs=pl.BlockSpec((tm, width), row),
        out_shape=jax.ShapeDtypeStruct((n, width), F32),
        scratch_shapes=[pltpu.VMEM((t_len + tm, width), F32),
                        pltpu.VMEM((tm // t_len, n2), F32),
                        pltpu.VMEM((1, n2), F32)],
        compiler_params=_cparams("parallel", "arbitrary"),
        name="ssm_scan",
    )(u_a, kb, b_in, c_out, pw_end, pw_nxt, a_t, d_skip)


def _dsa_kernel(q_ref, qi_ref, wi_ref, ki_ref, ckv_ref, ckvt_ref, wuk_ref, wuvt_ref, tri_ref, o_ref,
                key_ref, dm_ref, qa_ref, ot_ref, *acc_refs, topk, kb_len):
    tq = q_ref.shape[0]
    t0 = pl.program_id(1) * tq
    n_kb = (t0 + tq + kb_len - 1) // kb_len
    q_pos = t0 + lax.broadcasted_iota(jnp.int32, (1, tq), 1)
    q_chunk = q_pos // CHUNK
    k_off = lax.broadcasted_iota(jnp.int32, (kb_len, 1), 0)

    def fold8(m, op):
        r = m.reshape(kb_len // 8, 8, tq)
        n = kb_len // 8
        while n > 1:
            n //= 2
            r = op(r[:n], r[n:2 * n])
        return r[0]

    def col_sum(m):
        return fold8(m, jnp.add)

    wi_t = wi_ref[...].T
    qi_h = [qi_ref[:, h * IDX_DIM:(h + 1) * IDX_DIM] for h in range(IDX_HEADS)]
    w_h = [wi_t[IDX_DIM + h:IDX_DIM + h + 1, :] * ((IDX_HEADS ** -0.5) * (IDX_DIM ** -0.5))
           for h in range(IDX_HEADS)]

    def score_block(kb, carry):
        k0 = pl.multiple_of(kb * kb_len, kb_len)
        ki = ki_ref[pl.ds(k0, kb_len), :]
        score = jnp.zeros((kb_len, tq), F32)
        for h in range(IDX_HEADS):
            score = score + jnp.maximum(_dot_nt(ki, qi_h[h]), 0.0) * w_h[h]
        score = jnp.where(score == 0.0, 0.0, score)
        bits = lax.bitcast_convert_type(score, jnp.int32)
        key = jnp.where(bits < 0, bits ^ jnp.int32(0x7FFFFFFF), bits)
        adm = ((k0 + k_off) // CHUNK) <= q_chunk
        key_ref[pl.ds(k0, kb_len), :] = jnp.where(adm, key, jnp.int32(INT_MIN))
        return carry

    lax.fori_loop(0, n_kb, score_block, 0)

    def count(pred):
        def body(kb, acc):
            k0 = pl.multiple_of(kb * kb_len, kb_len)
            return acc + col_sum(pred(key_ref[pl.ds(k0, kb_len), :]).astype(F32))
        acc = lax.fori_loop(0, n_kb, body, jnp.zeros((8, tq), F32))
        return jnp.sum(acc, axis=0, keepdims=True)

    kf = jnp.float32(topk)

    def bit_step(step, ans):
        trial = ans + jnp.left_shift(jnp.int32(1), 31 - step)
        return jnp.where(count(lambda kblk: kblk >= trial) >= kf, trial, ans)

    ans = lax.fori_loop(0, 32, bit_step, jnp.full((1, tq), INT_MIN, jnp.int32))

    room = kf - count(lambda kblk: kblk > ans)
    tri = tri_ref[...]

    def select_block(kb, carry):
        k0 = pl.multiple_of(kb * kb_len, kb_len)
        kblk = key_ref[pl.ds(k0, kb_len), :]
        eq = kblk == ans
        pre = _dot(tri, eq.astype(BF16)) + carry
        sel = (kblk > ans) | (eq & (pre <= room))
        k_pos = k0 + k_off
        adm = (k_pos // CHUNK) <= q_chunk
        dist = jnp.abs(q_pos - k_pos).astype(F32)
        dm_ref[pl.ds(k0, kb_len), :] = jnp.where(sel & adm, dist, -NEG)
        return carry + jnp.sum(col_sum(eq.astype(F32)), axis=0, keepdims=True)

    lax.fori_loop(0, n_kb, select_block, jnp.zeros((1, tq), F32))

    for h in range(ATT_HEADS):
        hs = slice(h * ATT_HEAD_DIM, (h + 1) * ATT_HEAD_DIM)
        qa_ref[h] = (_dot(q_ref[:, hs], wuk_ref[h]) * (ATT_HEAD_DIM ** -0.5)).astype(BF16)
        acc_refs[h][...] = jnp.zeros((KV_RANK, tq), F32)

    def att_block(kb, carry):
        k0 = pl.multiple_of(kb * kb_len, kb_len)
        ckv = ckv_ref[pl.ds(k0, kb_len), :]
        ckvt = ckvt_ref[:, pl.ds(k0, kb_len)]
        dm = dm_ref[pl.ds(k0, kb_len), :]
        new = []
        qk = _dot_nt(ckv, qa_ref[0])
        for h in range(ATT_HEADS):
            m_run, l_run = carry[2 * h], carry[2 * h + 1]
            slope = 2.0 ** (-8.0 * (h + 1) / ATT_HEADS)
            lg = qk - slope * dm
            if h + 1 < ATT_HEADS:
                qk = _dot_nt(ckv, qa_ref[h + 1])
            m_new = jnp.maximum(m_run, jnp.max(fold8(lg, jnp.maximum), axis=0, keepdims=True))
            alpha = jnp.exp(m_run - m_new)
            p = jnp.exp(lg - m_new)
            new += [m_new, alpha * l_run + jnp.sum(col_sum(p), axis=0, keepdims=True)]
            acc_refs[h][...] = alpha * acc_refs[h][...] + _dot(ckvt, p.astype(BF16))
        return tuple(new)

    init = tuple(jnp.full((1, tq), -jnp.inf if i % 2 == 0 else 0.0, F32) for i in range(2 * ATT_HEADS))
    fin = lax.fori_loop(0, n_kb, att_block, init)
    for h in range(ATT_HEADS):
        o_t = (acc_refs[h][...] / fin[2 * h + 1]).astype(BF16)
        ot_ref[h * ATT_HEAD_DIM:(h + 1) * ATT_HEAD_DIM, :] = _dot(wuvt_ref[h], o_t)
    o_ref[...] = ot_ref[...].T.astype(o_ref.dtype)


def _dsa(q, qi, wi, ki, ckv, ckvt, w_uk, w_uv, bsz, seq, tq=256, kb_len=512):
    n = q.shape[0]
    nqt = seq // tq
    kb_len = min(kb_len, seq)
    topk = min(TOPK_MAX, seq // 4)
    tri = (jnp.arange(kb_len)[:, None] >= jnp.arange(kb_len)[None, :]).astype(BF16)
    qrow = lambda b, i: (b * nqt + i, 0)
    krow = lambda b, i: (b, 0)
    c3 = lambda b, i: (0, 0, 0)
    return pl.pallas_call(
        functools.partial(_dsa_kernel, topk=topk, kb_len=kb_len),
        grid=(bsz, nqt),
        in_specs=[pl.BlockSpec((tq, ATT_WIDTH), qrow),
                  pl.BlockSpec((tq, IDX_HEADS * IDX_DIM), qrow),
                  pl.BlockSpec((tq, LANES), qrow),
                  pl.BlockSpec((seq, IDX_DIM), krow),
                  pl.BlockSpec((seq, KV_RANK), krow),
                  pl.BlockSpec((KV_RANK, seq), lambda b, i: (0, b)),
                  pl.BlockSpec((ATT_HEADS, ATT_HEAD_DIM, KV_RANK), c3),
                  pl.BlockSpec((ATT_HEADS, ATT_HEAD_DIM, KV_RANK), c3),
                  pl.BlockSpec((kb_len, kb_len), lambda b, i: (0, 0))],
        out_specs=pl.BlockSpec((tq, ATT_WIDTH), qrow),
        out_shape=jax.ShapeDtypeStruct((n, ATT_WIDTH), BF16),
        scratch_shapes=[pltpu.VMEM((seq, tq), jnp.int32),
                        pltpu.VMEM((seq, tq), F32),
                        pltpu.VMEM((ATT_HEADS, tq, KV_RANK), BF16),
                        pltpu.VMEM((ATT_WIDTH, tq), F32)]
        + [pltpu.VMEM((KV_RANK, tq), F32) for _ in range(ATT_HEADS)],
        compiler_params=_cparams("parallel", "parallel"),
        name="dsa_attention",
    )(q, qi, wi, ki, ckv, ckvt, w_uk.astype(BF16), jnp.swapaxes(w_uv, 1, 2).astype(BF16), tri)


def _out_kernel(ys_ref, yb_ref, yc_ref, x_ref, wglu_ref, bglu_ref, wo_ref, g1_ref,
                n2_ref, sc_ref, sh_ref, xo_ref, h2_ref):
    yg = jax.nn.gelu(ys_ref[...])
    z = _dot(yg.astype(BF16), wglu_ref[...]) + bglu_ref[...]
    ya = yg * jax.nn.sigmoid(z)
    a_w, b_w = SSM_WIDTH, SSM_WIDTH + GMLP_WIDTH
    mix = (_dot(ya.astype(BF16), wo_ref[0:a_w, :])
           + _dot(yb_ref[...], wo_ref[a_w:b_w, :])
           + _dot(yc_ref[...], wo_ref[b_w:, :]))
    xn = x_ref[...] + g1_ref[0] * mix
    xo_ref[...] = xn
    ms = jnp.mean(xn * xn, axis=-1, keepdims=True)
    h = xn * lax.rsqrt(ms + RMS_EPS) * n2_ref[...]
    h2_ref[...] = (h * (1.0 + sc_ref[0]) + sh_ref[0]).astype(BF16)


def _output_stage(ys, yb, yc, x2, seq, w_glu, b_glu, w_out, g1, n2g, sc2, sh2, tm=512):
    n, d = x2.shape
    tpb = seq // tm
    row = lambda i: (i, 0)
    const2 = lambda i: (0, 0)
    per_b = lambda i: (i // tpb, 0, 0)
    return pl.pallas_call(
        _out_kernel,
        grid=(n // tm,),
        in_specs=[pl.BlockSpec((tm, SSM_WIDTH), row),
                  pl.BlockSpec((tm, GMLP_WIDTH), row),
                  pl.BlockSpec((tm, ATT_WIDTH), row),
                  pl.BlockSpec((tm, d), row),
                  pl.BlockSpec((SSM_WIDTH, SSM_WIDTH), const2),
                  pl.BlockSpec((1, SSM_WIDTH), const2),
                  pl.BlockSpec((d, d), const2),
                  pl.BlockSpec((1, 1, d), per_b),
                  pl.BlockSpec((1, d), const2),
                  pl.BlockSpec((1, 1, d), per_b),
                  pl.BlockSpec((1, 1, d), per_b)],
        out_specs=[pl.BlockSpec((tm, d), row), pl.BlockSpec((tm, d), row)],
        out_shape=[jax.ShapeDtypeStruct((n, d), F32), jax.ShapeDtypeStruct((n, d), BF16)],
        compiler_params=_cparams("parallel"),
        name="out_proj_norm2",
    )(ys, yb, yc, x2, w_glu, b_glu, w_out, g1, n2g, sc2, sh2)


def _sort16_network():
    def merge(lo, hi, r):
        step = r * 2
        if step < hi - lo:
            yield from merge(lo, hi, step)
            yield from merge(lo + r, hi, step)
            yield from [(i, i + r) for i in range(lo + r, hi - r, step)]
        else:
            yield (lo, lo + r)

    def sort(lo, hi):
        if hi - lo >= 1:
            mid = lo + (hi - lo) // 2
            yield from sort(lo, mid)
            yield from sort(mid + 1, hi)
            yield from merge(lo, hi, 1)

    return tuple(sort(0, 15))


def _extract_top(x, n_top):
    tops = []
    for _ in range(n_top):
        mx = jnp.max(x, axis=0, keepdims=True)
        tops.append(mx)
        x = jnp.where(x == mx, -jnp.inf, x)
    return tops


def _extract_top_128(x, n_top):
    sub = x.shape[0] // 16
    rows = [x[i * sub:(i + 1) * sub, :] for i in range(16)]
    for a, b in _sort16_network():
        rows[a], rows[b] = jnp.maximum(rows[a], rows[b]), jnp.minimum(rows[a], rows[b])
    tops = []
    for k in range(n_top):
        mx = jnp.max(rows[0], axis=0, keepdims=True)
        tops.append(mx)
        need = min(16, n_top - 1 - k)
        if need:
            hit = rows[0] == mx
            below = rows[1:need + 1] + ([jnp.full_like(rows[0], -jnp.inf)] if need == 16 else [])
            for d in range(need):
                rows[d] = jnp.where(hit, below[d], rows[d])
    return tops


def _peer_kernel(h2_ref, x_ref, g2_ref, fg_ref, wq_ref, kbd_ref, *rest, final_norm, n_split):
    u_refs, vt_refs = rest[:n_split], rest[n_split:2 * n_split]
    o_ref, st_ref, e1_ref, e2_ref, r2_ref, acc_ref, at_ref, ga_ref, v1_ref, v2_ref, h2t_ref = rest[2 * n_split:]
    j = pl.program_id(1)
    tm = h2_ref.shape[0]
    u_rows = u_refs[0].shape[0]
    v_rows = vt_refs[0].shape[0]
    te = u_rows * n_split
    n1 = te // PEER_N_KEYS
    nk = PEER_N_KEYS
    h2 = h2_ref[...]

    @pl.when(j == 0)
    def _():
        h2t_ref[...] = h2.astype(F32).T.astype(BF16)
        q = _dot(h2, wq_ref[...]).astype(BF16)
        st_ref[...] = _dot_nt(kbd_ref[...], q)

        def per_head(h, carry):
            base = pl.multiple_of(h * 2 * nk, 2 * nk)
            for lc in range(tm // LANES):
                ls = slice(lc * LANES, (lc + 1) * LANES)
                t1 = _extract_top_128(st_ref[pl.ds(base, nk), ls], PEER_TOPK + 1)
                t2 = _extract_top_128(st_ref[pl.ds(base + nk, nk), ls], PEER_TOPK + 1)
                for k in range(PEER_TOPK):
                    v1_ref[k:k + 1, :] = t1[k]
                    v2_ref[k:k + 1, :] = t2[k]
                v2_all = v2_ref[0:PEER_TOPK, :]
                v2_top = v2_ref[0:8, :]
                cand = [t1[0] + v2_all]
                cand += [t1[a] + v2_top for a in range(1, 8)]
                cand.append(v1_ref[8:PEER_TOPK, :] + t2[0])
                top = _extract_top(jnp.concatenate(cand, axis=0), PEER_TOPK + 1)
                z = jnp.zeros_like(top[0])
                for k in range(PEER_TOPK):
                    z = z + jnp.exp(top[k] - top[0])
                nxt = jnp.maximum(top[PEER_TOPK], jnp.maximum(t1[PEER_TOPK] + t2[0], t1[0] + t2[PEER_TOPK]))
                thr = 0.5 * (top[PEER_TOPK - 1] + nxt)
                s1 = st_ref[pl.ds(base, nk), ls]
                s2 = st_ref[pl.ds(base + nk, nk), ls]
                e1_ref[h, :, ls] = jnp.exp(s1 - t1[0]) * (0.5 / z)
                e2_ref[h, :, ls] = jnp.exp(s2 - t2[0]).astype(BF16)
                rank2 = jnp.zeros((nk, LANES), F32)
                for b in range(PEER_TOPK):
                    rank2 = rank2 + jnp.where(s2 < t2[b], 1.0, 0.0)
                r2_ref[h, :, ls] = rank2.astype(BF16)
                n_sel = jnp.zeros((nk, LANES), F32)
                for a in range(PEER_TOPK):
                    n_a = jnp.zeros_like(thr)
                    for b in range(PEER_TOPK // (a + 1)):
                        n_a = n_a + jnp.where(t1[a] + t2[b] >= thr, 1.0, 0.0)
                    n_sel = jnp.where(s1 == t1[a], n_a, n_sel)
                st_ref[pl.ds(base, nk), ls] = n_sel
            return carry

        lax.fori_loop(0, PEER_HEADS, per_head, 0)
        acc_ref[...] = jnp.zeros_like(acc_ref)
        ga_ref[1] = jnp.zeros(ga_ref.shape[1:], BF16)
        for k in range(n_split):
            at_ref[0, k * u_rows:(k + 1) * u_rows, :] = _dot(u_refs[k][...], h2t_ref[...])

    n_blocks = pl.num_programs(1) - 2
    rd = j % 2
    wr = 1 - rd

    @pl.when((j >= 1) & (j <= n_blocks))
    def _():
        for k in range(n_split):
            at_ref[rd, k * u_rows:(k + 1) * u_rows, :] = _dot(u_refs[k][...], h2t_ref[...])
        for k in range(n_split):
            acc_ref[k * v_rows:(k + 1) * v_rows, :] += _dot(vt_refs[k][...], ga_ref[rd])
        r0 = pl.multiple_of((j - 1) * n1, n1)
        for lc in range(tm // LANES):
            ls = slice(lc * LANES, (lc + 1) * LANES)
            for il in range(n1):
                g = jnp.zeros((nk, LANES), BF16)
                for h in range(PEER_HEADS):
                    n_row = st_ref[pl.ds(h * 2 * nk + r0, n1), ls][il:il + 1, :].astype(BF16)
                    e_row = e1_ref[h, pl.ds(r0, n1), ls][il:il + 1, :].astype(BF16)
                    picked = jnp.where(r2_ref[h, :, ls] < n_row, e2_ref[h, :, ls], jnp.zeros((), BF16))
                    g = g + picked * e_row
                a = at_ref[wr, il * nk:(il + 1) * nk, ls]
                inner = a * (0.7978845608028654 + 0.035677408136300125 * (a * a))
                act2 = a + a * jnp.tanh(inner)
                ga_ref[wr, il * nk:(il + 1) * nk, ls] = g * act2.astype(BF16)

    @pl.when(j == n_blocks + 1)
    def _():
        for k in range(n_split):
            acc_ref[k * v_rows:(k + 1) * v_rows, :] += _dot(vt_refs[k][...], ga_ref[rd])
        out = x_ref[...] + g2_ref[0] * acc_ref[...].T
        if final_norm:
            ms = jnp.mean(out * out, axis=-1, keepdims=True)
            out = out * lax.rsqrt(ms + RMS_EPS) * fg_ref[...]
        o_ref[...] = out


def _peer(h2, x2, seq, g2, fg, wq, kbd, u_bf, vt_bf, final_norm, tm=512, te=1024, n_split=4):
    n, d = x2.shape
    n_blk = u_bf.shape[0] // te
    tpb = seq // tm
    trow = lambda i, j: (i, 0)
    c2 = lambda i, j: (0, 0)
    hs = (PEER_HEADS, PEER_N_KEYS, tm)
    u_map = lambda k, i, j: (jnp.minimum(j, n_blk - 1) * n_split + k, 0)
    v_map = lambda k, i, j: (k, jnp.clip(j - 2, 0, n_blk - 1))
    return pl.pallas_call(
        functools.partial(_peer_kernel, final_norm=final_norm, n_split=n_split),
        grid=(n // tm, n_blk + 2),
        in_specs=[pl.BlockSpec((tm, d), trow),
                  pl.BlockSpec((tm, d), trow),
                  pl.BlockSpec((1, 1, d), lambda i, j: (i // tpb, 0, 0)),
                  pl.BlockSpec((1, d), c2),
                  pl.BlockSpec((d, PEER_HEADS * 2 * PEER_HALF), c2),
                  pl.BlockSpec((PEER_HEADS * 2 * PEER_N_KEYS, PEER_HEADS * 2 * PEER_HALF), c2)]
        + [pl.BlockSpec((te // n_split, d), functools.partial(u_map, k)) for k in range(n_split)]
        + [pl.BlockSpec((d // n_split, te), functools.partial(v_map, k)) for k in range(n_split)],
        out_specs=pl.BlockSpec((tm, d), trow),
        out_shape=jax.ShapeDtypeStruct((n, d), F32),
        scratch_shapes=[pltpu.VMEM((PEER_HEADS * 2 * PEER_N_KEYS, tm), F32),
                        pltpu.VMEM(hs, F32), pltpu.VMEM(hs, BF16), pltpu.VMEM(hs, BF16),
                        pltpu.VMEM((d, tm), F32),
                        pltpu.VMEM((2, te, tm), F32),
                        pltpu.VMEM((2, te, tm), BF16),
                        pltpu.VMEM((PEER_TOPK, LANES), F32), pltpu.VMEM((PEER_TOPK, LANES), F32),
                        pltpu.VMEM((d, tm), BF16)],
        compiler_params=_cparams("parallel", "arbitrary"),
        name="peer_dense",
    )(h2, x2, g2, fg, wq, kbd, *([u_bf] * n_split), *([vt_bf] * n_split))


def _peer_key_matrix(k1, k2):
    blocks = []
    for h in range(PEER_HEADS):
        for half, kk in enumerate((k1, k2)):
            col = (2 * h + half) * PEER_HALF
            blocks.append(jnp.pad(kk, ((0, 0), (col, PEER_HEADS * 2 * PEER_HALF - col - PEER_HALF))))
    return jnp.concatenate(blocks, axis=0).astype(BF16)


def kernel(x, c, norm1_g, norm2_g, w_mod, b_mod, w_in, ssm_a_re_log, ssm_a_im, ssm_b_re, ssm_b_im, ssm_c_re, ssm_c_im, ssm_d, ssm_log_dt, ssm_w_glu, ssm_b_glu, gmlp_w_sp, gmlp_b_sp, kv_norm_g, w_uk, w_uv, w_out, peer_w_q, peer_k1, peer_k2, peer_u, peer_v, final_g):
    bsz, seq, d = x.shape
    depth = w_mod.shape[0]
    x2 = x.reshape(bsz * seq, d)
    head_of = jnp.arange(GMLP_WIDTH) // GMLP_HEAD_DIM
    pavg = ((head_of[:, None] == head_of[None, :]).astype(F32) / GMLP_HEAD_DIM).astype(BF16)
    chunk_of = jnp.arange(GMLP_BLOCK) // CHUNK
    sp_mask = chunk_of[:, None] >= chunk_of[None, :]
    for l in range(depth):
        mod = _modulation(c, w_mod[l], b_mod[l])
        sh1, sc1, g1, sh2, sc2, g2 = [mod[:, i * d:(i + 1) * d].reshape(bsz, 1, d) for i in range(6)]
        w_in_pad = jnp.pad(w_in[l], ((0, 0), (0, IN_PAD - IN_WIDTH))).astype(BF16)
        wsp = jnp.where(sp_mask[None], gmlp_w_sp[l], 0.0).astype(BF16)
        bsp = jnp.repeat(gmlp_b_sp[l].T, GMLP_HEAD_DIM, axis=1)
        ua, yb, q, ckv, qi, ki, wi, ckvt = _input_stage(
            x2, seq, norm1_g[l].reshape(1, d), sc1, sh1, w_in_pad, kv_norm_g[l].reshape(1, KV_RANK),
            pavg, wsp, bsp)
        prep = _ssm_prep_tm(ssm_a_re_log[l], ssm_a_im[l], ssm_b_re[l], ssm_b_im[l], ssm_c_re[l], ssm_c_im[l],
                            ssm_log_dt[l], SSM_T)
        ys = _ssm_scan_tm(ua, bsz, seq, prep, ssm_d[l].reshape(1, SSM_WIDTH))
        yc = _dsa(q, qi, wi, ki, ckv, ckvt, w_uk[l], w_uv[l], bsz, seq)
        x2, h2 = _output_stage(
            ys, yb, yc, x2, seq, ssm_w_glu[l].astype(BF16),
            ssm_b_glu[l].reshape(1, SSM_WIDTH), w_out[l].astype(BF16), g1, norm2_g[l].reshape(1, d), sc2, sh2)
        x2 = _peer(h2, x2, seq, g2, final_g.reshape(1, d), peer_w_q[l].astype(BF16),
                   _peer_key_matrix(peer_k1[l], peer_k2[l]), peer_u[l].astype(BF16),
                   peer_v[l].T.astype(BF16), final_norm=(l == depth - 1))
    return x2.reshape(bsz, seq, d)
```

```python
import functools
import math

import jax
import jax.numpy as jnp
from jax import lax
from jax.experimental import pallas as pl
from jax.experimental.pallas import tpu as pltpu

CHUNK = 64
RMS_EPS = 1e-6

SSM_GROUP = 16
SSM_STATE = 64
SSM_WIDTH = 256
SSM_GROUPS = SSM_WIDTH // SSM_GROUP
SSM_T = 16
GMLP_HEADS = 4
GMLP_HEAD_DIM = 64
GMLP_WIDTH = GMLP_HEADS * GMLP_HEAD_DIM
GMLP_BLOCK = 128
ATT_HEADS = 8
ATT_HEAD_DIM = 64
ATT_WIDTH = ATT_HEADS * ATT_HEAD_DIM
KV_RANK = 128
IDX_HEADS = 4
IDX_DIM = 64
TOPK_MAX = 256

IN_SIZES = (SSM_WIDTH, GMLP_WIDTH, GMLP_WIDTH, ATT_WIDTH, KV_RANK, IDX_HEADS * IDX_DIM, IDX_DIM, IDX_HEADS)
IN_WIDTH = sum(IN_SIZES)
IN_PAD = 1792
COL_UB, COL_VB, COL_Q, COL_CKV, COL_QI, COL_KW = 256, 512, 768, 1280, 1408, 1664

PEER_HEADS = 8
PEER_N_KEYS = 128
PEER_HALF = 64
PEER_TOPK = 16

LANES = 128
PEER_LANE_PAD = LANES
VMEM_LIMIT = 56 * 1024 * 1024
NEG = -1e30
INT_MIN = -(2 ** 31)

F32 = jnp.float32
BF16 = jnp.bfloat16


def _cparams(*sem):
    return pltpu.CompilerParams(dimension_semantics=sem, vmem_limit_bytes=VMEM_LIMIT)


def _dot(a, b):
    return jnp.dot(a, b, preferred_element_type=F32)


def _dot_nt(a, b):
    return lax.dot_general(a, b, (((1,), (1,)), ((), ())), preferred_element_type=F32)


def _mod_kernel(c_ref, w_ref, b_ref, o_ref):
    c = c_ref[...]
    ca = c * jax.nn.sigmoid(c)
    o_ref[...] = _dot(ca.astype(BF16), w_ref[...].astype(BF16)) + b_ref[...]


def _modulation(c, w_mod, b_mod):
    bsz, d = c.shape
    n6 = w_mod.shape[1]
    return pl.pallas_call(
        _mod_kernel,
        grid=(n6 // d,),
        in_specs=[pl.BlockSpec((bsz, d), lambda j: (0, 0)),
                  pl.BlockSpec((d, d), lambda j: (0, j)),
                  pl.BlockSpec((1, d), lambda j: (0, j))],
        out_specs=pl.BlockSpec((bsz, d), lambda j: (0, j)),
        out_shape=jax.ShapeDtypeStruct((bsz, n6), F32),
        compiler_params=_cparams("parallel"),
        name="modulation",
    )(c, w_mod, b_mod.reshape(1, n6))


def _in_kernel(x_ref, g_ref, sc_ref, sh_ref, w_ref, kvg_ref, pavg_ref, wsp_ref, bsp_ref,
               ua_ref, yb_ref, q_ref, ckv_ref, qi_ref, ki_ref, wi_ref, ckvt_ref):
    tm = x_ref.shape[0]
    x = x_ref[...]
    ms = jnp.mean(x * x, axis=-1, keepdims=True)
    h = x * lax.rsqrt(ms + RMS_EPS) * g_ref[...]
    h = h * (1.0 + sc_ref[0]) + sh_ref[0]
    proj = _dot(h.astype(BF16), w_ref[...])

    ua_ref[...] = proj[:, 0:COL_UB]
    q_ref[...] = proj[:, COL_Q:COL_CKV].astype(BF16)
    ckv = proj[:, COL_CKV:COL_QI]
    ckv_ms = jnp.mean(ckv * ckv, axis=-1, keepdims=True)
    ckv_n = ckv * lax.rsqrt(ckv_ms + RMS_EPS) * kvg_ref[...]
    ckv_ref[...] = ckv_n.astype(BF16)
    ckvt_ref[...] = ckv_n.T.astype(BF16)
    qi_ref[...] = proj[:, COL_QI:COL_KW].astype(BF16)
    kw = proj[:, COL_KW:IN_PAD]
    ki_ref[...] = kw[:, 0:IDX_DIM].astype(BF16)
    wi_ref[...] = kw

    u_b = proj[:, COL_UB:COL_VB]
    v_b = proj[:, COL_VB:COL_Q]
    pavg = pavg_ref[...]

    def head_mean(a):
        hi = a.astype(BF16)
        lo = (a - hi.astype(F32)).astype(BF16)
        return _dot(hi, pavg) + _dot(lo, pavg)

    mu = head_mean(v_b)
    dv = v_b - mu
    var = head_mean(dv * dv)
    vn = (dv * lax.rsqrt(var + RMS_EPS)).astype(BF16)
    lane_head = lax.broadcasted_iota(jnp.int32, (GMLP_BLOCK, GMLP_WIDTH), 1) // GMLP_HEAD_DIM
    for blk in range(tm // GMLP_BLOCK):
        rows = slice(blk * GMLP_BLOCK, (blk + 1) * GMLP_BLOCK)
        vblk = vn[rows, :]
        mixed = bsp_ref[...]
        for hh in range(GMLP_HEADS):
            mixed = mixed + jnp.where(lane_head == hh, _dot(wsp_ref[hh], vblk), 0.0)
        yb_ref[rows, :] = (u_b[rows, :] * mixed).astype(BF16)


def _input_stage(x2, seq, g, sc, sh, w_in_pad, kvg, pavg, wsp, bsp, tm=512):
    n, d = x2.shape
    tpb = seq // tm
    row = lambda i: (i, 0)
    const2 = lambda i: (0, 0)
    per_b = lambda i: (i // tpb, 0, 0)
    outs = [(SSM_WIDTH, F32), (GMLP_WIDTH, BF16), (ATT_WIDTH, BF16), (KV_RANK, BF16),
            (IDX_HEADS * IDX_DIM, BF16), (IDX_DIM, BF16), (LANES, F32)]
    return pl.pallas_call(
        _in_kernel,
        grid=(n // tm,),
        in_specs=[pl.BlockSpec((tm, d), row),
                  pl.BlockSpec((1, d), const2),
                  pl.BlockSpec((1, 1, d), per_b),
                  pl.BlockSpec((1, 1, d), per_b),
                  pl.BlockSpec((d, IN_PAD), const2),
                  pl.BlockSpec((1, KV_RANK), const2),
                  pl.BlockSpec((GMLP_WIDTH, GMLP_WIDTH), const2),
                  pl.BlockSpec((GMLP_HEADS, GMLP_BLOCK, GMLP_BLOCK), lambda i: (0, 0, 0)),
                  pl.BlockSpec((GMLP_BLOCK, GMLP_WIDTH), const2)],
        out_specs=[pl.BlockSpec((tm, w), row) for w, _ in outs] + [pl.BlockSpec((KV_RANK, tm), lambda i: (0, i))],
        out_shape=[jax.ShapeDtypeStruct((n, w), dt) for w, dt in outs] + [jax.ShapeDtypeStruct((KV_RANK, n), BF16)],
        compiler_params=_cparams("parallel"),
        name="input_proj_gmlp",
    )(x2, g, sc, sh, w_in_pad, kvg, pavg, wsp, bsp)


def _ssm_prep_tm(a_re_log, a_im, b_re, b_im, c_re, c_im, log_dt, t_len):
    hp = lax.Precision.HIGHEST
    g_n, h_n, p_n = SSM_GROUPS, SSM_GROUP, SSM_STATE
    lam_re = -jnp.exp(a_re_log)
    lam_im = a_im
    dt = jnp.exp(log_dt)[:, None]
    mag = jnp.exp(lam_re * dt)
    abar_re = mag * jnp.cos(lam_im * dt)
    abar_im = mag * jnp.sin(lam_im * dt)
    den = lam_re * lam_re + lam_im * lam_im
    p = abar_re - 1.0
    qq = abar_im
    f_re = (p * lam_re + qq * lam_im) / den
    f_im = (qq * lam_re - p * lam_im) / den
    bb_re = f_re[..., None] * b_re - f_im[..., None] * b_im
    bb_im = f_re[..., None] * b_im + f_im[..., None] * b_re
    k = jnp.arange(t_len + 1, dtype=F32)[:, None, None]
    pmag = jnp.exp(k * (lam_re * dt))
    ang = k * (lam_im * dt)
    pw_re = pmag * jnp.cos(ang)
    pw_im = pmag * jnp.sin(ang)
    cp_re = c_re[None] * pw_re[:, :, None, :] - c_im[None] * pw_im[:, :, None, :]
    cp_im = c_re[None] * pw_im[:, :, None, :] + c_im[None] * pw_re[:, :, None, :]
    kern = (jnp.einsum('kghp,gpj->kghj', cp_re[:t_len], bb_re, precision=hp)
            - jnp.einsum('kghp,gpj->kghj', cp_im[:t_len], bb_im, precision=hp))
    grp_w = jnp.arange(g_n * h_n) // h_n
    grp_s = jnp.arange(g_n * p_n) // p_n
    kb = jnp.tile(kern.transpose(0, 1, 3, 2).reshape(t_len, g_n * h_n, h_n), (1, 1, g_n))
    kb = jnp.where((grp_w[:, None] == grp_w[None, :])[None], kb, 0.0)
    in_mask = grp_w[:, None] == grp_s[None, :]
    b_in = jnp.concatenate(
        [jnp.where(in_mask, jnp.tile(m.transpose(0, 2, 1).reshape(g_n * h_n, p_n), (1, g_n)), 0.0)
         for m in (bb_re, bb_im)], axis=1)
    c_out = jnp.concatenate(
        [jnp.where(in_mask.T, jnp.tile(m.transpose(0, 2, 1).reshape(g_n * p_n, h_n), (1, g_n)), 0.0)
         for m in (c_re, -c_im)], axis=0)
    tt = jnp.arange(t_len)
    flat = lambda a: a.reshape(a.shape[0], g_n * p_n)
    pw_end = jnp.concatenate([flat(pw_re[t_len - 1 - tt]), flat(pw_im[t_len - 1 - tt])], axis=1)
    pw_nxt = jnp.concatenate([flat(pw_re[1 + tt]), flat(pw_im[1 + tt])], axis=1)
    a_t = jnp.concatenate([flat(pw_re[t_len:]), flat(pw_im[t_len:])], axis=1)
    return kb.astype(BF16), b_in.astype(BF16), c_out.astype(BF16), pw_end, pw_nxt, a_t


def _ssm_tm_kernel(u_ref, kb_ref, bin_ref, cout_ref, pwe_ref, pwn_ref, at_ref, d_ref, y_ref,
                   ubuf_ref, sp_ref, st_ref, *, t_len):
    tm, width = u_ref.shape
    n_st = at_ref.shape[1] // 2
    nch = tm // t_len

    @pl.when(pl.program_id(1) == 0)
    def _():
        st_ref[...] = jnp.zeros_like(st_ref)

    u = u_ref[...]
    ubuf_ref[0:t_len, :] = jnp.zeros((t_len, width), F32)
    ubuf_ref[t_len:, :] = u
    pos = lax.broadcasted_iota(jnp.int32, (tm, 1), 0) % t_len
    y = _dot(u.astype(BF16), kb_ref[0])
    for k in range(1, t_len):
        shifted = ubuf_ref[t_len - k:t_len - k + tm, :]
        y = y + _dot(jnp.where(pos >= k, shifted, 0.0).astype(BF16), kb_ref[k])

    bu = _dot(u.astype(BF16), bin_ref[...]).reshape(nch, t_len, 2 * n_st)
    b_re, b_im = bu[:, :, :n_st], bu[:, :, n_st:]
    w_re, w_im = pwe_ref[:, :n_st][None], pwe_ref[:, n_st:][None]
    loc_re = jnp.sum(b_re * w_re - b_im * w_im, axis=1)
    loc_im = jnp.sum(b_re * w_im + b_im * w_re, axis=1)

    a_re, a_im = at_ref[:, :n_st], at_ref[:, n_st:]
    s_re, s_im = st_ref[:, :n_st], st_ref[:, n_st:]
    for c in range(nch):
        sp_ref[c:c + 1, :n_st] = s_re
        sp_ref[c:c + 1, n_st:] = s_im
        s_re, s_im = (a_re * s_re - a_im * s_im + loc_re[c:c + 1, :],
                      a_re * s_im + a_im * s_re + loc_im[c:c + 1, :])
    st_ref[:, :n_st] = s_re
    st_ref[:, n_st:] = s_im

    p_re, p_im = sp_ref[:, :n_st][:, None, :], sp_ref[:, n_st:][:, None, :]
    n_re, n_im = pwn_ref[:, :n_st][None], pwn_ref[:, n_st:][None]
    e_re = (p_re * n_re - p_im * n_im).reshape(tm, n_st).astype(BF16)
    e_im = (p_re * n_im + p_im * n_re).reshape(tm, n_st).astype(BF16)
    y = y + _dot(e_re, cout_ref[:n_st, :]) + _dot(e_im, cout_ref[n_st:, :])
    y_ref[...] = y + d_ref[...] * u


def _ssm_scan_tm(u_a, bsz, seq, prep, d_skip, tm=512):
    kb, b_in, c_out, pw_end, pw_nxt, a_t = prep
    t_len = kb.shape[0]
    n, width = u_a.shape
    tpb = seq // tm
    row = lambda b, i: (b * tpb + i, 0)
    c2 = lambda b, i: (0, 0)
    n2 = b_in.shape[1]
    return pl.pallas_call(
        functools.partial(_ssm_tm_kernel, t_len=t_len),
        grid=(bsz, tpb),
        in_specs=[pl.BlockSpec((tm, width), row),
                  pl.BlockSpec((t_len, width, width), lambda b, i: (0, 0, 0)),
                  pl.BlockSpec((width, n2), c2),
                  pl.BlockSpec((n2, width), c2),
                  pl.BlockSpec((t_len, n2), c2),
                  pl.BlockSpec((t_len, n2), c2),
                  pl.BlockSpec((1, n2), c2),
                  pl.BlockSpec((1, width), c2)],
        out_specs=pl.BlockSpec((tm, width), row),
        out_shape=jax.ShapeDtypeStruct((n, width), F32),
        scratch_shapes=[pltpu.VMEM((t_len + tm, width), F32),
                        pltpu.VMEM((tm // t_len, n2), F32),
                        pltpu.VMEM((1, n2), F32)],
        compiler_params=_cparams("parallel", "arbitrary"),
        name="ssm_scan",
    )(u_a, kb, b_in, c_out, pw_end, pw_nxt, a_t, d_skip)


def _dsa_kernel(q_ref, qi_ref, wi_ref, ki_ref, ckv_ref, ckvt_ref, wuk_ref, wuvt_ref, tri_ref, o_ref,
                key_ref, dm_ref, qa_ref, ot_ref, *acc_refs, topk, kb_len):
    tq = q_ref.shape[0]
    t0 = pl.program_id(1) * tq
    n_kb = (t0 + tq + kb_len - 1) // kb_len
    q_pos = t0 + lax.broadcasted_iota(jnp.int32, (1, tq), 1)
    q_chunk = q_pos // CHUNK
    k_off = lax.broadcasted_iota(jnp.int32, (kb_len, 1), 0)

    def fold8(m, op):
        r = m.reshape(kb_len // 8, 8, tq)
        n = kb_len // 8
        while n > 1:
            n //= 2
            r = op(r[:n], r[n:2 * n])
        return r[0]

    def col_sum(m):
        return fold8(m, jnp.add)

    wi_t = wi_ref[...].T
    qi_h = [qi_ref[:, h * IDX_DIM:(h + 1) * IDX_DIM] for h in range(IDX_HEADS)]
    w_h = [wi_t[IDX_DIM + h:IDX_DIM + h + 1, :] * ((IDX_HEADS ** -0.5) * (IDX_DIM ** -0.5))
           for h in range(IDX_HEADS)]

    def score_block(kb, carry):
        k0 = pl.multiple_of(kb * kb_len, kb_len)
        ki = ki_ref[pl.ds(k0, kb_len), :]
        score = jnp.zeros((kb_len, tq), F32)
        for h in range(IDX_HEADS):
            score = score + jnp.maximum(_dot_nt(ki, qi_h[h]), 0.0) * w_h[h]
        score = jnp.where(score == 0.0, 0.0, score)
        bits = lax.bitcast_convert_type(score, jnp.int32)
        key = jnp.where(bits < 0, bits ^ jnp.int32(0x7FFFFFFF), bits)
        adm = ((k0 + k_off) // CHUNK) <= q_chunk
        key_ref[pl.ds(k0, kb_len), :] = jnp.where(adm, key, jnp.int32(INT_MIN))
        return carry

    lax.fori_loop(0, n_kb, score_block, 0)

    def count(pred):
        def body(kb, acc):
            k0 = pl.multiple_of(kb * kb_len, kb_len)
            return acc + col_sum(pred(key_ref[pl.ds(k0, kb_len), :]).astype(F32))
        acc = lax.fori_loop(0, n_kb, body, jnp.zeros((8, tq), F32))
        return jnp.sum(acc, axis=0, keepdims=True)

    kf = jnp.float32(topk)

    def bit_step(step, ans):
        trial = ans + jnp.left_shift(jnp.int32(1), 31 - step)
        return jnp.where(count(lambda kblk: kblk >= trial) >= kf, trial, ans)

    ans = lax.fori_loop(0, 32, bit_step, jnp.full((1, tq), INT_MIN, jnp.int32))

    room = kf - count(lambda kblk: kblk > ans)
    tri = tri_ref[...]

    def select_block(kb, carry):
        k0 = pl.multiple_of(kb * kb_len, kb_len)
        kblk = key_ref[pl.ds(k0, kb_len), :]
        eq = kblk == ans
        pre = _dot(tri, eq.astype(BF16)) + carry
        sel = (kblk > ans) | (eq & (pre <= room))
        k_pos = k0 + k_off
        adm = (k_pos // CHUNK) <= q_chunk
        dist = jnp.abs(q_pos - k_pos).astype(F32)
        dm_ref[pl.ds(k0, kb_len), :] = jnp.where(sel & adm, dist, -NEG)
        return carry + jnp.sum(col_sum(eq.astype(F32)), axis=0, keepdims=True)

    lax.fori_loop(0, n_kb, select_block, jnp.zeros((1, tq), F32))

    for h in range(ATT_HEADS):
        hs = slice(h * ATT_HEAD_DIM, (h + 1) * ATT_HEAD_DIM)
        qa_ref[h] = (_dot(q_ref[:, hs], wuk_ref[h]) * (ATT_HEAD_DIM ** -0.5)).astype(BF16)
        acc_refs[h][...] = jnp.zeros((KV_RANK, tq), F32)

    def att_block(kb, carry):
        k0 = pl.multiple_of(kb * kb_len, kb_len)
        ckv = ckv_ref[pl.ds(k0, kb_len), :]
        ckvt = ckvt_ref[:, pl.ds(k0, kb_len)]
        dm = dm_ref[pl.ds(k0, kb_len), :]
        new = []
        qk = _dot_nt(ckv, qa_ref[0])
        for h in range(ATT_HEADS):
            m_run, l_run = carry[2 * h], carry[2 * h + 1]
            slope = 2.0 ** (-8.0 * (h + 1) / ATT_HEADS)
            lg = qk - slope * dm
            if h + 1 < ATT_HEADS:
                qk = _dot_nt(ckv, qa_ref[h + 1])
            m_new = jnp.maximum(m_run, jnp.max(fold8(lg, jnp.maximum), axis=0, keepdims=True))
            alpha = jnp.exp(m_run - m_new)
            p = jnp.exp(lg - m_new)
            new += [m_new, alpha * l_run + jnp.sum(col_sum(p), axis=0, keepdims=True)]
            acc_refs[h][...] = alpha * acc_refs[h][...] + _dot(ckvt, p.astype(BF16))
        return tuple(new)

    init = tuple(jnp.full((1, tq), -jnp.inf if i % 2 == 0 else 0.0, F32) for i in range(2 * ATT_HEADS))
    fin = lax.fori_loop(0, n_kb, att_block, init)
    for h in range(ATT_HEADS):
        o_t = (acc_refs[h][...] / fin[2 * h + 1]).astype(BF16)
        ot_ref[h * ATT_HEAD_DIM:(h + 1) * ATT_HEAD_DIM, :] = _dot(wuvt_ref[h], o_t)
    o_ref[...] = ot_ref[...].T.astype(o_ref.dtype)


def _dsa(q, qi, wi, ki, ckv, ckvt, w_uk, w_uv, bsz, seq, tq=256, kb_len=512):
    n = q.shape[0]
    nqt = seq // tq
    kb_len = min(kb_len, seq)
    topk = min(TOPK_MAX, seq // 4)
    tri = (jnp.arange(kb_len)[:, None] >= jnp.arange(kb_len)[None, :]).astype(BF16)
    qrow = lambda b, i: (b * nqt + i, 0)
    krow = lambda b, i: (b, 0)
    c3 = lambda b, i: (0, 0, 0)
    return pl.pallas_call(
        functools.partial(_dsa_kernel, topk=topk, kb_len=kb_len),
        grid=(bsz, nqt),
        in_specs=[pl.BlockSpec((tq, ATT_WIDTH), qrow),
                  pl.BlockSpec((tq, IDX_HEADS * IDX_DIM), qrow),
                  pl.BlockSpec((tq, LANES), qrow),
                  pl.BlockSpec((seq, IDX_DIM), krow),
                  pl.BlockSpec((seq, KV_RANK), krow),
                  pl.BlockSpec((KV_RANK, seq), lambda b, i: (0, b)),
                  pl.BlockSpec((ATT_HEADS, ATT_HEAD_DIM, KV_RANK), c3),
                  pl.BlockSpec((ATT_HEADS, ATT_HEAD_DIM, KV_RANK), c3),
                  pl.BlockSpec((kb_len, kb_len), lambda b, i: (0, 0))],
        out_specs=pl.BlockSpec((tq, ATT_WIDTH), qrow),
        out_shape=jax.ShapeDtypeStruct((n, ATT_WIDTH), BF16),
        scratch_shapes=[pltpu.VMEM((seq, tq), jnp.int32),
                        pltpu.VMEM((seq, tq), F32),
                        pltpu.VMEM((ATT_HEADS, tq, KV_RANK), BF16),
                        pltpu.VMEM((ATT_WIDTH, tq), F32)]
        + [pltpu.VMEM((KV_RANK, tq), F32) for _ in range(ATT_HEADS)],
        compiler_params=_cparams("parallel", "parallel"),
        name="dsa_attention",
    )(q, qi, wi, ki, ckv, ckvt, w_uk.astype(BF16), jnp.swapaxes(w_uv, 1, 2).astype(BF16), tri)


def _out_kernel(ys_ref, yb_ref, yc_ref, x_ref, wglu_ref, bglu_ref, wo_ref, g1_ref,
                n2_ref, sc_ref, sh_ref, xo_ref, h2_ref):
    yg = jax.nn.gelu(ys_ref[...])
    z = _dot(yg.astype(BF16), wglu_ref[...]) + bglu_ref[...]
    ya = yg * jax.nn.sigmoid(z)
    a_w, b_w = SSM_WIDTH, SSM_WIDTH + GMLP_WIDTH
    mix = (_dot(ya.astype(BF16), wo_ref[0:a_w, :])
           + _dot(yb_ref[...], wo_ref[a_w:b_w, :])
           + _dot(yc_ref[...], wo_ref[b_w:, :]))
    xn = x_ref[...] + g1_ref[0] * mix
    xo_ref[...] = xn
    ms = jnp.mean(xn * xn, axis=-1, keepdims=True)
    h = xn * lax.rsqrt(ms + RMS_EPS) * n2_ref[...]
    h2_ref[...] = (h * (1.0 + sc_ref[0]) + sh_ref[0]).astype(BF16)


def _output_stage(ys, yb, yc, x2, seq, w_glu, b_glu, w_out, g1, n2g, sc2, sh2, tm=512):
    n, d = x2.shape
    tpb = seq // tm
    row = lambda i: (i, 0)
    const2 = lambda i: (0, 0)
    per_b = lambda i: (i // tpb, 0, 0)
    return pl.pallas_call(
        _out_kernel,
        grid=(n // tm,),
        in_specs=[pl.BlockSpec((tm, SSM_WIDTH), row),
                  pl.BlockSpec((tm, GMLP_WIDTH), row),
                  pl.BlockSpec((tm, ATT_WIDTH), row),
                  pl.BlockSpec((tm, d), row),
                  pl.BlockSpec((SSM_WIDTH, SSM_WIDTH), const2),
                  pl.BlockSpec((1, SSM_WIDTH), const2),
                  pl.BlockSpec((d, d), const2),
                  pl.BlockSpec((1, 1, d), per_b),
                  pl.BlockSpec((1, d), const2),
                  pl.BlockSpec((1, 1, d), per_b),
                  pl.BlockSpec((1, 1, d), per_b)],
        out_specs=[pl.BlockSpec((tm, d), row), pl.BlockSpec((tm, d), row)],
        out_shape=[jax.ShapeDtypeStruct((n, d), F32), jax.ShapeDtypeStruct((n, d), BF16)],
        compiler_params=_cparams("parallel"),
        name="out_proj_norm2",
    )(ys, yb, yc, x2, w_glu, b_glu, w_out, g1, n2g, sc2, sh2)


def _sort16_network():
    def merge(lo, hi, r):
        step = r * 2
        if step < hi - lo:
            yield from merge(lo, hi, step)
            yield from merge(lo + r, hi, step)
            yield from [(i, i + r) for i in range(lo + r, hi - r, step)]
        else:
            yield (lo, lo + r)

    def sort(lo, hi):
        if hi - lo >= 1:
            mid = lo + (hi - lo) // 2
            yield from sort(lo, mid)
            yield from sort(mid + 1, hi)
            yield from merge(lo, hi, 1)

    return tuple(sort(0, 15))


def _extract_top(x, n_top):
    tops = []
    for _ in range(n_top):
        mx = jnp.max(x, axis=0, keepdims=True)
        tops.append(mx)
        x = jnp.where(x == mx, -jnp.inf, x)
    return tops


def _extract_top_128(x, n_top):
    sub = x.shape[0] // 16
    rows = [x[i * sub:(i + 1) * sub, :] for i in range(16)]
    for a, b in _sort16_network():
        rows[a], rows[b] = jnp.maximum(rows[a], rows[b]), jnp.minimum(rows[a], rows[b])
    tops = []
    for k in range(n_top):
        mx = jnp.max(rows[0], axis=0, keepdims=True)
        tops.append(mx)
        need = min(16, n_top - 1 - k)
        if need:
            hit = rows[0] == mx
            below = rows[1:need + 1] + ([jnp.full_like(rows[0], -jnp.inf)] if need == 16 else [])
            for d in range(need):
                rows[d] = jnp.where(hit, below[d], rows[d])
    return tops


def _peer_kernel(h2_ref, x_ref, g2_ref, fg_ref, wq_ref, kbd_ref, *rest, final_norm, n_split):
    u_refs, vt_refs = rest[:n_split], rest[n_split:2 * n_split]
    o_ref, st_ref, ns_ref, e1_ref, e2_ref, r2_ref, acc_ref, at_ref, ga_ref, v1_ref, v2_ref = rest[2 * n_split:]
    j = pl.program_id(1)
    tm = h2_ref.shape[0]
    u_rows = u_refs[0].shape[0]
    v_rows = vt_refs[0].shape[0]
    te = u_rows * n_split
    n1 = te // PEER_N_KEYS
    nk = PEER_N_KEYS
    h2 = h2_ref[...]

    @pl.when(j == 0)
    def _():
        q = _dot(h2, wq_ref[...]).astype(BF16)
        st_ref[:, 0:tm] = _dot_nt(kbd_ref[...], q)

        def per_head(h, carry):
            base = pl.multiple_of(h * 2 * nk, 2 * nk)
            for lc in range(tm // LANES):
                ls = slice(lc * LANES, (lc + 1) * LANES)
                t1 = _extract_top_128(st_ref[pl.ds(base, nk), ls], PEER_TOPK + 1)
                t2 = _extract_top_128(st_ref[pl.ds(base + nk, nk), ls], PEER_TOPK + 1)
                for k in range(PEER_TOPK):
                    v1_ref[k:k + 1, :] = t1[k]
                    v2_ref[k:k + 1, :] = t2[k]
                v2_all = v2_ref[0:PEER_TOPK, :]
                v2_top = v2_ref[0:8, :]
                cand = [t1[0] + v2_all]
                cand += [t1[a] + v2_top for a in range(1, 8)]
                cand.append(v1_ref[8:PEER_TOPK, :] + t2[0])
                top = _extract_top(jnp.concatenate(cand, axis=0), PEER_TOPK + 1)
                z = jnp.zeros_like(top[0])
                for k in range(PEER_TOPK):
                    z = z + jnp.exp(top[k] - top[0])
                nxt = jnp.maximum(top[PEER_TOPK], jnp.maximum(t1[PEER_TOPK] + t2[0], t1[0] + t2[PEER_TOPK]))
                thr = 0.5 * (top[PEER_TOPK - 1] + nxt)
                s1 = st_ref[pl.ds(base, nk), ls]
                s2 = st_ref[pl.ds(base + nk, nk), ls]
                e1_ref[lc, h] = jnp.exp(s1 - t1[0]) * (0.5 / z)
                e2_ref[lc, h] = jnp.exp(s2 - t2[0]).astype(BF16)
                rank2 = jnp.zeros((nk, LANES), F32)
                for b in range(PEER_TOPK):
                    rank2 = rank2 + jnp.where(s2 < t2[b], 1.0, 0.0)
                r2_ref[lc, h] = rank2.astype(BF16)
                n_sel = jnp.zeros((nk, LANES), F32)
                for a in range(PEER_TOPK):
                    n_a = jnp.zeros_like(thr)
                    for b in range(PEER_TOPK // (a + 1)):
                        n_a = n_a + jnp.where(t1[a] + t2[b] >= thr, 1.0, 0.0)
                    n_sel = jnp.where(s1 == t1[a], n_a, n_sel)
                ns_ref[lc, h] = n_sel
            return carry

        lax.fori_loop(0, PEER_HEADS, per_head, 0)
        acc_ref[...] = jnp.zeros_like(acc_ref)
        ga_ref[1, :, 0:tm] = jnp.zeros((te, tm), BF16)
        for k in range(n_split):
            at_ref[0, k * u_rows:(k + 1) * u_rows, 0:tm] = _dot_nt(u_refs[k][...], h2)

    n_blocks = pl.num_programs(1) - 2
    rd = j % 2
    wr = 1 - rd

    @pl.when((j >= 1) & (j <= n_blocks))
    def _():
        for k in range(n_split):
            at_ref[rd, k * u_rows:(k + 1) * u_rows, 0:tm] = _dot_nt(u_refs[k][...], h2)
        for k in range(n_split):
            acc_ref[k * v_rows:(k + 1) * v_rows, :] += _dot(vt_refs[k][...], ga_ref[rd, :, 0:tm])
        r0 = pl.multiple_of((j - 1) * n1, n1)
        for lc in range(tm // LANES):
            ls = slice(lc * LANES, (lc + 1) * LANES)
            for il in range(n1):
                g = jnp.zeros((nk, LANES), BF16)
                for h in range(PEER_HEADS):
                    n_row = ns_ref[lc, h, pl.ds(r0, n1), :][il:il + 1, :].astype(BF16)
                    e_row = e1_ref[lc, h, pl.ds(r0, n1), :][il:il + 1, :].astype(BF16)
                    picked = jnp.where(r2_ref[lc, h] < n_row, e2_ref[lc, h], jnp.zeros((), BF16))
                    g = g + picked * e_row
                a = at_ref[wr, il * nk:(il + 1) * nk, ls]
                inner = a * (0.7978845608028654 + 0.035677408136300125 * (a * a))
                act2 = a + a * jnp.tanh(inner)
                ga_ref[wr, il * nk:(il + 1) * nk, ls] = g * act2.astype(BF16)

    @pl.when(j == n_blocks + 1)
    def _():
        for k in range(n_split):
            acc_ref[k * v_rows:(k + 1) * v_rows, :] += _dot(vt_refs[k][...], ga_ref[rd, :, 0:tm])
        out = x_ref[...] + g2_ref[0] * acc_ref[...].T
        if final_norm:
            ms = jnp.mean(out * out, axis=-1, keepdims=True)
            out = out * lax.rsqrt(ms + RMS_EPS) * fg_ref[...]
        o_ref[...] = out


def _peer(h2, x2, seq, g2, fg, wq, kbd, u_bf, vt_bf, final_norm, tm=512, te=1024, n_split=4):
    n, d = x2.shape
    n_blk = u_bf.shape[0] // te
    tpb = seq // tm
    trow = lambda i, j: (i, 0)
    c2 = lambda i, j: (0, 0)
    hs = (tm // LANES, PEER_HEADS, PEER_N_KEYS, LANES)
    tmp = tm + PEER_LANE_PAD
    u_map = lambda k, i, j: (jnp.minimum(j, n_blk - 1) * n_split + k, 0)
    v_map = lambda k, i, j: (k, jnp.clip(j - 2, 0, n_blk - 1))
    return pl.pallas_call(
        functools.partial(_peer_kernel, final_norm=final_norm, n_split=n_split),
        grid=(n // tm, n_blk + 2),
        in_specs=[pl.BlockSpec((tm, d), trow),
                  pl.BlockSpec((tm, d), trow),
                  pl.BlockSpec((1, 1, d), lambda i, j: (i // tpb, 0, 0)),
                  pl.BlockSpec((1, d), c2),
                  pl.BlockSpec((d, PEER_HEADS * 2 * PEER_HALF), c2),
                  pl.BlockSpec((PEER_HEADS * 2 * PEER_N_KEYS, PEER_HEADS * 2 * PEER_HALF), c2)]
        + [pl.BlockSpec((te // n_split, d), functools.partial(u_map, k)) for k in range(n_split)]
        + [pl.BlockSpec((d // n_split, te), functools.partial(v_map, k)) for k in range(n_split)],
        out_specs=pl.BlockSpec((tm, d), trow),
        out_shape=jax.ShapeDtypeStruct((n, d), F32),
        scratch_shapes=[pltpu.VMEM((PEER_HEADS * 2 * PEER_N_KEYS, tmp), F32),
                        pltpu.VMEM(hs, F32), pltpu.VMEM(hs, F32), pltpu.VMEM(hs, BF16), pltpu.VMEM(hs, BF16),
                        pltpu.VMEM((d, tm), F32),
                        pltpu.VMEM((2, te, tmp), F32),
                        pltpu.VMEM((2, te, tmp), BF16),
                        pltpu.VMEM((PEER_TOPK, LANES), F32), pltpu.VMEM((PEER_TOPK, LANES), F32)],
        compiler_params=_cparams("parallel", "arbitrary"),
        name="peer_dense",
    )(h2, x2, g2, fg, wq, kbd, *([u_bf] * n_split), *([vt_bf] * n_split))


def _peer_key_matrix(k1, k2):
    blocks = []
    for h in range(PEER_HEADS):
        for half, kk in enumerate((k1, k2)):
            col = (2 * h + half) * PEER_HALF
            blocks.append(jnp.pad(kk, ((0, 0), (col, PEER_HEADS * 2 * PEER_HALF - col - PEER_HALF))))
    return jnp.concatenate(blocks, axis=0).astype(BF16)


def kernel(x, c, norm1_g, norm2_g, w_mod, b_mod, w_in, ssm_a_re_log, ssm_a_im, ssm_b_re, ssm_b_im, ssm_c_re, ssm_c_im, ssm_d, ssm_log_dt, ssm_w_glu, ssm_b_glu, gmlp_w_sp, gmlp_b_sp, kv_norm_g, w_uk, w_uv, w_out, peer_w_q, peer_k1, peer_k2, peer_u, peer_v, final_g):
    bsz, seq, d = x.shape
    depth = w_mod.shape[0]
    x2 = x.reshape(bsz * seq, d)
    head_of = jnp.arange(GMLP_WIDTH) // GMLP_HEAD_DIM
    pavg = ((head_of[:, None] == head_of[None, :]).astype(F32) / GMLP_HEAD_DIM).astype(BF16)
    chunk_of = jnp.arange(GMLP_BLOCK) // CHUNK
    sp_mask = chunk_of[:, None] >= chunk_of[None, :]
    for l in range(depth):
        mod = _modulation(c, w_mod[l], b_mod[l])
        sh1, sc1, g1, sh2, sc2, g2 = [mod[:, i * d:(i + 1) * d].reshape(bsz, 1, d) for i in range(6)]
        w_in_pad = jnp.pad(w_in[l], ((0, 0), (0, IN_PAD - IN_WIDTH))).astype(BF16)
        wsp = jnp.where(sp_mask[None], gmlp_w_sp[l], 0.0).astype(BF16)
        bsp = jnp.repeat(gmlp_b_sp[l].T, GMLP_HEAD_DIM, axis=1)
        ua, yb, q, ckv, qi, ki, wi, ckvt = _input_stage(
            x2, seq, norm1_g[l].reshape(1, d), sc1, sh1, w_in_pad, kv_norm_g[l].reshape(1, KV_RANK),
            pavg, wsp, bsp)
        prep = _ssm_prep_tm(ssm_a_re_log[l], ssm_a_im[l], ssm_b_re[l], ssm_b_im[l], ssm_c_re[l], ssm_c_im[l],
                            ssm_log_dt[l], SSM_T)
        ys = _ssm_scan_tm(ua, bsz, seq, prep, ssm_d[l].reshape(1, SSM_WIDTH))
        yc = _dsa(q, qi, wi, ki, ckv, ckvt, w_uk[l], w_uv[l], bsz, seq)
        x2, h2 = _output_stage(
            ys, yb, yc, x2, seq, ssm_w_glu[l].astype(BF16),
            ssm_b_glu[l].reshape(1, SSM_WIDTH), w_out[l].astype(BF16), g1, norm2_g[l].reshape(1, d), sc2, sh2)
        x2 = _peer(h2, x2, seq, g2, final_g.reshape(1, d), peer_w_q[l].astype(BF16),
                   _peer_key_matrix(peer_k1[l], peer_k2[l]), peer_u[l].astype(BF16),
                   peer_v[l].T.astype(BF16), final_norm=(l == depth - 1))
    return x2.reshape(bsz, seq, d)
```

```python
import functools
import math

import jax
import jax.numpy as jnp
from jax import lax
from jax.experimental import pallas as pl
from jax.experimental.pallas import tpu as pltpu

CHUNK = 64
RMS_EPS = 1e-6

SSM_GROUP = 16
SSM_STATE = 64
SSM_WIDTH = 256
SSM_GROUPS = SSM_WIDTH // SSM_GROUP
SSM_T = 16
GMLP_HEADS = 4
GMLP_HEAD_DIM = 64
GMLP_WIDTH = GMLP_HEADS * GMLP_HEAD_DIM
GMLP_BLOCK = 128
ATT_HEADS = 8
ATT_HEAD_DIM = 64
ATT_WIDTH = ATT_HEADS * ATT_HEAD_DIM
KV_RANK = 128
IDX_HEADS = 4
IDX_DIM = 64
TOPK_MAX = 256

IN_SIZES = (SSM_WIDTH, GMLP_WIDTH, GMLP_WIDTH, ATT_WIDTH, KV_RANK, IDX_HEADS * IDX_DIM, IDX_DIM, IDX_HEADS)
IN_WIDTH = sum(IN_SIZES)
IN_PAD = 1792
COL_UB, COL_VB, COL_Q, COL_CKV, COL_QI, COL_KW = 256, 512, 768, 1280, 1408, 1664

PEER_HEADS = 8
PEER_N_KEYS = 128
PEER_HALF = 64
PEER_TOPK = 16

LANES = 128
PEER_LANE_PAD = LANES
VMEM_LIMIT = 56 * 1024 * 1024
NEG = -1e30
INT_MIN = -(2 ** 31)

F32 = jnp.float32
BF16 = jnp.bfloat16


def _cparams(*sem):
    return pltpu.CompilerParams(dimension_semantics=sem, vmem_limit_bytes=VMEM_LIMIT)


def _dot(a, b):
    return jnp.dot(a, b, preferred_element_type=F32)


def _dot_nt(a, b):
    return lax.dot_general(a, b, (((1,), (1,)), ((), ())), preferred_element_type=F32)


def _mod_kernel(c_ref, w_ref, b_ref, o_ref):
    c = c_ref[...]
    ca = c * jax.nn.sigmoid(c)
    o_ref[...] = _dot(ca.astype(BF16), w_ref[...].astype(BF16)) + b_ref[...]


def _modulation(c, w_mod, b_mod):
    bsz, d = c.shape
    n6 = w_mod.shape[1]
    return pl.pallas_call(
        _mod_kernel,
        grid=(n6 // d,),
        in_specs=[pl.BlockSpec((bsz, d), lambda j: (0, 0)),
                  pl.BlockSpec((d, d), lambda j: (0, j)),
                  pl.BlockSpec((1, d), lambda j: (0, j))],
        out_specs=pl.BlockSpec((bsz, d), lambda j: (0, j)),
        out_shape=jax.ShapeDtypeStruct((bsz, n6), F32),
        compiler_params=_cparams("parallel"),
        name="modulation",
    )(c, w_mod, b_mod.reshape(1, n6))


def _in_kernel(x_ref, g_ref, sc_ref, sh_ref, w_ref, kvg_ref, pavg_ref, wsp_ref, bsp_ref,
               ua_ref, yb_ref, q_ref, ckv_ref, qi_ref, ki_ref, wi_ref, ckvt_ref):
    tm = x_ref.shape[0]
    x = x_ref[...]
    ms = jnp.mean(x * x, axis=-1, keepdims=True)
    h = x * lax.rsqrt(ms + RMS_EPS) * g_ref[...]
    h = h * (1.0 + sc_ref[0]) + sh_ref[0]
    proj = _dot(h.astype(BF16), w_ref[...])

    ua_ref[...] = proj[:, 0:COL_UB]
    q_ref[...] = proj[:, COL_Q:COL_CKV].astype(BF16)
    ckv = proj[:, COL_CKV:COL_QI]
    ckv_ms = jnp.mean(ckv * ckv, axis=-1, keepdims=True)
    ckv_n = ckv * lax.rsqrt(ckv_ms + RMS_EPS) * kvg_ref[...]
    ckv_ref[...] = ckv_n.astype(BF16)
    ckvt_ref[...] = ckv_n.T.astype(BF16)
    qi_ref[...] = proj[:, COL_QI:COL_KW].astype(BF16)
    kw = proj[:, COL_KW:IN_PAD]
    ki_ref[...] = kw[:, 0:IDX_DIM].astype(BF16)
    wi_ref[...] = kw

    u_b = proj[:, COL_UB:COL_VB]
    v_b = proj[:, COL_VB:COL_Q]
    pavg = pavg_ref[...]

    def head_mean(a):
        hi = a.astype(BF16)
        lo = (a - hi.astype(F32)).astype(BF16)
        return _dot(hi, pavg) + _dot(lo, pavg)

    mu = head_mean(v_b)
    dv = v_b - mu
    var = head_mean(dv * dv)
    vn = (dv * lax.rsqrt(var + RMS_EPS)).astype(BF16)
    lane_head = lax.broadcasted_iota(jnp.int32, (GMLP_BLOCK, GMLP_WIDTH), 1) // GMLP_HEAD_DIM
    for blk in range(tm // GMLP_BLOCK):
        rows = slice(blk * GMLP_BLOCK, (blk + 1) * GMLP_BLOCK)
        vblk = vn[rows, :]
        mixed = bsp_ref[...]
        for hh in range(GMLP_HEADS):
            mixed = mixed + jnp.where(lane_head == hh, _dot(wsp_ref[hh], vblk), 0.0)
        yb_ref[rows, :] = (u_b[rows, :] * mixed).astype(BF16)


def _input_stage(x2, seq, g, sc, sh, w_in_pad, kvg, pavg, wsp, bsp, tm=512):
    n, d = x2.shape
    tpb = seq // tm
    row = lambda i: (i, 0)
    const2 = lambda i: (0, 0)
    per_b = lambda i: (i // tpb, 0, 0)
    outs = [(SSM_WIDTH, F32), (GMLP_WIDTH, BF16), (ATT_WIDTH, BF16), (KV_RANK, BF16),
            (IDX_HEADS * IDX_DIM, BF16), (IDX_DIM, BF16), (LANES, F32)]
    return pl.pallas_call(
        _in_kernel,
        grid=(n // tm,),
        in_specs=[pl.BlockSpec((tm, d), row),
                  pl.BlockSpec((1, d), const2),
                  pl.BlockSpec((1, 1, d), per_b),
                  pl.BlockSpec((1, 1, d), per_b),
                  pl.BlockSpec((d, IN_PAD), const2),
                  pl.BlockSpec((1, KV_RANK), const2),
                  pl.BlockSpec((GMLP_WIDTH, GMLP_WIDTH), const2),
                  pl.BlockSpec((GMLP_HEADS, GMLP_BLOCK, GMLP_BLOCK), lambda i: (0, 0, 0)),
                  pl.BlockSpec((GMLP_BLOCK, GMLP_WIDTH), const2)],
        out_specs=[pl.BlockSpec((tm, w), row) for w, _ in outs] + [pl.BlockSpec((KV_RANK, tm), lambda i: (0, i))],
        out_shape=[jax.ShapeDtypeStruct((n, w), dt) for w, dt in outs] + [jax.ShapeDtypeStruct((KV_RANK, n), BF16)],
        compiler_params=_cparams("parallel"),
        name="input_proj_gmlp",
    )(x2, g, sc, sh, w_in_pad, kvg, pavg, wsp, bsp)


def _ssm_prep_tm(a_re_log, a_im, b_re, b_im, c_re, c_im, log_dt, t_len):
    hp = lax.Precision.HIGHEST
    g_n, h_n, p_n = SSM_GROUPS, SSM_GROUP, SSM_STATE
    lam_re = -jnp.exp(a_re_log)
    lam_im = a_im
    dt = jnp.exp(log_dt)[:, None]
    mag = jnp.exp(lam_re * dt)
    abar_re = mag * jnp.cos(lam_im * dt)
    abar_im = mag * jnp.sin(lam_im * dt)
    den = lam_re * lam_re + lam_im * lam_im
    p = abar_re - 1.0
    qq = abar_im
    f_re = (p * lam_re + qq * lam_im) / den
    f_im = (qq * lam_re - p * lam_im) / den
    bb_re = f_re[..., None] * b_re - f_im[..., None] * b_im
    bb_im = f_re[..., None] * b_im + f_im[..., None] * b_re
    k = jnp.arange(t_len + 1, dtype=F32)[:, None, None]
    pmag = jnp.exp(k * (lam_re * dt))
    ang = k * (lam_im * dt)
    pw_re = pmag * jnp.cos(ang)
    pw_im = pmag * jnp.sin(ang)
    cp_re = c_re[None] * pw_re[:, :, None, :] - c_im[None] * pw_im[:, :, None, :]
    cp_im = c_re[None] * pw_im[:, :, None, :] + c_im[None] * pw_re[:, :, None, :]
    kern = (jnp.einsum('kghp,gpj->kghj', cp_re[:t_len], bb_re, precision=hp)
            - jnp.einsum('kghp,gpj->kghj', cp_im[:t_len], bb_im, precision=hp))
    grp_w = jnp.arange(g_n * h_n) // h_n
    grp_s = jnp.arange(g_n * p_n) // p_n
    kb = jnp.tile(kern.transpose(0, 1, 3, 2).reshape(t_len, g_n * h_n, h_n), (1, 1, g_n))
    kb = jnp.where((grp_w[:, None] == grp_w[None, :])[None], kb, 0.0)
    in_mask = grp_w[:, None] == grp_s[None, :]
    b_in = jnp.concatenate(
        [jnp.where(in_mask, jnp.tile(m.transpose(0, 2, 1).reshape(g_n * h_n, p_n), (1, g_n)), 0.0)
         for m in (bb_re, bb_im)], axis=1)
    c_out = jnp.concatenate(
        [jnp.where(in_mask.T, jnp.tile(m.transpose(0, 2, 1).reshape(g_n * p_n, h_n), (1, g_n)), 0.0)
         for m in (c_re, -c_im)], axis=0)
    tt = jnp.arange(t_len)
    flat = lambda a: a.reshape(a.shape[0], g_n * p_n)
    pw_end = jnp.concatenate([flat(pw_re[t_len - 1 - tt]), flat(pw_im[t_len - 1 - tt])], axis=1)
    pw_nxt = jnp.concatenate([flat(pw_re[1 + tt]), flat(pw_im[1 + tt])], axis=1)
    a_t = jnp.concatenate([flat(pw_re[t_len:]), flat(pw_im[t_len:])], axis=1)
    return kb.astype(BF16), b_in.astype(BF16), c_out.astype(BF16), pw_end, pw_nxt, a_t


def _ssm_tm_kernel(u_ref, kb_ref, bin_ref, cout_ref, pwe_ref, pwn_ref, at_ref, d_ref, y_ref,
                   ubuf_ref, sp_ref, st_ref, *, t_len):
    tm, width = u_ref.shape
    n_st = at_ref.shape[1] // 2
    nch = tm // t_len

    @pl.when(pl.program_id(1) == 0)
    def _():
        st_ref[...] = jnp.zeros_like(st_ref)

    u = u_ref[...]
    ubuf_ref[0:t_len, :] = jnp.zeros((t_len, width), F32)
    ubuf_ref[t_len:, :] = u
    pos = lax.broadcasted_iota(jnp.int32, (tm, 1), 0) % t_len
    y = _dot(u.astype(BF16), kb_ref[0])
    for k in range(1, t_len):
        shifted = ubuf_ref[t_len - k:t_len - k + tm, :]
        y = y + _dot(jnp.where(pos >= k, shifted, 0.0).astype(BF16), kb_ref[k])

    bu = _dot(u.astype(BF16), bin_ref[...]).reshape(nch, t_len, 2 * n_st)
    b_re, b_im = bu[:, :, :n_st], bu[:, :, n_st:]
    w_re, w_im = pwe_ref[:, :n_st][None], pwe_ref[:, n_st:][None]
    loc_re = jnp.sum(b_re * w_re - b_im * w_im, axis=1)
    loc_im = jnp.sum(b_re * w_im + b_im * w_re, axis=1)

    a_re, a_im = at_ref[:, :n_st], at_ref[:, n_st:]
    s_re, s_im = st_ref[:, :n_st], st_ref[:, n_st:]
    for c in range(nch):
        sp_ref[c:c + 1, :n_st] = s_re
        sp_ref[c:c + 1, n_st:] = s_im
        s_re, s_im = (a_re * s_re - a_im * s_im + loc_re[c:c + 1, :],
                      a_re * s_im + a_im * s_re + loc_im[c:c + 1, :])
    st_ref[:, :n_st] = s_re
    st_ref[:, n_st:] = s_im

    p_re, p_im = sp_ref[:, :n_st][:, None, :], sp_ref[:, n_st:][:, None, :]
    n_re, n_im = pwn_ref[:, :n_st][None], pwn_ref[:, n_st:][None]
    e_re = (p_re * n_re - p_im * n_im).reshape(tm, n_st).astype(BF16)
    e_im = (p_re * n_im + p_im * n_re).reshape(tm, n_st).astype(BF16)
    y = y + _dot(e_re, cout_ref[:n_st, :]) + _dot(e_im, cout_ref[n_st:, :])
    y_ref[...] = y + d_ref[...] * u


def _ssm_scan_tm(u_a, bsz, seq, prep, d_skip, tm=512):
    kb, b_in, c_out, pw_end, pw_nxt, a_t = prep
    t_len = kb.shape[0]
    n, width = u_a.shape
    tpb = seq // tm
    row = lambda b, i: (b * tpb + i, 0)
    c2 = lambda b, i: (0, 0)
    n2 = b_in.shape[1]
    return pl.pallas_call(
        functools.partial(_ssm_tm_kernel, t_len=t_len),
        grid=(bsz, tpb),
        in_specs=[pl.BlockSpec((tm, width), row),
                  pl.BlockSpec((t_len, width, width), lambda b, i: (0, 0, 0)),
                  pl.BlockSpec((width, n2), c2),
                  pl.BlockSpec((n2, width), c2),
                  pl.BlockSpec((t_len, n2), c2),
                  pl.BlockSpec((t_len, n2), c2),
                  pl.BlockSpec((1, n2), c2),
                  pl.BlockSpec((1, width), c2)],
        out_specs=pl.BlockSpec((tm, width), row),
        out_shape=jax.ShapeDtypeStruct((n, width), F32),
        scratch_shapes=[pltpu.VMEM((t_len + tm, width), F32),
                        pltpu.VMEM((tm // t_len, n2), F32),
                        pltpu.VMEM((1, n2), F32)],
        compiler_params=_cparams("parallel", "arbitrary"),
        name="ssm_scan",
    )(u_a, kb, b_in, c_out, pw_end, pw_nxt, a_t, d_skip)


def _dsa_kernel(q_ref, qi_ref, wi_ref, ki_ref, ckv_ref, ckvt_ref, wuk_ref, wuvt_ref, tri_ref, o_ref,
                key_ref, dm_ref, qa_ref, ot_ref, *acc_refs, topk, kb_len):
    tq = q_ref.shape[0]
    t0 = pl.program_id(1) * tq
    n_kb = (t0 + tq + kb_len - 1) // kb_len
    q_pos = t0 + lax.broadcasted_iota(jnp.int32, (1, tq), 1)
    q_chunk = q_pos // CHUNK
    k_off = lax.broadcasted_iota(jnp.int32, (kb_len, 1), 0)

    def fold8(m, op):
        r = m.reshape(kb_len // 8, 8, tq)
        n = kb_len // 8
        while n > 1:
            n //= 2
            r = op(r[:n], r[n:2 * n])
        return r[0]

    def col_sum(m):
        return fold8(m, jnp.add)

    wi_t = wi_ref[...].T
    qi_h = [qi_ref[:, h * IDX_DIM:(h + 1) * IDX_DIM] for h in range(IDX_HEADS)]
    w_h = [wi_t[IDX_DIM + h:IDX_DIM + h + 1, :] * ((IDX_HEADS ** -0.5) * (IDX_DIM ** -0.5))
           for h in range(IDX_HEADS)]

    def score_block(kb, carry):
        k0 = pl.multiple_of(kb * kb_len, kb_len)
        ki = ki_ref[pl.ds(k0, kb_len), :]
        score = jnp.zeros((kb_len, tq), F32)
        for h in range(IDX_HEADS):
            score = score + jnp.maximum(_dot_nt(ki, qi_h[h]), 0.0) * w_h[h]
        score = jnp.where(score == 0.0, 0.0, score)
        bits = lax.bitcast_convert_type(score, jnp.int32)
        key = jnp.where(bits < 0, bits ^ jnp.int32(0x7FFFFFFF), bits)
        adm = ((k0 + k_off) // CHUNK) <= q_chunk
        key_ref[pl.ds(k0, kb_len), :] = jnp.where(adm, key, jnp.int32(INT_MIN))
        return carry

    lax.fori_loop(0, n_kb, score_block, 0)

    def count(pred):
        def body(kb, acc):
            k0 = pl.multiple_of(kb * kb_len, kb_len)
            return acc + col_sum(pred(key_ref[pl.ds(k0, kb_len), :]).astype(F32))
        acc = lax.fori_loop(0, n_kb, body, jnp.zeros((8, tq), F32))
        return jnp.sum(acc, axis=0, keepdims=True)

    kf = jnp.float32(topk)

    def bit_step(step, ans):
        trial = ans + jnp.left_shift(jnp.int32(1), 31 - step)
        return jnp.where(count(lambda kblk: kblk >= trial) >= kf, trial, ans)

    ans = lax.fori_loop(0, 32, bit_step, jnp.full((1, tq), INT_MIN, jnp.int32))

    room = kf - count(lambda kblk: kblk > ans)
    tri = tri_ref[...]

    def select_block(kb, carry):
        k0 = pl.multiple_of(kb * kb_len, kb_len)
        kblk = key_ref[pl.ds(k0, kb_len), :]
        eq = kblk == ans
        pre = _dot(tri, eq.astype(BF16)) + carry
        sel = (kblk > ans) | (eq & (pre <= room))
        k_pos = k0 + k_off
        adm = (k_pos // CHUNK) <= q_chunk
        dist = jnp.abs(q_pos - k_pos).astype(F32)
        dm_ref[pl.ds(k0, kb_len), :] = jnp.where(sel & adm, dist, -NEG)
        return carry + jnp.sum(col_sum(eq.astype(F32)), axis=0, keepdims=True)

    lax.fori_loop(0, n_kb, select_block, jnp.zeros((1, tq), F32))

    for h in range(ATT_HEADS):
        hs = slice(h * ATT_HEAD_DIM, (h + 1) * ATT_HEAD_DIM)
        qa_ref[h] = (_dot(q_ref[:, hs], wuk_ref[h]) * (ATT_HEAD_DIM ** -0.5)).astype(BF16)
        acc_refs[h][...] = jnp.zeros((KV_RANK, tq), F32)

    def att_block(kb, carry):
        k0 = pl.multiple_of(kb * kb_len, kb_len)
        ckv = ckv_ref[pl.ds(k0, kb_len), :]
        ckvt = ckvt_ref[:, pl.ds(k0, kb_len)]
        dm = dm_ref[pl.ds(k0, kb_len), :]
        new = []
        qk = _dot_nt(ckv, qa_ref[0])
        for h in range(ATT_HEADS):
            m_run, l_run = carry[2 * h], carry[2 * h + 1]
            slope = 2.0 ** (-8.0 * (h + 1) / ATT_HEADS)
            lg = qk - slope * dm
            if h + 1 < ATT_HEADS:
                qk = _dot_nt(ckv, qa_ref[h + 1])
            m_new = jnp.maximum(m_run, jnp.max(fold8(lg, jnp.maximum), axis=0, keepdims=True))
            alpha = jnp.exp(m_run - m_new)
            p = jnp.exp(lg - m_new)
            new += [m_new, alpha * l_run + jnp.sum(col_sum(p), axis=0, keepdims=True)]
            acc_refs[h][...] = alpha * acc_refs[h][...] + _dot(ckvt, p.astype(BF16))
        return tuple(new)

    init = tuple(jnp.full((1, tq), -jnp.inf if i % 2 == 0 else 0.0, F32) for i in range(2 * ATT_HEADS))
    fin = lax.fori_loop(0, n_kb, att_block, init)
    for h in range(ATT_HEADS):
        o_t = (acc_refs[h][...] / fin[2 * h + 1]).astype(BF16)
        ot_ref[h * ATT_HEAD_DIM:(h + 1) * ATT_HEAD_DIM, :] = _dot(wuvt_ref[h], o_t)
    o_ref[...] = ot_ref[...].T.astype(o_ref.dtype)


def _dsa(q, qi, wi, ki, ckv, ckvt, w_uk, w_uv, bsz, seq, tq=256, kb_len=512):
    n = q.shape[0]
    nqt = seq // tq
    kb_len = min(kb_len, seq)
    topk = min(TOPK_MAX, seq // 4)
    tri = (jnp.arange(kb_len)[:, None] >= jnp.arange(kb_len)[None, :]).astype(BF16)
    qrow = lambda b, i: (b * nqt + i, 0)
    krow = lambda b, i: (b, 0)
    c3 = lambda b, i: (0, 0, 0)
    return pl.pallas_call(
        functools.partial(_dsa_kernel, topk=topk, kb_len=kb_len),
        grid=(bsz, nqt),
        in_specs=[pl.BlockSpec((tq, ATT_WIDTH), qrow),
                  pl.BlockSpec((tq, IDX_HEADS * IDX_DIM), qrow),
                  pl.BlockSpec((tq, LANES), qrow),
                  pl.BlockSpec((seq, IDX_DIM), krow),
                  pl.BlockSpec((seq, KV_RANK), krow),
                  pl.BlockSpec((KV_RANK, seq), lambda b, i: (0, b)),
                  pl.BlockSpec((ATT_HEADS, ATT_HEAD_DIM, KV_RANK), c3),
                  pl.BlockSpec((ATT_HEADS, ATT_HEAD_DIM, KV_RANK), c3),
                  pl.BlockSpec((kb_len, kb_len), lambda b, i: (0, 0))],
        out_specs=pl.BlockSpec((tq, ATT_WIDTH), qrow),
        out_shape=jax.ShapeDtypeStruct((n, ATT_WIDTH), BF16),
        scratch_shapes=[pltpu.VMEM((seq, tq), jnp.int32),
                        pltpu.VMEM((seq, tq), F32),
                        pltpu.VMEM((ATT_HEADS, tq, KV_RANK), BF16),
                        pltpu.VMEM((ATT_WIDTH, tq), F32)]
        + [pltpu.VMEM((KV_RANK, tq), F32) for _ in range(ATT_HEADS)],
        compiler_params=_cparams("parallel", "parallel"),
        name="dsa_attention",
    )(q, qi, wi, ki, ckv, ckvt, w_uk.astype(BF16), jnp.swapaxes(w_uv, 1, 2).astype(BF16), tri)


def _out_kernel(ys_ref, yb_ref, yc_ref, x_ref, wglu_ref, bglu_ref, wo_ref, g1_ref,
                n2_ref, sc_ref, sh_ref, xo_ref, h2_ref):
    yg = jax.nn.gelu(ys_ref[...])
    z = _dot(yg.astype(BF16), wglu_ref[...]) + bglu_ref[...]
    ya = yg * jax.nn.sigmoid(z)
    a_w, b_w = SSM_WIDTH, SSM_WIDTH + GMLP_WIDTH
    mix = (_dot(ya.astype(BF16), wo_ref[0:a_w, :])
           + _dot(yb_ref[...], wo_ref[a_w:b_w, :])
           + _dot(yc_ref[...], wo_ref[b_w:, :]))
    xn = x_ref[...] + g1_ref[0] * mix
    xo_ref[...] = xn
    ms = jnp.mean(xn * xn, axis=-1, keepdims=True)
    h = xn * lax.rsqrt(ms + RMS_EPS) * n2_ref[...]
    h2_ref[...] = (h * (1.0 + sc_ref[0]) + sh_ref[0]).astype(BF16)


def _output_stage(ys, yb, yc, x2, seq, w_glu, b_glu, w_out, g1, n2g, sc2, sh2, tm=512):
    n, d = x2.shape
    tpb = seq // tm
    row = lambda i: (i, 0)
    const2 = lambda i: (0, 0)
    per_b = lambda i: (i // tpb, 0, 0)
    return pl.pallas_call(
        _out_kernel,
        grid=(n // tm,),
        in_specs=[pl.BlockSpec((tm, SSM_WIDTH), row),
                  pl.BlockSpec((tm, GMLP_WIDTH), row),
                  pl.BlockSpec((tm, ATT_WIDTH), row),
                  pl.BlockSpec((tm, d), row),
                  pl.BlockSpec((SSM_WIDTH, SSM_WIDTH), const2),
                  pl.BlockSpec((1, SSM_WIDTH), const2),
                  pl.BlockSpec((d, d), const2),
                  pl.BlockSpec((1, 1, d), per_b),
                  pl.BlockSpec((1, d), const2),
                  pl.BlockSpec((1, 1, d), per_b),
                  pl.BlockSpec((1, 1, d), per_b)],
        out_specs=[pl.BlockSpec((tm, d), row), pl.BlockSpec((tm, d), row)],
        out_shape=[jax.ShapeDtypeStruct((n, d), F32), jax.ShapeDtypeStruct((n, d), BF16)],
        compiler_params=_cparams("parallel"),
        name="out_proj_norm2",
    )(ys, yb, yc, x2, w_glu, b_glu, w_out, g1, n2g, sc2, sh2)


def _sort16_network():
    def merge(lo, hi, r):
        step = r * 2
        if step < hi - lo:
            yield from merge(lo, hi, step)
            yield from merge(lo + r, hi, step)
            yield from [(i, i + r) for i in range(lo + r, hi - r, step)]
        else:
            yield (lo, lo + r)

    def sort(lo, hi):
        if hi - lo >= 1:
            mid = lo + (hi - lo) // 2
            yield from sort(lo, mid)
            yield from sort(mid + 1, hi)
            yield from merge(lo, hi, 1)

    return tuple(sort(0, 15))


def _extract_top(x, n_top):
    tops = []
    for _ in range(n_top):
        mx = jnp.max(x, axis=0, keepdims=True)
        tops.append(mx)
        x = jnp.where(x == mx, -jnp.inf, x)
    return tops


def _extract_top_128(x, n_top):
    sub = x.shape[0] // 16
    rows = [x[i * sub:(i + 1) * sub, :] for i in range(16)]
    for a, b in _sort16_network():
        rows[a], rows[b] = jnp.maximum(rows[a], rows[b]), jnp.minimum(rows[a], rows[b])
    tops = []
    for k in range(n_top):
        mx = jnp.max(rows[0], axis=0, keepdims=True)
        tops.append(mx)
        need = min(16, n_top - 1 - k)
        if need:
            hit = rows[0] == mx
            below = rows[1:need + 1] + ([jnp.full_like(rows[0], -jnp.inf)] if need == 16 else [])
            for d in range(need):
                rows[d] = jnp.where(hit, below[d], rows[d])
    return tops


def _peer_kernel(h2_ref, x_ref, g2_ref, fg_ref, wq_ref, kbd_ref, *rest, final_norm, n_split):
    u_refs, vt_refs = rest[:n_split], rest[n_split:2 * n_split]
    o_ref, st_ref, ns_ref, e1_ref, e2_ref, r2_ref, acc_ref, at_ref, ga_ref, v1_ref, v2_ref = rest[2 * n_split:]
    j = pl.program_id(1)
    tm = h2_ref.shape[0]
    u_rows = u_refs[0].shape[0]
    v_rows = vt_refs[0].shape[0]
    te = u_rows * n_split
    n1 = te // PEER_N_KEYS
    nk = PEER_N_KEYS
    h2 = h2_ref[...]

    @pl.when(j == 0)
    def _():
        q = _dot(h2, wq_ref[...]).astype(BF16)
        st_ref[:, 0:tm] = _dot_nt(kbd_ref[...], q)

        def per_head(h, carry):
            base = pl.multiple_of(h * 2 * nk, 2 * nk)
            for lc in range(tm // LANES):
                ls = slice(lc * LANES, (lc + 1) * LANES)
                t1 = _extract_top_128(st_ref[pl.ds(base, nk), ls], PEER_TOPK + 1)
                t2 = _extract_top_128(st_ref[pl.ds(base + nk, nk), ls], PEER_TOPK + 1)
                for k in range(PEER_TOPK):
                    v1_ref[k:k + 1, :] = t1[k]
                    v2_ref[k:k + 1, :] = t2[k]
                v2_all = v2_ref[0:PEER_TOPK, :]
                v2_top = v2_ref[0:8, :]
                cand = [t1[0] + v2_all]
                cand += [t1[a] + v2_top for a in range(1, 8)]
                cand.append(v1_ref[8:PEER_TOPK, :] + t2[0])
                top = _extract_top(jnp.concatenate(cand, axis=0), PEER_TOPK + 1)
                z = jnp.zeros_like(top[0])
                for k in range(PEER_TOPK):
                    z = z + jnp.exp(top[k] - top[0])
                nxt = jnp.maximum(top[PEER_TOPK], jnp.maximum(t1[PEER_TOPK] + t2[0], t1[0] + t2[PEER_TOPK]))
                thr = 0.5 * (top[PEER_TOPK - 1] + nxt)
                s1 = st_ref[pl.ds(base, nk), ls]
                s2 = st_ref[pl.ds(base + nk, nk), ls]
                e1_ref[lc, h] = jnp.exp(s1 - t1[0]) * (0.5 / z)
                e2_ref[lc, h] = jnp.exp(s2 - t2[0]).astype(BF16)
                rank2 = jnp.zeros((nk, LANES), F32)
                for b in range(PEER_TOPK):
                    rank2 = rank2 + jnp.where(s2 < t2[b], 1.0, 0.0)
                r2_ref[lc, h] = rank2.astype(BF16)
                n_sel = jnp.zeros((nk, LANES), F32)
                for a in range(PEER_TOPK):
                    n_a = jnp.zeros_like(thr)
                    for b in range(PEER_TOPK // (a + 1)):
                        n_a = n_a + jnp.where(t1[a] + t2[b] >= thr, 1.0, 0.0)
                    n_sel = jnp.where(s1 == t1[a], n_a, n_sel)
                ns_ref[lc, h] = n_sel
            return carry

        lax.fori_loop(0, PEER_HEADS, per_head, 0)
        acc_ref[...] = jnp.zeros_like(acc_ref)
        ga_ref[1, :, 0:tm] = jnp.zeros((te, tm), BF16)
        for k in range(n_split):
            at_ref[0, k * u_rows:(k + 1) * u_rows, 0:tm] = _dot_nt(u_refs[k][...], h2)

    n_blocks = pl.num_programs(1) - 2
    rd = j % 2
    wr = 1 - rd

    @pl.when((j >= 1) & (j <= n_blocks))
    def _():
        for k in range(n_split):
            at_ref[rd, k * u_rows:(k + 1) * u_rows, 0:tm] = _dot_nt(u_refs[k][...], h2)
        for k in range(n_split):
            acc_ref[k * v_rows:(k + 1) * v_rows, :] += _dot(vt_refs[k][...], ga_ref[rd, :, 0:tm])
        r0 = pl.multiple_of((j - 1) * n1, n1)
        il_group = 4
        for lc in range(tm // LANES):
            ls = slice(lc * LANES, (lc + 1) * LANES)
            for il0 in range(0, n1, il_group):
                ils = range(il0, il0 + il_group)
                g = {il: jnp.zeros((nk, LANES), BF16) for il in ils}
                for h in range(PEER_HEADS):
                    n_rows = ns_ref[lc, h, pl.ds(r0, n1), :]
                    e_rows = e1_ref[lc, h, pl.ds(r0, n1), :]
                    r2 = r2_ref[lc, h]
                    e2 = e2_ref[lc, h]
                    for il in ils:
                        picked = jnp.where(r2 < n_rows[il:il + 1, :].astype(BF16), e2, jnp.zeros((), BF16))
                        g[il] = g[il] + picked * e_rows[il:il + 1, :].astype(BF16)
                for il in ils:
                    a = at_ref[wr, il * nk:(il + 1) * nk, ls]
                    inner = a * (0.7978845608028654 + 0.035677408136300125 * (a * a))
                    act2 = a + a * jnp.tanh(inner)
                    ga_ref[wr, il * nk:(il + 1) * nk, ls] = g[il] * act2.astype(BF16)

    @pl.when(j == n_blocks + 1)
    def _():
        for k in range(n_split):
            acc_ref[k * v_rows:(k + 1) * v_rows, :] += _dot(vt_refs[k][...], ga_ref[rd, :, 0:tm])
        out = x_ref[...] + g2_ref[0] * acc_ref[...].T
        if final_norm:
            ms = jnp.mean(out * out, axis=-1, keepdims=True)
            out = out * lax.rsqrt(ms + RMS_EPS) * fg_ref[...]
        o_ref[...] = out


def _peer(h2, x2, seq, g2, fg, wq, kbd, u_bf, vt_bf, final_norm, tm=512, te=1024, n_split=4):
    n, d = x2.shape
    n_blk = u_bf.shape[0] // te
    tpb = seq // tm
    trow = lambda i, j: (i, 0)
    c2 = lambda i, j: (0, 0)
    hs = (tm // LANES, PEER_HEADS, PEER_N_KEYS, LANES)
    tmp = tm + PEER_LANE_PAD
    u_map = lambda k, i, j: (jnp.minimum(j, n_blk - 1) * n_split + k, 0)
    v_map = lambda k, i, j: (k, jnp.clip(j - 2, 0, n_blk - 1))
    return pl.pallas_call(
        functools.partial(_peer_kernel, final_norm=final_norm, n_split=n_split),
        grid=(n // tm, n_blk + 2),
        in_specs=[pl.BlockSpec((tm, d), trow),
                  pl.BlockSpec((tm, d), trow),
                  pl.BlockSpec((1, 1, d), lambda i, j: (i // tpb, 0, 0)),
                  pl.BlockSpec((1, d), c2),
                  pl.BlockSpec((d, PEER_HEADS * 2 * PEER_HALF), c2),
                  pl.BlockSpec((PEER_HEADS * 2 * PEER_N_KEYS, PEER_HEADS * 2 * PEER_HALF), c2)]
        + [pl.BlockSpec((te // n_split, d), functools.partial(u_map, k)) for k in range(n_split)]
        + [pl.BlockSpec((d // n_split, te), functools.partial(v_map, k)) for k in range(n_split)],
        out_specs=pl.BlockSpec((tm, d), trow),
        out_shape=jax.ShapeDtypeStruct((n, d), F32),
        scratch_shapes=[pltpu.VMEM((PEER_HEADS * 2 * PEER_N_KEYS, tmp), F32),
                        pltpu.VMEM(hs, F32), pltpu.VMEM(hs, F32), pltpu.VMEM(hs, BF16), pltpu.VMEM(hs, BF16),
                        pltpu.VMEM((d, tm), F32),
                        pltpu.VMEM((2, te, tmp), F32),
                        pltpu.VMEM((2, te, tmp), BF16),
                        pltpu.VMEM((PEER_TOPK, LANES), F32), pltpu.VMEM((PEER_TOPK, LANES), F32)],
        compiler_params=_cparams("parallel", "arbitrary"),
        name="peer_dense",
    )(h2, x2, g2, fg, wq, kbd, *([u_bf] * n_split), *([vt_bf] * n_split))


def _peer_key_matrix(k1, k2):
    blocks = []
    for h in range(PEER_HEADS):
        for half, kk in enumerate((k1, k2)):
            col = (2 * h + half) * PEER_HALF
            blocks.append(jnp.pad(kk, ((0, 0), (col, PEER_HEADS * 2 * PEER_HALF - col - PEER_HALF))))
    return jnp.concatenate(blocks, axis=0).astype(BF16)


def kernel(x, c, norm1_g, norm2_g, w_mod, b_mod, w_in, ssm_a_re_log, ssm_a_im, ssm_b_re, ssm_b_im, ssm_c_re, ssm_c_im, ssm_d, ssm_log_dt, ssm_w_glu, ssm_b_glu, gmlp_w_sp, gmlp_b_sp, kv_norm_g, w_uk, w_uv, w_out, peer_w_q, peer_k1, peer_k2, peer_u, peer_v, final_g):
    bsz, seq, d = x.shape
    depth = w_mod.shape[0]
    x2 = x.reshape(bsz * seq, d)
    head_of = jnp.arange(GMLP_WIDTH) // GMLP_HEAD_DIM
    pavg = ((head_of[:, None] == head_of[None, :]).astype(F32) / GMLP_HEAD_DIM).astype(BF16)
    chunk_of = jnp.arange(GMLP_BLOCK) // CHUNK
    sp_mask = chunk_of[:, None] >= chunk_of[None, :]
    for l in range(depth):
        mod = _modulation(c, w_mod[l], b_mod[l])
        sh1, sc1, g1, sh2, sc2, g2 = [mod[:, i * d:(i + 1) * d].reshape(bsz, 1, d) for i in range(6)]
        w_in_pad = jnp.pad(w_in[l], ((0, 0), (0, IN_PAD - IN_WIDTH))).astype(BF16)
        wsp = jnp.where(sp_mask[None], gmlp_w_sp[l], 0.0).astype(BF16)
        bsp = jnp.repeat(gmlp_b_sp[l].T, GMLP_HEAD_DIM, axis=1)
        ua, yb, q, ckv, qi, ki, wi, ckvt = _input_stage(
            x2, seq, norm1_g[l].reshape(1, d), sc1, sh1, w_in_pad, kv_norm_g[l].reshape(1, KV_RANK),
            pavg, wsp, bsp)
        prep = _ssm_prep_tm(ssm_a_re_log[l], ssm_a_im[l], ssm_b_re[l], ssm_b_im[l], ssm_c_re[l], ssm_c_im[l],
                            ssm_log_dt[l], SSM_T)
        ys = _ssm_scan_tm(ua, bsz, seq, prep, ssm_d[l].reshape(1, SSM_WIDTH))
        yc = _dsa(q, qi, wi, ki, ckv, ckvt, w_uk[l], w_uv[l], bsz, seq)
        x2, h2 = _output_stage(
            ys, yb, yc, x2, seq, ssm_w_glu[l].astype(BF16),
            ssm_b_glu[l].reshape(1, SSM_WIDTH), w_out[l].astype(BF16), g1, norm2_g[l].reshape(1, d), sc2, sh2)
        x2 = _peer(h2, x2, seq, g2, final_g.reshape(1, d), peer_w_q[l].astype(BF16),
                   _peer_key_matrix(peer_k1[l], peer_k2[l]), peer_u[l].astype(BF16),
                   peer_v[l].T.astype(BF16), final_norm=(l == depth - 1))
    return x2.reshape(bsz, seq, d)
```

```python
import functools
import math

import jax
import jax.numpy as jnp
from jax import lax
from jax.experimental import pallas as pl
from jax.experimental.pallas import tpu as pltpu

CHUNK = 64
RMS_EPS = 1e-6

SSM_GROUP = 16
SSM_STATE = 64
SSM_WIDTH = 256
SSM_GROUPS = SSM_WIDTH // SSM_GROUP
SSM_T = 16
GMLP_HEADS = 4
GMLP_HEAD_DIM = 64
GMLP_WIDTH = GMLP_HEADS * GMLP_HEAD_DIM
GMLP_BLOCK = 128
ATT_HEADS = 8
ATT_HEAD_DIM = 64
ATT_WIDTH = ATT_HEADS * ATT_HEAD_DIM
KV_RANK = 128
KV_ROWS = KV_RANK + 8
IDX_HEADS = 4
IDX_DIM = 64
TOPK_MAX = 256

IN_SIZES = (SSM_WIDTH, GMLP_WIDTH, GMLP_WIDTH, ATT_WIDTH, KV_RANK, IDX_HEADS * IDX_DIM, IDX_DIM, IDX_HEADS)
IN_WIDTH = sum(IN_SIZES)
IN_PAD = 1792
COL_UB, COL_VB, COL_Q, COL_CKV, COL_QI, COL_KW = 256, 512, 768, 1280, 1408, 1664

PEER_HEADS = 8
PEER_N_KEYS = 128
PEER_HALF = 64
PEER_TOPK = 16

LANES = 128
VMEM_LIMIT = 56 * 1024 * 1024
NEG = -1e30
INT_MIN = -(2 ** 31)

F32 = jnp.float32
BF16 = jnp.bfloat16


def _cparams(*sem):
    return pltpu.CompilerParams(dimension_semantics=sem, vmem_limit_bytes=VMEM_LIMIT)


def _dot(a, b):
    return jnp.dot(a, b, preferred_element_type=F32)


def _dot_nt(a, b):
    return lax.dot_general(a, b, (((1,), (1,)), ((), ())), preferred_element_type=F32)


def _mod_kernel(c_ref, w_ref, b_ref, o_ref):
    c = c_ref[...]
    ca = c * jax.nn.sigmoid(c)
    o_ref[...] = _dot(ca.astype(BF16), w_ref[...].astype(BF16)) + b_ref[...]


def _modulation(c, w_mod, b_mod):
    bsz, d = c.shape
    n6 = w_mod.shape[1]
    return pl.pallas_call(
        _mod_kernel,
        grid=(n6 // d,),
        in_specs=[pl.BlockSpec((bsz, d), lambda j: (0, 0)),
                  pl.BlockSpec((d, d), lambda j: (0, j)),
                  pl.BlockSpec((1, d), lambda j: (0, j))],
        out_specs=pl.BlockSpec((bsz, d), lambda j: (0, j)),
        out_shape=jax.ShapeDtypeStruct((bsz, n6), F32),
        compiler_params=_cparams("parallel"),
        name="modulation",
    )(c, w_mod, b_mod.reshape(1, n6))


def _in_kernel(x_ref, g_ref, sc_ref, sh_ref, w_ref, kvg_ref, pavg_ref, wsp_ref, bsp_ref,
               ua_ref, yb_ref, q_ref, ckv_ref, qi_ref, ki_ref, wi_ref, ckvt_ref):
    tm = x_ref.shape[0]
    x = x_ref[...]
    ms = jnp.mean(x * x, axis=-1, keepdims=True)
    h = x * lax.rsqrt(ms + RMS_EPS) * g_ref[...]
    h = h * (1.0 + sc_ref[0]) + sh_ref[0]
    proj = _dot(h.astype(BF16), w_ref[...])

    ua_ref[...] = proj[:, 0:COL_UB]
    q_ref[...] = proj[:, COL_Q:COL_CKV].astype(BF16)
    ckv = proj[:, COL_CKV:COL_QI]
    ckv_ms = jnp.mean(ckv * ckv, axis=-1, keepdims=True)
    ckv_n = ckv * lax.rsqrt(ckv_ms + RMS_EPS) * kvg_ref[...]
    ckv_ref[...] = ckv_n.astype(BF16)
    ckvt_ref[0:KV_RANK, :] = ckv_n.T.astype(BF16)
    ones_row = lax.broadcasted_iota(jnp.int32, (KV_ROWS - KV_RANK, tm), 0) == 0
    ckvt_ref[KV_RANK:, :] = jnp.where(ones_row, 1.0, 0.0).astype(BF16)
    qi_ref[...] = proj[:, COL_QI:COL_KW].astype(BF16)
    kw = proj[:, COL_KW:IN_PAD]
    ki_ref[...] = kw[:, 0:IDX_DIM].astype(BF16)
    wi_ref[...] = kw

    u_b = proj[:, COL_UB:COL_VB]
    v_b = proj[:, COL_VB:COL_Q]
    pavg = pavg_ref[...]

    def head_mean(a):
        hi = a.astype(BF16)
        lo = (a - hi.astype(F32)).astype(BF16)
        return _dot(hi, pavg) + _dot(lo, pavg)

    mu = head_mean(v_b)
    dv = v_b - mu
    var = head_mean(dv * dv)
    vn = (dv * lax.rsqrt(var + RMS_EPS)).astype(BF16)
    lane_head = lax.broadcasted_iota(jnp.int32, (GMLP_BLOCK, GMLP_WIDTH), 1) // GMLP_HEAD_DIM
    for blk in range(tm // GMLP_BLOCK):
        rows = slice(blk * GMLP_BLOCK, (blk + 1) * GMLP_BLOCK)
        vblk = vn[rows, :]
        mixed = bsp_ref[...]
        for hh in range(GMLP_HEADS):
            mixed = mixed + jnp.where(lane_head == hh, _dot(wsp_ref[hh], vblk), 0.0)
        yb_ref[rows, :] = (u_b[rows, :] * mixed).astype(BF16)


def _input_stage(x2, seq, g, sc, sh, w_in_pad, kvg, pavg, wsp, bsp, tm=512):
    n, d = x2.shape
    tpb = seq // tm
    row = lambda i: (i, 0)
    const2 = lambda i: (0, 0)
    per_b = lambda i: (i // tpb, 0, 0)
    outs = [(SSM_WIDTH, F32), (GMLP_WIDTH, BF16), (ATT_WIDTH, BF16), (KV_RANK, BF16),
            (IDX_HEADS * IDX_DIM, BF16), (IDX_DIM, BF16), (LANES, F32)]
    return pl.pallas_call(
        _in_kernel,
        grid=(n // tm,),
        in_specs=[pl.BlockSpec((tm, d), row),
                  pl.BlockSpec((1, d), const2),
                  pl.BlockSpec((1, 1, d), per_b),
                  pl.BlockSpec((1, 1, d), per_b),
                  pl.BlockSpec((d, IN_PAD), const2),
                  pl.BlockSpec((1, KV_RANK), const2),
                  pl.BlockSpec((GMLP_WIDTH, GMLP_WIDTH), const2),
                  pl.BlockSpec((GMLP_HEADS, GMLP_BLOCK, GMLP_BLOCK), lambda i: (0, 0, 0)),
                  pl.BlockSpec((GMLP_BLOCK, GMLP_WIDTH), const2)],
        out_specs=[pl.BlockSpec((tm, w), row) for w, _ in outs] + [pl.BlockSpec((KV_ROWS, tm), lambda i: (0, i))],
        out_shape=[jax.ShapeDtypeStruct((n, w), dt) for w, dt in outs] + [jax.ShapeDtypeStruct((KV_ROWS, n), BF16)],
        compiler_params=_cparams("parallel"),
        name="input_proj_gmlp",
    )(x2, g, sc, sh, w_in_pad, kvg, pavg, wsp, bsp)


def _ssm_prep_tm(a_re_log, a_im, b_re, b_im, c_re, c_im, log_dt, t_len):
    hp = lax.Precision.HIGHEST
    g_n, h_n, p_n = SSM_GROUPS, SSM_GROUP, SSM_STATE
    lam_re = -jnp.exp(a_re_log)
    lam_im = a_im
    dt = jnp.exp(log_dt)[:, None]
    mag = jnp.exp(lam_re * dt)
    abar_re = mag * jnp.cos(lam_im * dt)
    abar_im = mag * jnp.sin(lam_im * dt)
    den = lam_re * lam_re + lam_im * lam_im
    p = abar_re - 1.0
    qq = abar_im
    f_re = (p * lam_re + qq * lam_im) / den
    f_im = (qq * lam_re - p * lam_im) / den
    bb_re = f_re[..., None] * b_re - f_im[..., None] * b_im
    bb_im = f_re[..., None] * b_im + f_im[..., None] * b_re
    k = jnp.arange(t_len + 1, dtype=F32)[:, None, None]
    pmag = jnp.exp(k * (lam_re * dt))
    ang = k * (lam_im * dt)
    pw_re = pmag * jnp.cos(ang)
    pw_im = pmag * jnp.sin(ang)
    cp_re = c_re[None] * pw_re[:, :, None, :] - c_im[None] * pw_im[:, :, None, :]
    cp_im = c_re[None] * pw_im[:, :, None, :] + c_im[None] * pw_re[:, :, None, :]
    kern = (jnp.einsum('kghp,gpj->kghj', cp_re[:t_len], bb_re, precision=hp)
            - jnp.einsum('kghp,gpj->kghj', cp_im[:t_len], bb_im, precision=hp))
    grp_w = jnp.arange(g_n * h_n) // h_n
    grp_s = jnp.arange(g_n * p_n) // p_n
    kb = jnp.tile(kern.transpose(0, 1, 3, 2).reshape(t_len, g_n * h_n, h_n), (1, 1, g_n))
    kb = jnp.where((grp_w[:, None] == grp_w[None, :])[None], kb, 0.0)
    in_mask = grp_w[:, None] == grp_s[None, :]
    b_in = jnp.concatenate(
        [jnp.where(in_mask, jnp.tile(m.transpose(0, 2, 1).reshape(g_n * h_n, p_n), (1, g_n)), 0.0)
         for m in (bb_re, bb_im)], axis=1)
    c_out = jnp.concatenate(
        [jnp.where(in_mask.T, jnp.tile(m.transpose(0, 2, 1).reshape(g_n * p_n, h_n), (1, g_n)), 0.0)
         for m in (c_re, -c_im)], axis=0)
    tt = jnp.arange(t_len)
    flat = lambda a: a.reshape(a.shape[0], g_n * p_n)
    pw_end = jnp.concatenate([flat(pw_re[t_len - 1 - tt]), flat(pw_im[t_len - 1 - tt])], axis=1)
    pw_nxt = jnp.concatenate([flat(pw_re[1 + tt]), flat(pw_im[1 + tt])], axis=1)
    a_t = jnp.concatenate([flat(pw_re[t_len:]), flat(pw_im[t_len:])], axis=1)
    return kb.astype(BF16), b_in.astype(BF16), c_out.astype(BF16), pw_end, pw_nxt, a_t


def _ssm_tm_kernel(u_ref, kb_ref, bin_ref, cout_ref, pwe_ref, pwn_ref, at_ref, d_ref, y_ref,
                   ubuf_ref, sp_ref, st_ref, *, t_len):
    tm, width = u_ref.shape
    n_st = at_ref.shape[1] // 2
    nch = tm // t_len

    @pl.when(pl.program_id(1) == 0)
    def _():
        st_ref[...] = jnp.zeros_like(st_ref)

    u = u_ref[...]
    ubuf_ref[0:t_len, :] = jnp.zeros((t_len, width), F32)
    ubuf_ref[t_len:, :] = u
    pos = lax.broadcasted_iota(jnp.int32, (tm, 1), 0) % t_len
    y = _dot(u.astype(BF16), kb_ref[0])
    for k in range(1, t_len):
        shifted = ubuf_ref[t_len - k:t_len - k + tm, :]
        y = y + _dot(jnp.where(pos >= k, shifted, 0.0).astype(BF16), kb_ref[k])

    bu = _dot(u.astype(BF16), bin_ref[...]).reshape(nch, t_len, 2 * n_st)
    b_re, b_im = bu[:, :, :n_st], bu[:, :, n_st:]
    w_re, w_im = pwe_ref[:, :n_st][None], pwe_ref[:, n_st:][None]
    loc_re = jnp.sum(b_re * w_re - b_im * w_im, axis=1)
    loc_im = jnp.sum(b_re * w_im + b_im * w_re, axis=1)

    a_re, a_im = at_ref[:, :n_st], at_ref[:, n_st:]
    s_re, s_im = st_ref[:, :n_st], st_ref[:, n_st:]
    for c in range(nch):
        sp_ref[c:c + 1, :n_st] = s_re
        sp_ref[c:c + 1, n_st:] = s_im
        s_re, s_im = (a_re * s_re - a_im * s_im + loc_re[c:c + 1, :],
                      a_re * s_im + a_im * s_re + loc_im[c:c + 1, :])
    st_ref[:, :n_st] = s_re
    st_ref[:, n_st:] = s_im

    p_re, p_im = sp_ref[:, :n_st][:, None, :], sp_ref[:, n_st:][:, None, :]
    n_re, n_im = pwn_ref[:, :n_st][None], pwn_ref[:, n_st:][None]
    e_re = (p_re * n_re - p_im * n_im).reshape(tm, n_st).astype(BF16)
    e_im = (p_re * n_im + p_im * n_re).reshape(tm, n_st).astype(BF16)
    y = y + _dot(e_re, cout_ref[:n_st, :]) + _dot(e_im, cout_ref[n_st:, :])
    y_ref[...] = y + d_ref[...] * u


def _ssm_scan_tm(u_a, bsz, seq, prep, d_skip, tm=512):
    kb, b_in, c_out, pw_end, pw_nxt, a_t = prep
    t_len = kb.shape[0]
    n, width = u_a.shape
    tpb = seq // tm
    row = lambda b, i: (b * tpb + i, 0)
    c2 = lambda b, i: (0, 0)
    n2 = b_in.shape[1]
    return pl.pallas_call(
        functools.partial(_ssm_tm_kernel, t_len=t_len),
        grid=(bsz, tpb),
        in_specs=[pl.BlockSpec((tm, width), row),
                  pl.BlockSpec((t_len, width, width), lambda b, i: (0, 0, 0)),
                  pl.BlockSpec((width, n2), c2),
                  pl.BlockSpec((n2, width), c2),
                  pl.BlockSpec((t_len, n2), c2),
                  pl.BlockSpec((t_len, n2), c2),
                  pl.BlockSpec((1, n2), c2),
                  pl.BlockSpec((1, width), c2)],
        out_specs=pl.BlockSpec((tm, width), row),
        out_shape=jax.ShapeDtypeStruct((n, width), F32),
        scratch_shapes=[pltpu.VMEM((t_len + tm, width), F32),
                        pltpu.VMEM((tm // t_len, n2), F32),
                        pltpu.VMEM((1, n2), F32)],
        compiler_params=_cparams("parallel", "arbitrary"),
        name="ssm_scan",
    )(u_a, kb, b_in, c_out, pw_end, pw_nxt, a_t, d_skip)


def _dsa_kernel(q_ref, qi_ref, wi_ref, ki_ref, ckv_ref, ckvt_ref, wuk_ref, wuvt_ref, tri_ref, o_ref,
                key_ref, dm_ref, qa_ref, ot_ref, *acc_refs, topk, kb_len):
    tq = q_ref.shape[0]
    t0 = pl.program_id(1) * tq
    n_kb = (t0 + tq + kb_len - 1) // kb_len
    q_pos = t0 + lax.broadcasted_iota(jnp.int32, (1, tq), 1)
    q_chunk = q_pos // CHUNK
    k_off = lax.broadcasted_iota(jnp.int32, (kb_len, 1), 0)

    def fold8(m, op):
        r = m.reshape(kb_len // 8, 8, tq)
        n = kb_len // 8
        while n > 1:
            n //= 2
            r = op(r[:n], r[n:2 * n])
        return r[0]

    def col_sum(m):
        return fold8(m, jnp.add)

    wi_t = wi_ref[...].T
    qi_h = [qi_ref[:, h * IDX_DIM:(h + 1) * IDX_DIM] for h in range(IDX_HEADS)]
    w_h = [wi_t[IDX_DIM + h:IDX_DIM + h + 1, :] * ((IDX_HEADS ** -0.5) * (IDX_DIM ** -0.5))
           for h in range(IDX_HEADS)]

    def score_block(kb, carry):
        k0 = pl.multiple_of(kb * kb_len, kb_len)
        ki = ki_ref[pl.ds(k0, kb_len), :]
        score = jnp.zeros((kb_len, tq), F32)
        for h in range(IDX_HEADS):
            score = score + jnp.maximum(_dot_nt(ki, qi_h[h]), 0.0) * w_h[h]
        score = jnp.where(score == 0.0, 0.0, score)
        bits = lax.bitcast_convert_type(score, jnp.int32)
        key = jnp.where(bits < 0, bits ^ jnp.int32(0x7FFFFFFF), bits)
        adm = ((k0 + k_off) // CHUNK) <= q_chunk
        key_ref[pl.ds(k0, kb_len), :] = jnp.where(adm, key, jnp.int32(INT_MIN))
        return carry

    lax.fori_loop(0, n_kb, score_block, 0)

    def count(pred):
        def body(kb, acc):
            k0 = pl.multiple_of(kb * kb_len, kb_len)
            return acc + col_sum(pred(key_ref[pl.ds(k0, kb_len), :]).astype(F32))
        acc = lax.fori_loop(0, n_kb, body, jnp.zeros((8, tq), F32))
        return jnp.sum(acc, axis=0, keepdims=True)

    kf = jnp.float32(topk)

    def bit_step(step, ans):
        trial = ans + jnp.left_shift(jnp.int32(1), 31 - step)
        return jnp.where(count(lambda kblk: kblk >= trial) >= kf, trial, ans)

    ans = lax.fori_loop(0, 32, bit_step, jnp.full((1, tq), INT_MIN, jnp.int32))

    room = kf - count(lambda kblk: kblk > ans)
    tri = tri_ref[...]

    def select_block(kb, carry):
        k0 = pl.multiple_of(kb * kb_len, kb_len)
        kblk = key_ref[pl.ds(k0, kb_len), :]
        eq = kblk == ans
        pre = _dot(tri, eq.astype(BF16)) + carry
        sel = (kblk > ans) | (eq & (pre <= room))
        k_pos = k0 + k_off
        adm = (k_pos // CHUNK) <= q_chunk
        dist = jnp.abs(q_pos - k_pos).astype(F32)
        dm_ref[pl.ds(k0, kb_len), :] = jnp.where(sel & adm, dist, -NEG)
        return carry + jnp.sum(col_sum(eq.astype(F32)), axis=0, keepdims=True)

    lax.fori_loop(0, n_kb, select_block, jnp.zeros((1, tq), F32))

    for h in range(ATT_HEADS):
        hs = slice(h * ATT_HEAD_DIM, (h + 1) * ATT_HEAD_DIM)
        qa_ref[h] = (_dot(q_ref[:, hs], wuk_ref[h]) * (ATT_HEAD_DIM ** -0.5)).astype(BF16)
        acc_refs[h][...] = jnp.zeros((KV_ROWS, tq), F32)

    def att_block(kb, carry):
        k0 = pl.multiple_of(kb * kb_len, kb_len)
        ckv = ckv_ref[pl.ds(k0, kb_len), :]
        ckvt = ckvt_ref[:, pl.ds(k0, kb_len)]
        dm = dm_ref[pl.ds(k0, kb_len), :]
        new = []
        qk = _dot_nt(ckv, qa_ref[0])
        for h in range(ATT_HEADS):
            m_run = carry[h]
            slope = 2.0 ** (-8.0 * (h + 1) / ATT_HEADS)
            lg = qk - slope * dm
            if h + 1 < ATT_HEADS:
                qk = _dot_nt(ckv, qa_ref[h + 1])
            m_new = jnp.maximum(m_run, jnp.max(fold8(lg, jnp.maximum), axis=0, keepdims=True))
            alpha = jnp.exp(m_run - m_new)
            p = jnp.exp(lg - m_new)
            new.append(m_new)
            acc_refs[h][...] = alpha * acc_refs[h][...] + _dot(ckvt, p.astype(BF16))
        return tuple(new)

    lax.fori_loop(0, n_kb, att_block, tuple(jnp.full((1, tq), -jnp.inf, F32) for _ in range(ATT_HEADS)))
    for h in range(ATT_HEADS):
        o_t = (acc_refs[h][0:KV_RANK, :] / acc_refs[h][KV_RANK:KV_RANK + 1, :]).astype(BF16)
        ot_ref[h * ATT_HEAD_DIM:(h + 1) * ATT_HEAD_DIM, :] = _dot(wuvt_ref[h], o_t)
    o_ref[...] = ot_ref[...].T.astype(o_ref.dtype)


def _dsa(q, qi, wi, ki, ckv, ckvt, w_uk, w_uv, bsz, seq, tq=256, kb_len=512):
    n = q.shape[0]
    nqt = seq // tq
    kb_len = min(kb_len, seq)
    topk = min(TOPK_MAX, seq // 4)
    tri = (jnp.arange(kb_len)[:, None] >= jnp.arange(kb_len)[None, :]).astype(BF16)
    qrow = lambda b, i: (b * nqt + i, 0)
    krow = lambda b, i: (b, 0)
    c3 = lambda b, i: (0, 0, 0)
    return pl.pallas_call(
        functools.partial(_dsa_kernel, topk=topk, kb_len=kb_len),
        grid=(bsz, nqt),
        in_specs=[pl.BlockSpec((tq, ATT_WIDTH), qrow),
                  pl.BlockSpec((tq, IDX_HEADS * IDX_DIM), qrow),
                  pl.BlockSpec((tq, LANES), qrow),
                  pl.BlockSpec((seq, IDX_DIM), krow),
                  pl.BlockSpec((seq, KV_RANK), krow),
                  pl.BlockSpec((KV_ROWS, seq), lambda b, i: (0, b)),
                  pl.BlockSpec((ATT_HEADS, ATT_HEAD_DIM, KV_RANK), c3),
                  pl.BlockSpec((ATT_HEADS, ATT_HEAD_DIM, KV_RANK), c3),
                  pl.BlockSpec((kb_len, kb_len), lambda b, i: (0, 0))],
        out_specs=pl.BlockSpec((tq, ATT_WIDTH), qrow),
        out_shape=jax.ShapeDtypeStruct((n, ATT_WIDTH), BF16),
        scratch_shapes=[pltpu.VMEM((seq, tq), jnp.int32),
                        pltpu.VMEM((seq, tq), F32),
                        pltpu.VMEM((ATT_HEADS, tq, KV_RANK), BF16),
                        pltpu.VMEM((ATT_WIDTH, tq), F32)]
        + [pltpu.VMEM((KV_ROWS, tq), F32) for _ in range(ATT_HEADS)],
        compiler_params=_cparams("parallel", "parallel"),
        name="dsa_attention",
    )(q, qi, wi, ki, ckv, ckvt, w_uk.astype(BF16), jnp.swapaxes(w_uv, 1, 2).astype(BF16), tri)


def _out_kernel(ys_ref, yb_ref, yc_ref, x_ref, wglu_ref, bglu_ref, wo_ref, g1_ref,
                n2_ref, sc_ref, sh_ref, xo_ref, h2_ref):
    yg = jax.nn.gelu(ys_ref[...])
    z = _dot(yg.astype(BF16), wglu_ref[...]) + bglu_ref[...]
    ya = yg * jax.nn.sigmoid(z)
    a_w, b_w = SSM_WIDTH, SSM_WIDTH + GMLP_WIDTH
    mix = (_dot(ya.astype(BF16), wo_ref[0:a_w, :])
           + _dot(yb_ref[...], wo_ref[a_w:b_w, :])
           + _dot(yc_ref[...], wo_ref[b_w:, :]))
    xn = x_ref[...] + g1_ref[0] * mix
    xo_ref[...] = xn
    ms = jnp.mean(xn * xn, axis=-1, keepdims=True)
    h = xn * lax.rsqrt(ms + RMS_EPS) * n2_ref[...]
    h2_ref[...] = (h * (1.0 + sc_ref[0]) + sh_ref[0]).astype(BF16)


def _output_stage(ys, yb, yc, x2, seq, w_glu, b_glu, w_out, g1, n2g, sc2, sh2, tm=512):
    n, d = x2.shape
    tpb = seq // tm
    row = lambda i: (i, 0)
    const2 = lambda i: (0, 0)
    per_b = lambda i: (i // tpb, 0, 0)
    return pl.pallas_call(
        _out_kernel,
        grid=(n // tm,),
        in_specs=[pl.BlockSpec((tm, SSM_WIDTH), row),
                  pl.BlockSpec((tm, GMLP_WIDTH), row),
                  pl.BlockSpec((tm, ATT_WIDTH), row),
                  pl.BlockSpec((tm, d), row),
                  pl.BlockSpec((SSM_WIDTH, SSM_WIDTH), const2),
                  pl.BlockSpec((1, SSM_WIDTH), const2),
                  pl.BlockSpec((d, d), const2),
                  pl.BlockSpec((1, 1, d), per_b),
                  pl.BlockSpec((1, d), const2),
                  pl.BlockSpec((1, 1, d), per_b),
                  pl.BlockSpec((1, 1, d), per_b)],
        out_specs=[pl.BlockSpec((tm, d), row), pl.BlockSpec((tm, d), row)],
        out_shape=[jax.ShapeDtypeStruct((n, d), F32), jax.ShapeDtypeStruct((n, d), BF16)],
        compiler_params=_cparams("parallel"),
        name="out_proj_norm2",
    )(ys, yb, yc, x2, w_glu, b_glu, w_out, g1, n2g, sc2, sh2)


def _sort16_network():
    def merge(lo, hi, r):
        step = r * 2
        if step < hi - lo:
            yield from merge(lo, hi, step)
            yield from merge(lo + r, hi, step)
            yield from [(i, i + r) for i in range(lo + r, hi - r, step)]
        else:
            yield (lo, lo + r)

    def sort(lo, hi):
        if hi - lo >= 1:
            mid = lo + (hi - lo) // 2
            yield from sort(lo, mid)
            yield from sort(mid + 1, hi)
            yield from merge(lo, hi, 1)

    return tuple(sort(0, 15))


def _extract_top(x, n_top):
    tops = []
    for _ in range(n_top):
        mx = jnp.max(x, axis=0, keepdims=True)
        tops.append(mx)
        x = jnp.where(x == mx, -jnp.inf, x)
    return tops


def _extract_top_128(x, n_top):
    sub = x.shape[0] // 16
    rows = [x[i * sub:(i + 1) * sub, :] for i in range(16)]
    for a, b in _sort16_network():
        rows[a], rows[b] = jnp.maximum(rows[a], rows[b]), jnp.minimum(rows[a], rows[b])
    tops = []
    for k in range(n_top):
        mx = jnp.max(rows[0], axis=0, keepdims=True)
        tops.append(mx)
        need = min(16, n_top - 1 - k)
        if need:
            hit = rows[0] == mx
            below = rows[1:need + 1] + ([jnp.full_like(rows[0], -jnp.inf)] if need == 16 else [])
            for d in range(need):
                rows[d] = jnp.where(hit, below[d], rows[d])
    return tops


def _peer_kernel(h2_ref, x_ref, g2_ref, fg_ref, wq_ref, kbd_ref, *rest, final_norm, n_split):
    u_refs, vt_refs = rest[:n_split], rest[n_split:2 * n_split]
    o_ref, st_ref, e1_ref, e2_ref, r2_ref, acc_ref, at_ref, ga_ref, v1_ref, v2_ref = rest[2 * n_split:]
    j = pl.program_id(1)
    tm = h2_ref.shape[0]
    u_rows = u_refs[0].shape[0]
    v_rows = vt_refs[0].shape[0]
    te = u_rows * n_split
    n1 = te // PEER_N_KEYS
    nk = PEER_N_KEYS
    h2 = h2_ref[...]

    @pl.when(j == 0)
    def _():
        q = _dot(h2, wq_ref[...]).astype(BF16)
        st_ref[...] = _dot_nt(kbd_ref[...], q)

        def per_head(h, carry):
            base = pl.multiple_of(h * 2 * nk, 2 * nk)
            for lc in range(tm // LANES):
                ls = slice(lc * LANES, (lc + 1) * LANES)
                t1 = _extract_top_128(st_ref[pl.ds(base, nk), ls], PEER_TOPK + 1)
                t2 = _extract_top_128(st_ref[pl.ds(base + nk, nk), ls], PEER_TOPK + 1)
                for k in range(PEER_TOPK):
                    v1_ref[k:k + 1, :] = t1[k]
                    v2_ref[k:k + 1, :] = t2[k]
                v2_all = v2_ref[0:PEER_TOPK, :]
                v2_top = v2_ref[0:8, :]
                cand = [t1[0] + v2_all]
                cand += [t1[a] + v2_top for a in range(1, 8)]
                cand.append(v1_ref[8:PEER_TOPK, :] + t2[0])
                top = _extract_top(jnp.concatenate(cand, axis=0), PEER_TOPK + 1)
                z = jnp.zeros_like(top[0])
                for k in range(PEER_TOPK):
                    z = z + jnp.exp(top[k] - top[0])
                nxt = jnp.maximum(top[PEER_TOPK], jnp.maximum(t1[PEER_TOPK] + t2[0], t1[0] + t2[PEER_TOPK]))
                thr = 0.5 * (top[PEER_TOPK - 1] + nxt)
                s1 = st_ref[pl.ds(base, nk), ls]
                s2 = st_ref[pl.ds(base + nk, nk), ls]
                e1_ref[h, :, ls] = jnp.exp(s1 - t1[0]) * (0.5 / z)
                e2_ref[h, :, ls] = jnp.exp(s2 - t2[0]).astype(BF16)
                rank2 = jnp.zeros((nk, LANES), F32)
                for b in range(PEER_TOPK):
                    rank2 = rank2 + jnp.where(s2 < t2[b], 1.0, 0.0)
                r2_ref[h, :, ls] = rank2.astype(BF16)
                n_sel = jnp.zeros((nk, LANES), F32)
                for a in range(PEER_TOPK):
                    n_a = jnp.zeros_like(thr)
                    for b in range(PEER_TOPK // (a + 1)):
                        n_a = n_a + jnp.where(t1[a] + t2[b] >= thr, 1.0, 0.0)
                    n_sel = jnp.where(s1 == t1[a], n_a, n_sel)
                st_ref[pl.ds(base, nk), ls] = n_sel
            return carry

        lax.fori_loop(0, PEER_HEADS, per_head, 0)
        acc_ref[...] = jnp.zeros_like(acc_ref)
        ga_ref[1] = jnp.zeros(ga_ref.shape[1:], BF16)
        for k in range(n_split):
            at_ref[0, k * u_rows:(k + 1) * u_rows, :] = _dot_nt(u_refs[k][...], h2)

    n_blocks = pl.num_programs(1) - 2
    rd = j % 2
    wr = 1 - rd

    @pl.when((j >= 1) & (j <= n_blocks))
    def _():
        for k in range(n_split):
            at_ref[rd, k * u_rows:(k + 1) * u_rows, :] = _dot_nt(u_refs[k][...], h2)
        for k in range(n_split):
            acc_ref[k * v_rows:(k + 1) * v_rows, :] += _dot(vt_refs[k][...], ga_ref[rd])
        r0 = pl.multiple_of((j - 1) * n1, n1)
        for lc in range(tm // LANES):
            ls = slice(lc * LANES, (lc + 1) * LANES)
            for il in range(n1):
                g = jnp.zeros((nk, LANES), BF16)
                for h in range(PEER_HEADS):
                    n_row = st_ref[pl.ds(h * 2 * nk + r0, n1), ls][il:il + 1, :].astype(BF16)
                    e_row = e1_ref[h, pl.ds(r0, n1), ls][il:il + 1, :].astype(BF16)
                    picked = jnp.where(r2_ref[h, :, ls] < n_row, e2_ref[h, :, ls], jnp.zeros((), BF16))
                    g = g + picked * e_row
                a = at_ref[wr, il * nk:(il + 1) * nk, ls]
                inner = a * (0.7978845608028654 + 0.035677408136300125 * (a * a))
                act2 = a + a * jnp.tanh(inner)
                ga_ref[wr, il * nk:(il + 1) * nk, ls] = g * act2.astype(BF16)

    @pl.when(j == n_blocks + 1)
    def _():
        for k in range(n_split):
            acc_ref[k * v_rows:(k + 1) * v_rows, :] += _dot(vt_refs[k][...], ga_ref[rd])
        out = x_ref[...] + g2_ref[0] * acc_ref[...].T
        if final_norm:
            ms = jnp.mean(out * out, axis=-1, keepdims=True)
            out = out * lax.rsqrt(ms + RMS_EPS) * fg_ref[...]
        o_ref[...] = out


def _peer(h2, x2, seq, g2, fg, wq, kbd, u_bf, vt_bf, final_norm, tm=512, te=1024, n_split=4):
    n, d = x2.shape
    n_blk = u_bf.shape[0] // te
    tpb = seq // tm
    trow = lambda i, j: (i, 0)
    c2 = lambda i, j: (0, 0)
    hs = (PEER_HEADS, PEER_N_KEYS, tm)
    u_map = lambda k, i, j: (jnp.minimum(j, n_blk - 1) * n_split + k, 0)
    v_map = lambda k, i, j: (k, jnp.clip(j - 2, 0, n_blk - 1))
    return pl.pallas_call(
        functools.partial(_peer_kernel, final_norm=final_norm, n_split=n_split),
        grid=(n // tm, n_blk + 2),
        in_specs=[pl.BlockSpec((tm, d), trow),
                  pl.BlockSpec((tm, d), trow),
                  pl.BlockSpec((1, 1, d), lambda i, j: (i // tpb, 0, 0)),
                  pl.BlockSpec((1, d), c2),
                  pl.BlockSpec((d, PEER_HEADS * 2 * PEER_HALF), c2),
                  pl.BlockSpec((PEER_HEADS * 2 * PEER_N_KEYS, PEER_HEADS * 2 * PEER_HALF), c2)]
        + [pl.BlockSpec((te // n_split, d), functools.partial(u_map, k)) for k in range(n_split)]
        + [pl.BlockSpec((d // n_split, te), functools.partial(v_map, k)) for k in range(n_split)],
        out_specs=pl.BlockSpec((tm, d), trow),
        out_shape=jax.ShapeDtypeStruct((n, d), F32),
        scratch_shapes=[pltpu.VMEM((PEER_HEADS * 2 * PEER_N_KEYS, tm), F32),
                        pltpu.VMEM(hs, F32), pltpu.VMEM(hs, BF16), pltpu.VMEM(hs, BF16),
                        pltpu.VMEM((d, tm), F32),
                        pltpu.VMEM((2, te, tm), F32),
                        pltpu.VMEM((2, te, tm), BF16),
                        pltpu.VMEM((PEER_TOPK, LANES), F32), pltpu.VMEM((PEER_TOPK, LANES), F32)],
        compiler_params=_cparams("parallel", "arbitrary"),
        name="peer_dense",
    )(h2, x2, g2, fg, wq, kbd, *([u_bf] * n_split), *([vt_bf] * n_split))


def _peer_key_matrix(k1, k2):
    blocks = []
    for h in range(PEER_HEADS):
        for half, kk in enumerate((k1, k2)):
            col = (2 * h + half) * PEER_HALF
            blocks.append(jnp.pad(kk, ((0, 0), (col, PEER_HEADS * 2 * PEER_HALF - col - PEER_HALF))))
    return jnp.concatenate(blocks, axis=0).astype(BF16)


def kernel(x, c, norm1_g, norm2_g, w_mod, b_mod, w_in, ssm_a_re_log, ssm_a_im, ssm_b_re, ssm_b_im, ssm_c_re, ssm_c_im, ssm_d, ssm_log_dt, ssm_w_glu, ssm_b_glu, gmlp_w_sp, gmlp_b_sp, kv_norm_g, w_uk, w_uv, w_out, peer_w_q, peer_k1, peer_k2, peer_u, peer_v, final_g):
    bsz, seq, d = x.shape
    depth = w_mod.shape[0]
    x2 = x.reshape(bsz * seq, d)
    head_of = jnp.arange(GMLP_WIDTH) // GMLP_HEAD_DIM
    pavg = ((head_of[:, None] == head_of[None, :]).astype(F32) / GMLP_HEAD_DIM).astype(BF16)
    chunk_of = jnp.arange(GMLP_BLOCK) // CHUNK
    sp_mask = chunk_of[:, None] >= chunk_of[None, :]
    for l in range(depth):
        mod = _modulation(c, w_mod[l], b_mod[l])
        sh1, sc1, g1, sh2, sc2, g2 = [mod[:, i * d:(i + 1) * d].reshape(bsz, 1, d) for i in range(6)]
        w_in_pad = jnp.pad(w_in[l], ((0, 0), (0, IN_PAD - IN_WIDTH))).astype(BF16)
        wsp = jnp.where(sp_mask[None], gmlp_w_sp[l], 0.0).astype(BF16)
        bsp = jnp.repeat(gmlp_b_sp[l].T, GMLP_HEAD_DIM, axis=1)
        ua, yb, q, ckv, qi, ki, wi, ckvt = _input_stage(
            x2, seq, norm1_g[l].reshape(1, d), sc1, sh1, w_in_pad, kv_norm_g[l].reshape(1, KV_RANK),
            pavg, wsp, bsp)
        prep = _ssm_prep_tm(ssm_a_re_log[l], ssm_a_im[l], ssm_b_re[l], ssm_b_im[l], ssm_c_re[l], ssm_c_im[l],
                            ssm_log_dt[l], SSM_T)
        ys = _ssm_scan_tm(ua, bsz, seq, prep, ssm_d[l].reshape(1, SSM_WIDTH))
        yc = _dsa(q, qi, wi, ki, ckv, ckvt, w_uk[l], w_uv[l], bsz, seq)
        x2, h2 = _output_stage(
            ys, yb, yc, x2, seq, ssm_w_glu[l].astype(BF16),
            ssm_b_glu[l].reshape(1, SSM_WIDTH), w_out[l].astype(BF16), g1, norm2_g[l].reshape(1, d), sc2, sh2)
        x2 = _peer(h2, x2, seq, g2, final_g.reshape(1, d), peer_w_q[l].astype(BF16),
                   _peer_key_matrix(peer_k1[l], peer_k2[l]), peer_u[l].astype(BF16),
                   peer_v[l].T.astype(BF16), final_norm=(l == depth - 1))
    return x2.reshape(bsz, seq, d)
```

```python
import functools
import math

import jax
import jax.numpy as jnp
from jax import lax
from jax.experimental import pallas as pl
from jax.experimental.pallas import tpu as pltpu

CHUNK = 64
RMS_EPS = 1e-6

SSM_GROUP = 16
SSM_STATE = 64
SSM_WIDTH = 256
SSM_GROUPS = SSM_WIDTH // SSM_GROUP
SSM_T = 16
GMLP_HEADS = 4
GMLP_HEAD_DIM = 64
GMLP_WIDTH = GMLP_HEADS * GMLP_HEAD_DIM
GMLP_BLOCK = 128
ATT_HEADS = 8
ATT_HEAD_DIM = 64
ATT_WIDTH = ATT_HEADS * ATT_HEAD_DIM
KV_RANK = 128
KV_ROWS = KV_RANK + 8
IDX_HEADS = 4
IDX_DIM = 64
TOPK_MAX = 256

IN_SIZES = (SSM_WIDTH, GMLP_WIDTH, GMLP_WIDTH, ATT_WIDTH, KV_RANK, IDX_HEADS * IDX_DIM, IDX_DIM, IDX_HEADS)
IN_WIDTH = sum(IN_SIZES)
IN_PAD = 1792
COL_UB, COL_VB, COL_Q, COL_CKV, COL_QI, COL_KW = 256, 512, 768, 1280, 1408, 1664

PEER_HEADS = 8
PEER_N_KEYS = 128
PEER_HALF = 64
PEER_TOPK = 16

LANES = 128
VMEM_LIMIT = 56 * 1024 * 1024
NEG = -1e30
MIN_NORMAL = 2.0 ** -126

F32 = jnp.float32
BF16 = jnp.bfloat16


def _cparams(*sem):
    return pltpu.CompilerParams(dimension_semantics=sem, vmem_limit_bytes=VMEM_LIMIT)


def _dot(a, b):
    return jnp.dot(a, b, preferred_element_type=F32)


def _dot_nt(a, b):
    return lax.dot_general(a, b, (((1,), (1,)), ((), ())), preferred_element_type=F32)


def _mod_kernel(c_ref, w_ref, b_ref, o_ref):
    c = c_ref[...]
    ca = c * jax.nn.sigmoid(c)
    o_ref[...] = _dot(ca.astype(BF16), w_ref[...].astype(BF16)) + b_ref[...]


def _modulation(c, w_mod, b_mod):
    bsz, d = c.shape
    n6 = w_mod.shape[1]
    return pl.pallas_call(
        _mod_kernel,
        grid=(n6 // d,),
        in_specs=[pl.BlockSpec((bsz, d), lambda j: (0, 0)),
                  pl.BlockSpec((d, d), lambda j: (0, j)),
                  pl.BlockSpec((1, d), lambda j: (0, j))],
        out_specs=pl.BlockSpec((bsz, d), lambda j: (0, j)),
        out_shape=jax.ShapeDtypeStruct((bsz, n6), F32),
        compiler_params=_cparams("parallel"),
        name="modulation",
    )(c, w_mod, b_mod.reshape(1, n6))


def _in_kernel(x_ref, g_ref, sc_ref, sh_ref, w_ref, kvg_ref, pavg_ref, wsp_ref, bsp_ref,
               ua_ref, yb_ref, q_ref, ckv_ref, qi_ref, ki_ref, wi_ref, ckvt_ref):
    tm = x_ref.shape[0]
    x = x_ref[...]
    ms = jnp.mean(x * x, axis=-1, keepdims=True)
    h = x * lax.rsqrt(ms + RMS_EPS) * g_ref[...]
    h = h * (1.0 + sc_ref[0]) + sh_ref[0]
    proj = _dot(h.astype(BF16), w_ref[...])

    ua_ref[...] = proj[:, 0:COL_UB]
    q_ref[...] = proj[:, COL_Q:COL_CKV].astype(BF16)
    ckv = proj[:, COL_CKV:COL_QI]
    ckv_ms = jnp.mean(ckv * ckv, axis=-1, keepdims=True)
    ckv_n = ckv * lax.rsqrt(ckv_ms + RMS_EPS) * kvg_ref[...]
    ckv_ref[...] = ckv_n.astype(BF16)
    ckvt_ref[0:KV_RANK, :] = ckv_n.T.astype(BF16)
    ones_row = lax.broadcasted_iota(jnp.int32, (KV_ROWS - KV_RANK, tm), 0) == 0
    ckvt_ref[KV_RANK:, :] = jnp.where(ones_row, 1.0, 0.0).astype(BF16)
    qi_ref[...] = proj[:, COL_QI:COL_KW].astype(BF16)
    kw = proj[:, COL_KW:IN_PAD]
    ki_ref[...] = kw[:, 0:IDX_DIM].astype(BF16)
    wi_ref[...] = kw

    u_b = proj[:, COL_UB:COL_VB]
    v_b = proj[:, COL_VB:COL_Q]
    pavg = pavg_ref[...]

    def head_mean(a):
        hi = a.astype(BF16)
        lo = (a - hi.astype(F32)).astype(BF16)
        return _dot(hi, pavg) + _dot(lo, pavg)

    mu = head_mean(v_b)
    dv = v_b - mu
    var = head_mean(dv * dv)
    vn = (dv * lax.rsqrt(var + RMS_EPS)).astype(BF16)
    lane_head = lax.broadcasted_iota(jnp.int32, (GMLP_BLOCK, GMLP_WIDTH), 1) // GMLP_HEAD_DIM
    for blk in range(tm // GMLP_BLOCK):
        rows = slice(blk * GMLP_BLOCK, (blk + 1) * GMLP_BLOCK)
        vblk = vn[rows, :]
        mixed = bsp_ref[...]
        for hh in range(GMLP_HEADS):
            mixed = mixed + jnp.where(lane_head == hh, _dot(wsp_ref[hh], vblk), 0.0)
        yb_ref[rows, :] = (u_b[rows, :] * mixed).astype(BF16)


def _input_stage(x2, seq, g, sc, sh, w_in_pad, kvg, pavg, wsp, bsp, tm=512):
    n, d = x2.shape
    tpb = seq // tm
    row = lambda i: (i, 0)
    const2 = lambda i: (0, 0)
    per_b = lambda i: (i // tpb, 0, 0)
    outs = [(SSM_WIDTH, F32), (GMLP_WIDTH, BF16), (ATT_WIDTH, BF16), (KV_RANK, BF16),
            (IDX_HEADS * IDX_DIM, BF16), (IDX_DIM, BF16), (LANES, F32)]
    return pl.pallas_call(
        _in_kernel,
        grid=(n // tm,),
        in_specs=[pl.BlockSpec((tm, d), row),
                  pl.BlockSpec((1, d), const2),
                  pl.BlockSpec((1, 1, d), per_b),
                  pl.BlockSpec((1, 1, d), per_b),
                  pl.BlockSpec((d, IN_PAD), const2),
                  pl.BlockSpec((1, KV_RANK), const2),
                  pl.BlockSpec((GMLP_WIDTH, GMLP_WIDTH), const2),
                  pl.BlockSpec((GMLP_HEADS, GMLP_BLOCK, GMLP_BLOCK), lambda i: (0, 0, 0)),
                  pl.BlockSpec((GMLP_BLOCK, GMLP_WIDTH), const2)],
        out_specs=[pl.BlockSpec((tm, w), row) for w, _ in outs] + [pl.BlockSpec((KV_ROWS, tm), lambda i: (0, i))],
        out_shape=[jax.ShapeDtypeStruct((n, w), dt) for w, dt in outs] + [jax.ShapeDtypeStruct((KV_ROWS, n), BF16)],
        compiler_params=_cparams("parallel"),
        name="input_proj_gmlp",
    )(x2, g, sc, sh, w_in_pad, kvg, pavg, wsp, bsp)


def _ssm_prep_tm(a_re_log, a_im, b_re, b_im, c_re, c_im, log_dt, t_len):
    hp = lax.Precision.HIGHEST
    g_n, h_n, p_n = SSM_GROUPS, SSM_GROUP, SSM_STATE
    lam_re = -jnp.exp(a_re_log)
    lam_im = a_im
    dt = jnp.exp(log_dt)[:, None]
    mag = jnp.exp(lam_re * dt)
    abar_re = mag * jnp.cos(lam_im * dt)
    abar_im = mag * jnp.sin(lam_im * dt)
    den = lam_re * lam_re + lam_im * lam_im
    p = abar_re - 1.0
    qq = abar_im
    f_re = (p * lam_re + qq * lam_im) / den
    f_im = (qq * lam_re - p * lam_im) / den
    bb_re = f_re[..., None] * b_re - f_im[..., None] * b_im
    bb_im = f_re[..., None] * b_im + f_im[..., None] * b_re
    k = jnp.arange(t_len + 1, dtype=F32)[:, None, None]
    pmag = jnp.exp(k * (lam_re * dt))
    ang = k * (lam_im * dt)
    pw_re = pmag * jnp.cos(ang)
    pw_im = pmag * jnp.sin(ang)
    cp_re = c_re[None] * pw_re[:, :, None, :] - c_im[None] * pw_im[:, :, None, :]
    cp_im = c_re[None] * pw_im[:, :, None, :] + c_im[None] * pw_re[:, :, None, :]
    kern = (jnp.einsum('kghp,gpj->kghj', cp_re[:t_len], bb_re, precision=hp)
            - jnp.einsum('kghp,gpj->kghj', cp_im[:t_len], bb_im, precision=hp))
    grp_w = jnp.arange(g_n * h_n) // h_n
    grp_s = jnp.arange(g_n * p_n) // p_n
    kb = jnp.tile(kern.transpose(0, 1, 3, 2).reshape(t_len, g_n * h_n, h_n), (1, 1, g_n))
    kb = jnp.where((grp_w[:, None] == grp_w[None, :])[None], kb, 0.0)
    in_mask = grp_w[:, None] == grp_s[None, :]
    b_in = jnp.concatenate(
        [jnp.where(in_mask, jnp.tile(m.transpose(0, 2, 1).reshape(g_n * h_n, p_n), (1, g_n)), 0.0)
         for m in (bb_re, bb_im)], axis=1)
    c_out = jnp.concatenate(
        [jnp.where(in_mask.T, jnp.tile(m.transpose(0, 2, 1).reshape(g_n * p_n, h_n), (1, g_n)), 0.0)
         for m in (c_re, -c_im)], axis=0)
    tt = jnp.arange(t_len)
    flat = lambda a: a.reshape(a.shape[0], g_n * p_n)
    pw_end = jnp.concatenate([flat(pw_re[t_len - 1 - tt]), flat(pw_im[t_len - 1 - tt])], axis=1)
    pw_nxt = jnp.concatenate([flat(pw_re[1 + tt]), flat(pw_im[1 + tt])], axis=1)
    a_t = jnp.concatenate([flat(pw_re[t_len:]), flat(pw_im[t_len:])], axis=1)
    return kb.astype(BF16), b_in.astype(BF16), c_out.astype(BF16), pw_end, pw_nxt, a_t


def _ssm_tm_kernel(u_ref, kb_ref, bin_ref, cout_ref, pwe_ref, pwn_ref, at_ref, d_ref, y_ref,
                   ubuf_ref, sp_ref, st_ref, *, t_len):
    tm, width = u_ref.shape
    n_st = at_ref.shape[1] // 2
    nch = tm // t_len

    @pl.when(pl.program_id(1) == 0)
    def _():
        st_ref[...] = jnp.zeros_like(st_ref)

    u = u_ref[...]
    ubuf_ref[0:t_len, :] = jnp.zeros((t_len, width), F32)
    ubuf_ref[t_len:, :] = u
    pos = lax.broadcasted_iota(jnp.int32, (tm, 1), 0) % t_len
    y = _dot(u.astype(BF16), kb_ref[0])
    for k in range(1, t_len):
        shifted = ubuf_ref[t_len - k:t_len - k + tm, :]
        y = y + _dot(jnp.where(pos >= k, shifted, 0.0).astype(BF16), kb_ref[k])

    bu = _dot(u.astype(BF16), bin_ref[...]).reshape(nch, t_len, 2 * n_st)
    b_re, b_im = bu[:, :, :n_st], bu[:, :, n_st:]
    w_re, w_im = pwe_ref[:, :n_st][None], pwe_ref[:, n_st:][None]
    loc_re = jnp.sum(b_re * w_re - b_im * w_im, axis=1)
    loc_im = jnp.sum(b_re * w_im + b_im * w_re, axis=1)

    a_re, a_im = at_ref[:, :n_st], at_ref[:, n_st:]
    s_re, s_im = st_ref[:, :n_st], st_ref[:, n_st:]
    for c in range(nch):
        sp_ref[c:c + 1, :n_st] = s_re
        sp_ref[c:c + 1, n_st:] = s_im
        s_re, s_im = (a_re * s_re - a_im * s_im + loc_re[c:c + 1, :],
                      a_re * s_im + a_im * s_re + loc_im[c:c + 1, :])
    st_ref[:, :n_st] = s_re
    st_ref[:, n_st:] = s_im

    p_re, p_im = sp_ref[:, :n_st][:, None, :], sp_ref[:, n_st:][:, None, :]
    n_re, n_im = pwn_ref[:, :n_st][None], pwn_ref[:, n_st:][None]
    e_re = (p_re * n_re - p_im * n_im).reshape(tm, n_st).astype(BF16)
    e_im = (p_re * n_im + p_im * n_re).reshape(tm, n_st).astype(BF16)
    y = y + _dot(e_re, cout_ref[:n_st, :]) + _dot(e_im, cout_ref[n_st:, :])
    y_ref[...] = y + d_ref[...] * u


def _ssm_scan_tm(u_a, bsz, seq, prep, d_skip, tm=512):
    kb, b_in, c_out, pw_end, pw_nxt, a_t = prep
    t_len = kb.shape[0]
    n, width = u_a.shape
    tpb = seq // tm
    row = lambda b, i: (b * tpb + i, 0)
    c2 = lambda b, i: (0, 0)
    n2 = b_in.shape[1]
    return pl.pallas_call(
        functools.partial(_ssm_tm_kernel, t_len=t_len),
        grid=(bsz, tpb),
        in_specs=[pl.BlockSpec((tm, width), row),
                  pl.BlockSpec((t_len, width, width), lambda b, i: (0, 0, 0)),
                  pl.BlockSpec((width, n2), c2),
                  pl.BlockSpec((n2, width), c2),
                  pl.BlockSpec((t_len, n2), c2),
                  pl.BlockSpec((t_len, n2), c2),
                  pl.BlockSpec((1, n2), c2),
                  pl.BlockSpec((1, width), c2)],
        out_specs=pl.BlockSpec((tm, width), row),
        out_shape=jax.ShapeDtypeStruct((n, width), F32),
        scratch_shapes=[pltpu.VMEM((t_len + tm, width), F32),
                        pltpu.VMEM((tm // t_len, n2), F32),
                        pltpu.VMEM((1, n2), F32)],
        compiler_params=_cparams("parallel", "arbitrary"),
        name="ssm_scan",
    )(u_a, kb, b_in, c_out, pw_end, pw_nxt, a_t, d_skip)


def _dsa_kernel(q_ref, qi_ref, wi_ref, ki_ref, ckv_ref, ckvt_ref, wuk_ref, wuvt_ref, tri_ref, o_ref,
                key_ref, hi_ref, dm_ref, qa_ref, ot_ref, *acc_refs, topk, kb_len):
    tq = q_ref.shape[0]
    t0 = pl.program_id(1) * tq
    n_kb = (t0 + tq + kb_len - 1) // kb_len
    q_pos = t0 + lax.broadcasted_iota(jnp.int32, (1, tq), 1)
    q_chunk = q_pos // CHUNK
    k_off = lax.broadcasted_iota(jnp.int32, (kb_len, 1), 0)

    def fold8(m, op):
        r = m.reshape(kb_len // 8, 8, tq)
        n = kb_len // 8
        while n > 1:
            n //= 2
            r = op(r[:n], r[n:2 * n])
        return r[0]

    def col_sum(m):
        return fold8(m, jnp.add)

    wi_t = wi_ref[...].T
    qi_h = [qi_ref[:, h * IDX_DIM:(h + 1) * IDX_DIM] for h in range(IDX_HEADS)]
    w_h = [wi_t[IDX_DIM + h:IDX_DIM + h + 1, :] * ((IDX_HEADS ** -0.5) * (IDX_DIM ** -0.5))
           for h in range(IDX_HEADS)]

    def score_block(kb, carry):
        k0 = pl.multiple_of(kb * kb_len, kb_len)
        ki = ki_ref[pl.ds(k0, kb_len), :]
        score = jnp.zeros((kb_len, tq), F32)
        for h in range(IDX_HEADS):
            score = score + jnp.maximum(_dot_nt(ki, qi_h[h]), 0.0) * w_h[h]
        score = jnp.where(jnp.abs(score) < MIN_NORMAL, 0.0, score)
        adm = ((k0 + k_off) // CHUNK) <= q_chunk
        score = jnp.where(adm, score, -jnp.inf)
        key_ref[pl.ds(k0, kb_len), :] = score
        hi_ref[pl.ds(k0, kb_len), :] = score.astype(BF16)
        return carry

    lax.fori_loop(0, n_kb, score_block, 0)

    def count(pred):
        def body(kb, acc):
            k0 = pl.multiple_of(kb * kb_len, kb_len)
            return acc + col_sum(pred(key_ref[pl.ds(k0, kb_len), :]).astype(F32))
        acc = lax.fori_loop(0, n_kb, body, jnp.zeros((8, tq), F32))
        return jnp.sum(acc, axis=0, keepdims=True)

    kf = jnp.float32(topk)
    code_neg_inf = jnp.int32(0xFF800000 - (1 << 32)) ^ jnp.int32(0x7FFFFFFF)

    def code_to_float(code):
        code = jnp.maximum(code, code_neg_inf)
        code = jnp.where((code > 0) & (code < 0x00800000), jnp.int32(0x00800000), code)
        return lax.bitcast_convert_type(jnp.where(code >= 0, code, code ^ jnp.int32(0x7FFFFFFF)), F32)

    def count_upper(trial_bf):
        def body(kb, acc):
            k0 = pl.multiple_of(kb * kb_len, kb_len)
            hit = jnp.where(hi_ref[pl.ds(k0, kb_len), :] >= trial_bf, jnp.ones((), BF16), jnp.zeros((), BF16))
            r = hit.reshape(kb_len // 16, 16, tq)
            n = kb_len // 16
            while n > 1:
                n //= 2
                r = r[:n] + r[n:2 * n]
            return acc + r[0]
        acc = lax.fori_loop(0, n_kb, body, jnp.zeros((16, tq), BF16))
        return jnp.sum(acc.astype(F32), axis=0, keepdims=True)

    def upper_step(step, up):
        trial = up + jnp.left_shift(jnp.int32(1), 15 - step)
        trial_bf = code_to_float(jnp.left_shift(trial, 16)).astype(BF16)
        return jnp.where(count_upper(trial_bf) >= kf, trial, up)

    upper_half = lax.fori_loop(0, 16, upper_step, jnp.full((1, tq), -32768, jnp.int32))

    def lower_step(step, code):
        trial = code + jnp.left_shift(jnp.int32(1), 16 - step)
        trial_f = code_to_float(trial)
        return jnp.where(count(lambda sblk: sblk >= trial_f) >= kf, trial, code)

    window_lo = jnp.left_shift(upper_half, 16) - jnp.int32(0x8001)
    ans = code_to_float(lax.fori_loop(0, 17, lower_step, window_lo))

    room = kf - count(lambda kblk: kblk > ans)
    tri = tri_ref[...]

    def select_block(kb, carry):
        k0 = pl.multiple_of(kb * kb_len, kb_len)
        kblk = key_ref[pl.ds(k0, kb_len), :]
        eq = kblk == ans
        pre = _dot(tri, eq.astype(BF16)) + carry
        sel = (kblk > ans) | (eq & (pre <= room))
        k_pos = k0 + k_off
        adm = (k_pos // CHUNK) <= q_chunk
        dist = jnp.abs(q_pos - k_pos).astype(F32)
        dm_ref[pl.ds(k0, kb_len), :] = jnp.where(sel & adm, dist, -NEG)
        return carry + jnp.sum(col_sum(eq.astype(F32)), axis=0, keepdims=True)

    lax.fori_loop(0, n_kb, select_block, jnp.zeros((1, tq), F32))

    for h in range(ATT_HEADS):
        hs = slice(h * ATT_HEAD_DIM, (h + 1) * ATT_HEAD_DIM)
        qa_ref[h] = (_dot(q_ref[:, hs], wuk_ref[h]) * (ATT_HEAD_DIM ** -0.5)).astype(BF16)
        acc_refs[h][...] = jnp.zeros((KV_ROWS, tq), F32)

    def att_block(kb, carry):
        k0 = pl.multiple_of(kb * kb_len, kb_len)
        ckv = ckv_ref[pl.ds(k0, kb_len), :]
        ckvt = ckvt_ref[:, pl.ds(k0, kb_len)]
        dm = dm_ref[pl.ds(k0, kb_len), :]
        new = []
        qk = _dot_nt(ckv, qa_ref[0])
        for h in range(ATT_HEADS):
            m_run = carry[h]
            slope = 2.0 ** (-8.0 * (h + 1) / ATT_HEADS)
            lg = qk - slope * dm
            if h + 1 < ATT_HEADS:
                qk = _dot_nt(ckv, qa_ref[h + 1])
            m_new = jnp.maximum(m_run, jnp.max(fold8(lg, jnp.maximum), axis=0, keepdims=True))
            alpha = jnp.exp(m_run - m_new)
            p = jnp.exp(lg - m_new)
            new.append(m_new)
            acc_refs[h][...] = alpha * acc_refs[h][...] + _dot(ckvt, p.astype(BF16))
        return tuple(new)

    lax.fori_loop(0, n_kb, att_block, tuple(jnp.full((1, tq), -jnp.inf, F32) for _ in range(ATT_HEADS)))
    for h in range(ATT_HEADS):
        o_t = (acc_refs[h][0:KV_RANK, :] / acc_refs[h][KV_RANK:KV_RANK + 1, :]).astype(BF16)
        ot_ref[h * ATT_HEAD_DIM:(h + 1) * ATT_HEAD_DIM, :] = _dot(wuvt_ref[h], o_t)
    o_ref[...] = ot_ref[...].T.astype(o_ref.dtype)


def _dsa(q, qi, wi, ki, ckv, ckvt, w_uk, w_uv, bsz, seq, tq=256, kb_len=512):
    n = q.shape[0]
    nqt = seq // tq
    kb_len = min(kb_len, seq)
    topk = min(TOPK_MAX, seq // 4)
    tri = (jnp.arange(kb_len)[:, None] >= jnp.arange(kb_len)[None, :]).astype(BF16)
    qrow = lambda b, i: (b * nqt + i, 0)
    krow = lambda b, i: (b, 0)
    c3 = lambda b, i: (0, 0, 0)
    return pl.pallas_call(
        functools.partial(_dsa_kernel, topk=topk, kb_len=kb_len),
        grid=(bsz, nqt),
        in_specs=[pl.BlockSpec((tq, ATT_WIDTH), qrow),
                  pl.BlockSpec((tq, IDX_HEADS * IDX_DIM), qrow),
                  pl.BlockSpec((tq, LANES), qrow),
                  pl.BlockSpec((seq, IDX_DIM), krow),
                  pl.BlockSpec((seq, KV_RANK), krow),
                  pl.BlockSpec((KV_ROWS, seq), lambda b, i: (0, b)),
                  pl.BlockSpec((ATT_HEADS, ATT_HEAD_DIM, KV_RANK), c3),
                  pl.BlockSpec((ATT_HEADS, ATT_HEAD_DIM, KV_RANK), c3),
                  pl.BlockSpec((kb_len, kb_len), lambda b, i: (0, 0))],
        out_specs=pl.BlockSpec((tq, ATT_WIDTH), qrow),
        out_shape=jax.ShapeDtypeStruct((n, ATT_WIDTH), BF16),
        scratch_shapes=[pltpu.VMEM((seq, tq), F32),
                        pltpu.VMEM((seq, tq), BF16),
                        pltpu.VMEM((seq, tq), F32),
                        pltpu.VMEM((ATT_HEADS, tq, KV_RANK), BF16),
                        pltpu.VMEM((ATT_WIDTH, tq), F32)]
        + [pltpu.VMEM((KV_ROWS, tq), F32) for _ in range(ATT_HEADS)],
        compiler_params=_cparams("parallel", "parallel"),
        name="dsa_attention",
    )(q, qi, wi, ki, ckv, ckvt, w_uk.astype(BF16), jnp.swapaxes(w_uv, 1, 2).astype(BF16), tri)


def _out_kernel(ys_ref, yb_ref, yc_ref, x_ref, wglu_ref, bglu_ref, wo_ref, g1_ref,
                n2_ref, sc_ref, sh_ref, xo_ref, h2_ref):
    yg = jax.nn.gelu(ys_ref[...])
    z = _dot(yg.astype(BF16), wglu_ref[...]) + bglu_ref[...]
    ya = yg * jax.nn.sigmoid(z)
    a_w, b_w = SSM_WIDTH, SSM_WIDTH + GMLP_WIDTH
    mix = (_dot(ya.astype(BF16), wo_ref[0:a_w, :])
           + _dot(yb_ref[...], wo_ref[a_w:b_w, :])
           + _dot(yc_ref[...], wo_ref[b_w:, :]))
    xn = x_ref[...] + g1_ref[0] * mix
    xo_ref[...] = xn
    ms = jnp.mean(xn * xn, axis=-1, keepdims=True)
    h = xn * lax.rsqrt(ms + RMS_EPS) * n2_ref[...]
    h2_ref[...] = (h * (1.0 + sc_ref[0]) + sh_ref[0]).astype(BF16)


def _output_stage(ys, yb, yc, x2, seq, w_glu, b_glu, w_out, g1, n2g, sc2, sh2, tm=512):
    n, d = x2.shape
    tpb = seq // tm
    row = lambda i: (i, 0)
    const2 = lambda i: (0, 0)
    per_b = lambda i: (i // tpb, 0, 0)
    return pl.pallas_call(
        _out_kernel,
        grid=(n // tm,),
        in_specs=[pl.BlockSpec((tm, SSM_WIDTH), row),
                  pl.BlockSpec((tm, GMLP_WIDTH), row),
                  pl.BlockSpec((tm, ATT_WIDTH), row),
                  pl.BlockSpec((tm, d), row),
                  pl.BlockSpec((SSM_WIDTH, SSM_WIDTH), const2),
                  pl.BlockSpec((1, SSM_WIDTH), const2),
                  pl.BlockSpec((d, d), const2),
                  pl.BlockSpec((1, 1, d), per_b),
                  pl.BlockSpec((1, d), const2),
                  pl.BlockSpec((1, 1, d), per_b),
                  pl.BlockSpec((1, 1, d), per_b)],
        out_specs=[pl.BlockSpec((tm, d), row), pl.BlockSpec((tm, d), row)],
        out_shape=[jax.ShapeDtypeStruct((n, d), F32), jax.ShapeDtypeStruct((n, d), BF16)],
        compiler_params=_cparams("parallel"),
        name="out_proj_norm2",
    )(ys, yb, yc, x2, w_glu, b_glu, w_out, g1, n2g, sc2, sh2)


def _sort16_network():
    def merge(lo, hi, r):
        step = r * 2
        if step < hi - lo:
            yield from merge(lo, hi, step)
            yield from merge(lo + r, hi, step)
            yield from [(i, i + r) for i in range(lo + r, hi - r, step)]
        else:
            yield (lo, lo + r)

    def sort(lo, hi):
        if hi - lo >= 1:
            mid = lo + (hi - lo) // 2
            yield from sort(lo, mid)
            yield from sort(mid + 1, hi)
            yield from merge(lo, hi, 1)

    return tuple(sort(0, 15))


def _extract_top(x, n_top):
    tops = []
    for _ in range(n_top):
        mx = jnp.max(x, axis=0, keepdims=True)
        tops.append(mx)
        x = jnp.where(x == mx, -jnp.inf, x)
    return tops


def _extract_top_128(x, n_top):
    sub = x.shape[0] // 16
    rows = [x[i * sub:(i + 1) * sub, :] for i in range(16)]
    for a, b in _sort16_network():
        rows[a], rows[b] = jnp.maximum(rows[a], rows[b]), jnp.minimum(rows[a], rows[b])
    tops = []
    for k in range(n_top):
        mx = jnp.max(rows[0], axis=0, keepdims=True)
        tops.append(mx)
        need = min(16, n_top - 1 - k)
        if need:
            hit = rows[0] == mx
            below = rows[1:need + 1] + ([jnp.full_like(rows[0], -jnp.inf)] if need == 16 else [])
            for d in range(need):
                rows[d] = jnp.where(hit, below[d], rows[d])
    return tops


def _peer_kernel(h2_ref, x_ref, g2_ref, fg_ref, wq_ref, kbd_ref, *rest, final_norm, n_split):
    u_refs, vt_refs = rest[:n_split], rest[n_split:2 * n_split]
    o_ref, st_ref, e1_ref, e2_ref, r2_ref, acc_ref, at_ref, ga_ref, v1_ref, v2_ref = rest[2 * n_split:]
    j = pl.program_id(1)
    tm = h2_ref.shape[0]
    u_rows = u_refs[0].shape[0]
    v_rows = vt_refs[0].shape[0]
    te = u_rows * n_split
    n1 = te // PEER_N_KEYS
    nk = PEER_N_KEYS
    h2 = h2_ref[...]

    @pl.when(j == 0)
    def _():
        q = _dot(h2, wq_ref[...]).astype(BF16)
        st_ref[...] = _dot_nt(kbd_ref[...], q)

        def per_head(h, carry):
            base = pl.multiple_of(h * 2 * nk, 2 * nk)
            for lc in range(tm // LANES):
                ls = slice(lc * LANES, (lc + 1) * LANES)
                t1 = _extract_top_128(st_ref[pl.ds(base, nk), ls], PEER_TOPK + 1)
                t2 = _extract_top_128(st_ref[pl.ds(base + nk, nk), ls], PEER_TOPK + 1)
                for k in range(PEER_TOPK):
                    v1_ref[k:k + 1, :] = t1[k]
                    v2_ref[k:k + 1, :] = t2[k]
                v2_all = v2_ref[0:PEER_TOPK, :]
                v2_top = v2_ref[0:8, :]
                cand = [t1[0] + v2_all]
                cand += [t1[a] + v2_top for a in range(1, 8)]
                cand.append(v1_ref[8:PEER_TOPK, :] + t2[0])
                top = _extract_top(jnp.concatenate(cand, axis=0), PEER_TOPK + 1)
                z = jnp.zeros_like(top[0])
                for k in range(PEER_TOPK):
                    z = z + jnp.exp(top[k] - top[0])
                nxt = jnp.maximum(top[PEER_TOPK], jnp.maximum(t1[PEER_TOPK] + t2[0], t1[0] + t2[PEER_TOPK]))
                thr = 0.5 * (top[PEER_TOPK - 1] + nxt)
                s1 = st_ref[pl.ds(base, nk), ls]
                s2 = st_ref[pl.ds(base + nk, nk), ls]
                e1_ref[h, :, ls] = jnp.exp(s1 - t1[0]) * (0.5 / z)
                e2_ref[h, :, ls] = jnp.exp(s2 - t2[0]).astype(BF16)
                rank2 = jnp.zeros((nk, LANES), F32)
                for b in range(PEER_TOPK):
                    rank2 = rank2 + jnp.where(s2 < t2[b], 1.0, 0.0)
                r2_ref[h, :, ls] = rank2.astype(BF16)
                n_sel = jnp.zeros((nk, LANES), F32)
                for a in range(PEER_TOPK):
                    n_a = jnp.zeros_like(thr)
                    for b in range(PEER_TOPK // (a + 1)):
                        n_a = n_a + jnp.where(t1[a] + t2[b] >= thr, 1.0, 0.0)
                    n_sel = jnp.where(s1 == t1[a], n_a, n_sel)
                st_ref[pl.ds(base, nk), ls] = n_sel
            return carry

        lax.fori_loop(0, PEER_HEADS, per_head, 0)
        acc_ref[...] = jnp.zeros_like(acc_ref)
        ga_ref[1] = jnp.zeros(ga_ref.shape[1:], BF16)
        for k in range(n_split):
            at_ref[0, k * u_rows:(k + 1) * u_rows, :] = _dot_nt(u_refs[k][...], h2)

    n_blocks = pl.num_programs(1) - 2
    rd = j % 2
    wr = 1 - rd

    @pl.when((j >= 1) & (j <= n_blocks))
    def _():
        for k in range(n_split):
            at_ref[rd, k * u_rows:(k + 1) * u_rows, :] = _dot_nt(u_refs[k][...], h2)
        for k in range(n_split):
            acc_ref[k * v_rows:(k + 1) * v_rows, :] += _dot(vt_refs[k][...], ga_ref[rd])
        r0 = pl.multiple_of((j - 1) * n1, n1)
        for lc in range(tm // LANES):
            ls = slice(lc * LANES, (lc + 1) * LANES)
            for il in range(n1):
                g = jnp.zeros((nk, LANES), BF16)
                for h in range(PEER_HEADS):
                    n_row = st_ref[pl.ds(h * 2 * nk + r0, n1), ls][il:il + 1, :].astype(BF16)
                    e_row = e1_ref[h, pl.ds(r0, n1), ls][il:il + 1, :].astype(BF16)
                    picked = jnp.where(r2_ref[h, :, ls] < n_row, e2_ref[h, :, ls], jnp.zeros((), BF16))
                    g = g + picked * e_row
                a = at_ref[wr, il * nk:(il + 1) * nk, ls]
                inner = a * (0.7978845608028654 + 0.035677408136300125 * (a * a))
                act2 = a + a * jnp.tanh(inner)
                ga_ref[wr, il * nk:(il + 1) * nk, ls] = g * act2.astype(BF16)

    @pl.when(j == n_blocks + 1)
    def _():
        for k in range(n_split):
            acc_ref[k * v_rows:(k + 1) * v_rows, :] += _dot(vt_refs[k][...], ga_ref[rd])
        out = x_ref[...] + g2_ref[0] * acc_ref[...].T
        if final_norm:
            ms = jnp.mean(out * out, axis=-1, keepdims=True)
            out = out * lax.rsqrt(ms + RMS_EPS) * fg_ref[...]
        o_ref[...] = out


def _peer(h2, x2, seq, g2, fg, wq, kbd, u_bf, vt_bf, final_norm, tm=512, te=1024, n_split=4):
    n, d = x2.shape
    n_blk = u_bf.shape[0] // te
    tpb = seq // tm
    trow = lambda i, j: (i, 0)
    c2 = lambda i, j: (0, 0)
    hs = (PEER_HEADS, PEER_N_KEYS, tm)
    u_map = lambda k, i, j: (jnp.minimum(j, n_blk - 1) * n_split + k, 0)
    v_map = lambda k, i, j: (k, jnp.clip(j - 2, 0, n_blk - 1))
    return pl.pallas_call(
        functools.partial(_peer_kernel, final_norm=final_norm, n_split=n_split),
        grid=(n // tm, n_blk + 2),
        in_specs=[pl.BlockSpec((tm, d), trow),
                  pl.BlockSpec((tm, d), trow),
                  pl.BlockSpec((1, 1, d), lambda i, j: (i // tpb, 0, 0)),
                  pl.BlockSpec((1, d), c2),
                  pl.BlockSpec((d, PEER_HEADS * 2 * PEER_HALF), c2),
                  pl.BlockSpec((PEER_HEADS * 2 * PEER_N_KEYS, PEER_HEADS * 2 * PEER_HALF), c2)]
        + [pl.BlockSpec((te // n_split, d), functools.partial(u_map, k)) for k in range(n_split)]
        + [pl.BlockSpec((d // n_split, te), functools.partial(v_map, k)) for k in range(n_split)],
        out_specs=pl.BlockSpec((tm, d), trow),
        out_shape=jax.ShapeDtypeStruct((n, d), F32),
        scratch_shapes=[pltpu.VMEM((PEER_HEADS * 2 * PEER_N_KEYS, tm), F32),
                        pltpu.VMEM(hs, F32), pltpu.VMEM(hs, BF16), pltpu.VMEM(hs, BF16),
                        pltpu.VMEM((d, tm), F32),
                        pltpu.VMEM((2, te, tm), F32),
                        pltpu.VMEM((2, te, tm), BF16),
                        pltpu.VMEM((PEER_TOPK, LANES), F32), pltpu.VMEM((PEER_TOPK, LANES), F32)],
        compiler_params=_cparams("parallel", "arbitrary"),
        name="peer_dense",
    )(h2, x2, g2, fg, wq, kbd, *([u_bf] * n_split), *([vt_bf] * n_split))


def _peer_key_matrix(k1, k2):
    blocks = []
    for h in range(PEER_HEADS):
        for half, kk in enumerate((k1, k2)):
            col = (2 * h + half) * PEER_HALF
            blocks.append(jnp.pad(kk, ((0, 0), (col, PEER_HEADS * 2 * PEER_HALF - col - PEER_HALF))))
    return jnp.concatenate(blocks, axis=0).astype(BF16)


def kernel(x, c, norm1_g, norm2_g, w_mod, b_mod, w_in, ssm_a_re_log, ssm_a_im, ssm_b_re, ssm_b_im, ssm_c_re, ssm_c_im, ssm_d, ssm_log_dt, ssm_w_glu, ssm_b_glu, gmlp_w_sp, gmlp_b_sp, kv_norm_g, w_uk, w_uv, w_out, peer_w_q, peer_k1, peer_k2, peer_u, peer_v, final_g):
    bsz, seq, d = x.shape
    depth = w_mod.shape[0]
    x2 = x.reshape(bsz * seq, d)
    head_of = jnp.arange(GMLP_WIDTH) // GMLP_HEAD_DIM
    pavg = ((head_of[:, None] == head_of[None, :]).astype(F32) / GMLP_HEAD_DIM).astype(BF16)
    chunk_of = jnp.arange(GMLP_BLOCK) // CHUNK
    sp_mask = chunk_of[:, None] >= chunk_of[None, :]
    for l in range(depth):
        mod = _modulation(c, w_mod[l], b_mod[l])
        sh1, sc1, g1, sh2, sc2, g2 = [mod[:, i * d:(i + 1) * d].reshape(bsz, 1, d) for i in range(6)]
        w_in_pad = jnp.pad(w_in[l], ((0, 0), (0, IN_PAD - IN_WIDTH))).astype(BF16)
        wsp = jnp.where(sp_mask[None], gmlp_w_sp[l], 0.0).astype(BF16)
        bsp = jnp.repeat(gmlp_b_sp[l].T, GMLP_HEAD_DIM, axis=1)
        ua, yb, q, ckv, qi, ki, wi, ckvt = _input_stage(
            x2, seq, norm1_g[l].reshape(1, d), sc1, sh1, w_in_pad, kv_norm_g[l].reshape(1, KV_RANK),
            pavg, wsp, bsp)
        prep = _ssm_prep_tm(ssm_a_re_log[l], ssm_a_im[l], ssm_b_re[l], ssm_b_im[l], ssm_c_re[l], ssm_c_im[l],
                            ssm_log_dt[l], SSM_T)
        ys = _ssm_scan_tm(ua, bsz, seq, prep, ssm_d[l].reshape(1, SSM_WIDTH))
        yc = _dsa(q, qi, wi, ki, ckv, ckvt, w_uk[l], w_uv[l], bsz, seq)
        x2, h2 = _output_stage(
            ys, yb, yc, x2, seq, ssm_w_glu[l].astype(BF16),
            ssm_b_glu[l].reshape(1, SSM_WIDTH), w_out[l].astype(BF16), g1, norm2_g[l].reshape(1, d), sc2, sh2)
        x2 = _peer(h2, x2, seq, g2, final_g.reshape(1, d), peer_w_q[l].astype(BF16),
                   _peer_key_matrix(peer_k1[l], peer_k2[l]), peer_u[l].astype(BF16),
                   peer_v[l].T.astype(BF16), final_norm=(l == depth - 1))
    return x2.reshape(bsz, seq, d)
```

```python
import functools
import math

import jax
import jax.numpy as jnp
from jax import lax
from jax.experimental import pallas as pl
from jax.experimental.pallas import tpu as pltpu

CHUNK = 64
RMS_EPS = 1e-6

SSM_GROUP = 16
SSM_STATE = 64
SSM_WIDTH = 256
SSM_GROUPS = SSM_WIDTH // SSM_GROUP
SSM_T = 16
GMLP_HEADS = 4
GMLP_HEAD_DIM = 64
GMLP_WIDTH = GMLP_HEADS * GMLP_HEAD_DIM
GMLP_BLOCK = 128
ATT_HEADS = 8
ATT_HEAD_DIM = 64
ATT_WIDTH = ATT_HEADS * ATT_HEAD_DIM
KV_RANK = 128
KV_ROWS = KV_RANK + 8
IDX_HEADS = 4
IDX_DIM = 64
TOPK_MAX = 256

IN_SIZES = (SSM_WIDTH, GMLP_WIDTH, GMLP_WIDTH, ATT_WIDTH, KV_RANK, IDX_HEADS * IDX_DIM, IDX_DIM, IDX_HEADS)
IN_WIDTH = sum(IN_SIZES)
IN_PAD = 1792
COL_UB, COL_VB, COL_Q, COL_CKV, COL_QI, COL_KW = 256, 512, 768, 1280, 1408, 1664

PEER_HEADS = 8
PEER_N_KEYS = 128
PEER_HALF = 64
PEER_TOPK = 16

LANES = 128
VMEM_LIMIT = 56 * 1024 * 1024
NEG = -1e30
MIN_NORMAL = 2.0 ** -126

F32 = jnp.float32
BF16 = jnp.bfloat16


def _cparams(*sem):
    return pltpu.CompilerParams(dimension_semantics=sem, vmem_limit_bytes=VMEM_LIMIT)


def _dot(a, b):
    return jnp.dot(a, b, preferred_element_type=F32)


def _dot_nt(a, b):
    return lax.dot_general(a, b, (((1,), (1,)), ((), ())), preferred_element_type=F32)


def _mod_kernel(c_ref, w_ref, b_ref, o_ref):
    c = c_ref[...]
    ca = c * jax.nn.sigmoid(c)
    o_ref[...] = _dot(ca.astype(BF16), w_ref[...].astype(BF16)) + b_ref[...]


def _modulation(c, w_mod, b_mod):
    bsz, d = c.shape
    n6 = w_mod.shape[1]
    return pl.pallas_call(
        _mod_kernel,
        grid=(n6 // d,),
        in_specs=[pl.BlockSpec((bsz, d), lambda j: (0, 0)),
                  pl.BlockSpec((d, d), lambda j: (0, j)),
                  pl.BlockSpec((1, d), lambda j: (0, j))],
        out_specs=pl.BlockSpec((bsz, d), lambda j: (0, j)),
        out_shape=jax.ShapeDtypeStruct((bsz, n6), F32),
        compiler_params=_cparams("parallel"),
        name="modulation",
    )(c, w_mod, b_mod.reshape(1, n6))


def _in_kernel(x_ref, g_ref, sc_ref, sh_ref, w_ref, kvg_ref, pavg_ref, wsp_ref, bsp_ref,
               ua_ref, yb_ref, q_ref, ckv_ref, qi_ref, ki_ref, wi_ref, ckvt_ref):
    tm = x_ref.shape[0]
    x = x_ref[...]
    ms = jnp.mean(x * x, axis=-1, keepdims=True)
    h = x * lax.rsqrt(ms + RMS_EPS) * g_ref[...]
    h = h * (1.0 + sc_ref[0]) + sh_ref[0]
    proj = _dot(h.astype(BF16), w_ref[...])

    ua_ref[...] = proj[:, 0:COL_UB]
    q_ref[...] = proj[:, COL_Q:COL_CKV].astype(BF16)
    ckv = proj[:, COL_CKV:COL_QI]
    ckv_ms = jnp.mean(ckv * ckv, axis=-1, keepdims=True)
    ckv_n = ckv * lax.rsqrt(ckv_ms + RMS_EPS) * kvg_ref[...]
    ckv_ref[...] = ckv_n.astype(BF16)
    ckvt_ref[0:KV_RANK, :] = ckv_n.T.astype(BF16)
    ones_row = lax.broadcasted_iota(jnp.int32, (KV_ROWS - KV_RANK, tm), 0) == 0
    ckvt_ref[KV_RANK:, :] = jnp.where(ones_row, 1.0, 0.0).astype(BF16)
    qi_ref[...] = proj[:, COL_QI:COL_KW].astype(BF16)
    kw = proj[:, COL_KW:IN_PAD]
    ki_ref[...] = kw[:, 0:IDX_DIM].astype(BF16)
    wi_ref[...] = kw

    u_b = proj[:, COL_UB:COL_VB]
    v_b = proj[:, COL_VB:COL_Q]
    pavg = pavg_ref[...]

    def head_mean(a):
        hi = a.astype(BF16)
        lo = (a - hi.astype(F32)).astype(BF16)
        return _dot(hi, pavg) + _dot(lo, pavg)

    mu = head_mean(v_b)
    dv = v_b - mu
    var = head_mean(dv * dv)
    vn = (dv * lax.rsqrt(var + RMS_EPS)).astype(BF16)
    lane_head = lax.broadcasted_iota(jnp.int32, (GMLP_BLOCK, GMLP_WIDTH), 1) // GMLP_HEAD_DIM
    for blk in range(tm // GMLP_BLOCK):
        rows = slice(blk * GMLP_BLOCK, (blk + 1) * GMLP_BLOCK)
        vblk = vn[rows, :]
        mixed = bsp_ref[...]
        for hh in range(GMLP_HEADS):
            mixed = mixed + jnp.where(lane_head == hh, _dot(wsp_ref[hh], vblk), 0.0)
        yb_ref[rows, :] = (u_b[rows, :] * mixed).astype(BF16)


def _input_stage(x2, seq, g, sc, sh, w_in_pad, kvg, pavg, wsp, bsp, tm=512):
    n, d = x2.shape
    tpb = seq // tm
    row = lambda i: (i, 0)
    const2 = lambda i: (0, 0)
    per_b = lambda i: (i // tpb, 0, 0)
    outs = [(SSM_WIDTH, F32), (GMLP_WIDTH, BF16), (ATT_WIDTH, BF16), (KV_RANK, BF16),
            (IDX_HEADS * IDX_DIM, BF16), (IDX_DIM, BF16), (LANES, F32)]
    return pl.pallas_call(
        _in_kernel,
        grid=(n // tm,),
        in_specs=[pl.BlockSpec((tm, d), row),
                  pl.BlockSpec((1, d), const2),
                  pl.BlockSpec((1, 1, d), per_b),
                  pl.BlockSpec((1, 1, d), per_b),
                  pl.BlockSpec((d, IN_PAD), const2),
                  pl.BlockSpec((1, KV_RANK), const2),
                  pl.BlockSpec((GMLP_WIDTH, GMLP_WIDTH), const2),
                  pl.BlockSpec((GMLP_HEADS, GMLP_BLOCK, GMLP_BLOCK), lambda i: (0, 0, 0)),
                  pl.BlockSpec((GMLP_BLOCK, GMLP_WIDTH), const2)],
        out_specs=[pl.BlockSpec((tm, w), row) for w, _ in outs] + [pl.BlockSpec((KV_ROWS, tm), lambda i: (0, i))],
        out_shape=[jax.ShapeDtypeStruct((n, w), dt) for w, dt in outs] + [jax.ShapeDtypeStruct((KV_ROWS, n), BF16)],
        compiler_params=_cparams("parallel"),
        name="input_proj_gmlp",
    )(x2, g, sc, sh, w_in_pad, kvg, pavg, wsp, bsp)


def _ssm_prep_tm(a_re_log, a_im, b_re, b_im, c_re, c_im, log_dt, t_len):
    hp = lax.Precision.HIGHEST
    g_n, h_n, p_n = SSM_GROUPS, SSM_GROUP, SSM_STATE
    lam_re = -jnp.exp(a_re_log)
    lam_im = a_im
    dt = jnp.exp(log_dt)[:, None]
    mag = jnp.exp(lam_re * dt)
    abar_re = mag * jnp.cos(lam_im * dt)
    abar_im = mag * jnp.sin(lam_im * dt)
    den = lam_re * lam_re + lam_im * lam_im
    p = abar_re - 1.0
    qq = abar_im
    f_re = (p * lam_re + qq * lam_im) / den
    f_im = (qq * lam_re - p * lam_im) / den
    bb_re = f_re[..., None] * b_re - f_im[..., None] * b_im
    bb_im = f_re[..., None] * b_im + f_im[..., None] * b_re
    k = jnp.arange(t_len + 1, dtype=F32)[:, None, None]
    pmag = jnp.exp(k * (lam_re * dt))
    ang = k * (lam_im * dt)
    pw_re = pmag * jnp.cos(ang)
    pw_im = pmag * jnp.sin(ang)
    cp_re = c_re[None] * pw_re[:, :, None, :] - c_im[None] * pw_im[:, :, None, :]
    cp_im = c_re[None] * pw_im[:, :, None, :] + c_im[None] * pw_re[:, :, None, :]
    kern = (jnp.einsum('kghp,gpj->kghj', cp_re[:t_len], bb_re, precision=hp)
            - jnp.einsum('kghp,gpj->kghj', cp_im[:t_len], bb_im, precision=hp))
    grp_w = jnp.arange(g_n * h_n) // h_n
    grp_s = jnp.arange(g_n * p_n) // p_n
    kb = jnp.tile(kern.transpose(0, 1, 3, 2).reshape(t_len, g_n * h_n, h_n), (1, 1, g_n))
    kb = jnp.where((grp_w[:, None] == grp_w[None, :])[None], kb, 0.0)
    in_mask = grp_w[:, None] == grp_s[None, :]
    b_in = jnp.concatenate(
        [jnp.where(in_mask, jnp.tile(m.transpose(0, 2, 1).reshape(g_n * h_n, p_n), (1, g_n)), 0.0)
         for m in (bb_re, bb_im)], axis=1)
    c_out = jnp.concatenate(
        [jnp.where(in_mask.T, jnp.tile(m.transpose(0, 2, 1).reshape(g_n * p_n, h_n), (1, g_n)), 0.0)
         for m in (c_re, -c_im)], axis=0)
    tt = jnp.arange(t_len)
    flat = lambda a: a.reshape(a.shape[0], g_n * p_n)
    pw_end = jnp.concatenate([flat(pw_re[t_len - 1 - tt]), flat(pw_im[t_len - 1 - tt])], axis=1)
    pw_nxt = jnp.concatenate([flat(pw_re[1 + tt]), flat(pw_im[1 + tt])], axis=1)
    a_t = jnp.concatenate([flat(pw_re[t_len:]), flat(pw_im[t_len:])], axis=1)
    return kb.astype(BF16), b_in.astype(BF16), c_out.astype(BF16), pw_end, pw_nxt, a_t


def _ssm_tm_kernel(u_ref, kb_ref, bin_ref, cout_ref, pwe_ref, pwn_ref, at_ref, d_ref, y_ref,
                   ubuf_ref, sp_ref, st_ref, *, t_len):
    tm, width = u_ref.shape
    n_st = at_ref.shape[1] // 2
    nch = tm // t_len

    @pl.when(pl.program_id(1) == 0)
    def _():
        st_ref[...] = jnp.zeros_like(st_ref)

    u = u_ref[...]
    ubuf_ref[0:t_len, :] = jnp.zeros((t_len, width), F32)
    ubuf_ref[t_len:, :] = u
    pos = lax.broadcasted_iota(jnp.int32, (tm, 1), 0) % t_len
    y = _dot(u.astype(BF16), kb_ref[0])
    for k in range(1, t_len):
        shifted = ubuf_ref[t_len - k:t_len - k + tm, :]
        y = y + _dot(jnp.where(pos >= k, shifted, 0.0).astype(BF16), kb_ref[k])

    bu = _dot(u.astype(BF16), bin_ref[...]).reshape(nch, t_len, 2 * n_st)
    b_re, b_im = bu[:, :, :n_st], bu[:, :, n_st:]
    w_re, w_im = pwe_ref[:, :n_st][None], pwe_ref[:, n_st:][None]
    loc_re = jnp.sum(b_re * w_re - b_im * w_im, axis=1)
    loc_im = jnp.sum(b_re * w_im + b_im * w_re, axis=1)

    a_re, a_im = at_ref[:, :n_st], at_ref[:, n_st:]
    s_re, s_im = st_ref[:, :n_st], st_ref[:, n_st:]
    for c in range(nch):
        sp_ref[c:c + 1, :n_st] = s_re
        sp_ref[c:c + 1, n_st:] = s_im
        s_re, s_im = (a_re * s_re - a_im * s_im + loc_re[c:c + 1, :],
                      a_re * s_im + a_im * s_re + loc_im[c:c + 1, :])
    st_ref[:, :n_st] = s_re
    st_ref[:, n_st:] = s_im

    p_re, p_im = sp_ref[:, :n_st][:, None, :], sp_ref[:, n_st:][:, None, :]
    n_re, n_im = pwn_ref[:, :n_st][None], pwn_ref[:, n_st:][None]
    e_re = (p_re * n_re - p_im * n_im).reshape(tm, n_st).astype(BF16)
    e_im = (p_re * n_im + p_im * n_re).reshape(tm, n_st).astype(BF16)
    y = y + _dot(e_re, cout_ref[:n_st, :]) + _dot(e_im, cout_ref[n_st:, :])
    y_ref[...] = y + d_ref[...] * u


def _ssm_scan_tm(u_a, bsz, seq, prep, d_skip, tm=512):
    kb, b_in, c_out, pw_end, pw_nxt, a_t = prep
    t_len = kb.shape[0]
    n, width = u_a.shape
    tpb = seq // tm
    row = lambda b, i: (b * tpb + i, 0)
    c2 = lambda b, i: (0, 0)
    n2 = b_in.shape[1]
    return pl.pallas_call(
        functools.partial(_ssm_tm_kernel, t_len=t_len),
        grid=(bsz, tpb),
        in_specs=[pl.BlockSpec((tm, width), row),
                  pl.BlockSpec((t_len, width, width), lambda b, i: (0, 0, 0)),
                  pl.BlockSpec((width, n2), c2),
                  pl.BlockSpec((n2, width), c2),
                  pl.BlockSpec((t_len, n2), c2),
                  pl.BlockSpec((t_len, n2), c2),
                  pl.BlockSpec((1, n2), c2),
                  pl.BlockSpec((1, width), c2)],
        out_specs=pl.BlockSpec((tm, width), row),
        out_shape=jax.ShapeDtypeStruct((n, width), F32),
        scratch_shapes=[pltpu.VMEM((t_len + tm, width), F32),
                        pltpu.VMEM((tm // t_len, n2), F32),
                        pltpu.VMEM((1, n2), F32)],
        compiler_params=_cparams("parallel", "arbitrary"),
        name="ssm_scan",
    )(u_a, kb, b_in, c_out, pw_end, pw_nxt, a_t, d_skip)


def _dsa_kernel(q_ref, qi_ref, wi_ref, ki_ref, ckv_ref, ckvt_ref, wuk_ref, wuvt_ref, tri_ref, o_ref,
                key_ref, hi_ref, dm_ref, qa_ref, ot_ref, *acc_refs, topk, kb_len):
    tq = q_ref.shape[0]
    t0 = pl.program_id(1) * tq
    n_kb = (t0 + tq + kb_len - 1) // kb_len
    q_pos = t0 + lax.broadcasted_iota(jnp.int32, (1, tq), 1)
    q_chunk = q_pos // CHUNK
    k_off = lax.broadcasted_iota(jnp.int32, (kb_len, 1), 0)

    def fold8(m, op):
        r = m.reshape(kb_len // 8, 8, tq)
        n = kb_len // 8
        while n > 1:
            n //= 2
            r = op(r[:n], r[n:2 * n])
        return r[0]

    def col_sum(m):
        return fold8(m, jnp.add)

    wi_t = wi_ref[...].T
    qi_h = [qi_ref[:, h * IDX_DIM:(h + 1) * IDX_DIM] for h in range(IDX_HEADS)]
    w_h = [wi_t[IDX_DIM + h:IDX_DIM + h + 1, :] * ((IDX_HEADS ** -0.5) * (IDX_DIM ** -0.5))
           for h in range(IDX_HEADS)]

    def score_block(kb, carry):
        k0 = pl.multiple_of(kb * kb_len, kb_len)
        ki = ki_ref[pl.ds(k0, kb_len), :]
        score = jnp.zeros((kb_len, tq), F32)
        for h in range(IDX_HEADS):
            score = score + jnp.maximum(_dot_nt(ki, qi_h[h]), 0.0) * w_h[h]
        score = jnp.where(jnp.abs(score) < MIN_NORMAL, 0.0, score)
        adm = ((k0 + k_off) // CHUNK) <= q_chunk
        score = jnp.where(adm, score, -jnp.inf)
        key_ref[pl.ds(k0, kb_len), :] = score
        hi_ref[pl.ds(k0, kb_len), :] = score.astype(BF16)
        return carry

    lax.fori_loop(0, n_kb, score_block, 0)

    def count(pred):
        def body(kb, acc):
            k0 = pl.multiple_of(kb * kb_len, kb_len)
            return acc + col_sum(pred(key_ref[pl.ds(k0, kb_len), :]).astype(F32))
        acc = lax.fori_loop(0, n_kb, body, jnp.zeros((8, tq), F32))
        return jnp.sum(acc, axis=0, keepdims=True)

    kf = jnp.float32(topk)
    code_neg_inf = jnp.int32(0xFF800000 - (1 << 32)) ^ jnp.int32(0x7FFFFFFF)

    def code_to_float(code):
        code = jnp.maximum(code, code_neg_inf)
        code = jnp.where((code > 0) & (code < 0x00800000), jnp.int32(0x00800000), code)
        return lax.bitcast_convert_type(jnp.where(code >= 0, code, code ^ jnp.int32(0x7FFFFFFF)), F32)

    def count_upper(trial_bf):
        def body(kb, acc):
            k0 = pl.multiple_of(kb * kb_len, kb_len)
            hit = jnp.where(hi_ref[pl.ds(k0, kb_len), :] >= trial_bf, jnp.ones((), BF16), jnp.zeros((), BF16))
            r = hit.reshape(kb_len // 16, 16, tq)
            n = kb_len // 16
            while n > 1:
                n //= 2
                r = r[:n] + r[n:2 * n]
            return acc + r[0]
        acc = lax.fori_loop(0, n_kb, body, jnp.zeros((16, tq), BF16))
        return jnp.sum(acc.astype(F32), axis=0, keepdims=True)

    def upper_step(step, up):
        trial = up + jnp.left_shift(jnp.int32(1), 15 - step)
        trial_bf = code_to_float(jnp.left_shift(trial, 16)).astype(BF16)
        return jnp.where(count_upper(trial_bf) >= kf, trial, up)

    upper_half = lax.fori_loop(0, 16, upper_step, jnp.full((1, tq), -32768, jnp.int32))

    def lower_step(step, code):
        trial = code + jnp.left_shift(jnp.int32(1), 16 - step)
        trial_f = code_to_float(trial)
        return jnp.where(count(lambda sblk: sblk >= trial_f) >= kf, trial, code)

    window_lo = jnp.left_shift(upper_half, 16) - jnp.int32(0x8001)
    ans = code_to_float(lax.fori_loop(0, 17, lower_step, window_lo))

    room = kf - count(lambda kblk: kblk > ans)
    tri = tri_ref[...]

    def select_block(kb, carry):
        k0 = pl.multiple_of(kb * kb_len, kb_len)
        kblk = key_ref[pl.ds(k0, kb_len), :]
        eq = kblk == ans
        pre = _dot(tri, eq.astype(BF16)) + carry
        sel = (kblk > ans) | (eq & (pre <= room))
        k_pos = k0 + k_off
        adm = (k_pos // CHUNK) <= q_chunk
        dist = jnp.abs(q_pos - k_pos).astype(F32)
        dm_ref[pl.ds(k0, kb_len), :] = jnp.where(sel & adm, dist, -NEG)
        return carry + jnp.sum(col_sum(eq.astype(F32)), axis=0, keepdims=True)

    lax.fori_loop(0, n_kb, select_block, jnp.zeros((1, tq), F32))

    for h in range(ATT_HEADS):
        hs = slice(h * ATT_HEAD_DIM, (h + 1) * ATT_HEAD_DIM)
        qa_ref[h] = (_dot(q_ref[:, hs], wuk_ref[h]) * (ATT_HEAD_DIM ** -0.5)).astype(BF16)
        acc_refs[h][...] = jnp.zeros((KV_ROWS, tq), F32)

    def att_block(kb, carry):
        k0 = pl.multiple_of(kb * kb_len, kb_len)
        ckv = ckv_ref[pl.ds(k0, kb_len), :]
        ckvt = ckvt_ref[:, pl.ds(k0, kb_len)]
        dm = dm_ref[pl.ds(k0, kb_len), :]
        new = []
        qk = _dot_nt(ckv, qa_ref[0])
        for h in range(ATT_HEADS):
            m_run = carry[h]
            slope = 2.0 ** (-8.0 * (h + 1) / ATT_HEADS)
            lg = qk - slope * dm
            if h + 1 < ATT_HEADS:
                qk = _dot_nt(ckv, qa_ref[h + 1])
            m_new = jnp.maximum(m_run, jnp.max(fold8(lg, jnp.maximum), axis=0, keepdims=True))
            alpha = jnp.exp(m_run - m_new)
            p = jnp.exp(lg - m_new)
            new.append(m_new)
            acc_refs[h][...] = alpha * acc_refs[h][...] + _dot(ckvt, p.astype(BF16))
        return tuple(new)

    lax.fori_loop(0, n_kb, att_block, tuple(jnp.full((1, tq), -jnp.inf, F32) for _ in range(ATT_HEADS)))
    for h in range(ATT_HEADS):
        o_t = (acc_refs[h][0:KV_RANK, :] / acc_refs[h][KV_RANK:KV_RANK + 1, :]).astype(BF16)
        ot_ref[h * ATT_HEAD_DIM:(h + 1) * ATT_HEAD_DIM, :] = _dot(wuvt_ref[h], o_t)
    o_ref[...] = ot_ref[...].T.astype(o_ref.dtype)


def _dsa(q, qi, wi, ki, ckv, ckvt, w_uk, w_uv, bsz, seq, tq=256, kb_len=512):
    n = q.shape[0]
    nqt = seq // tq
    kb_len = min(kb_len, seq)
    topk = min(TOPK_MAX, seq // 4)
    tri = (jnp.arange(kb_len)[:, None] >= jnp.arange(kb_len)[None, :]).astype(BF16)
    qrow = lambda b, i: (b * nqt + i, 0)
    krow = lambda b, i: (b, 0)
    c3 = lambda b, i: (0, 0, 0)
    return pl.pallas_call(
        functools.partial(_dsa_kernel, topk=topk, kb_len=kb_len),
        grid=(bsz, nqt),
        in_specs=[pl.BlockSpec((tq, ATT_WIDTH), qrow),
                  pl.BlockSpec((tq, IDX_HEADS * IDX_DIM), qrow),
                  pl.BlockSpec((tq, LANES), qrow),
                  pl.BlockSpec((seq, IDX_DIM), krow),
                  pl.BlockSpec((seq, KV_RANK), krow),
                  pl.BlockSpec((KV_ROWS, seq), lambda b, i: (0, b)),
                  pl.BlockSpec((ATT_HEADS, ATT_HEAD_DIM, KV_RANK), c3),
                  pl.BlockSpec((ATT_HEADS, ATT_HEAD_DIM, KV_RANK), c3),
                  pl.BlockSpec((kb_len, kb_len), lambda b, i: (0, 0))],
        out_specs=pl.BlockSpec((tq, ATT_WIDTH), qrow),
        out_shape=jax.ShapeDtypeStruct((n, ATT_WIDTH), BF16),
        scratch_shapes=[pltpu.VMEM((seq, tq), F32),
                        pltpu.VMEM((seq, tq), BF16),
                        pltpu.VMEM((seq, tq), F32),
                        pltpu.VMEM((ATT_HEADS, tq, KV_RANK), BF16),
                        pltpu.VMEM((ATT_WIDTH, tq), F32)]
        + [pltpu.VMEM((KV_ROWS, tq), F32) for _ in range(ATT_HEADS)],
        compiler_params=_cparams("parallel", "parallel"),
        name="dsa_attention",
    )(q, qi, wi, ki, ckv, ckvt, w_uk.astype(BF16), jnp.swapaxes(w_uv, 1, 2).astype(BF16), tri)


def _out_kernel(ys_ref, yb_ref, yc_ref, x_ref, wglu_ref, bglu_ref, wo_ref, g1_ref,
                n2_ref, sc_ref, sh_ref, xo_ref, h2_ref):
    yg = jax.nn.gelu(ys_ref[...])
    z = _dot(yg.astype(BF16), wglu_ref[...]) + bglu_ref[...]
    ya = yg * jax.nn.sigmoid(z)
    a_w, b_w = SSM_WIDTH, SSM_WIDTH + GMLP_WIDTH
    mix = (_dot(ya.astype(BF16), wo_ref[0:a_w, :])
           + _dot(yb_ref[...], wo_ref[a_w:b_w, :])
           + _dot(yc_ref[...], wo_ref[b_w:, :]))
    xn = x_ref[...] + g1_ref[0] * mix
    xo_ref[...] = xn
    ms = jnp.mean(xn * xn, axis=-1, keepdims=True)
    h = xn * lax.rsqrt(ms + RMS_EPS) * n2_ref[...]
    h2_ref[...] = (h * (1.0 + sc_ref[0]) + sh_ref[0]).astype(BF16)


def _output_stage(ys, yb, yc, x2, seq, w_glu, b_glu, w_out, g1, n2g, sc2, sh2, tm=512):
    n, d = x2.shape
    tpb = seq // tm
    row = lambda i: (i, 0)
    const2 = lambda i: (0, 0)
    per_b = lambda i: (i // tpb, 0, 0)
    return pl.pallas_call(
        _out_kernel,
        grid=(n // tm,),
        in_specs=[pl.BlockSpec((tm, SSM_WIDTH), row),
                  pl.BlockSpec((tm, GMLP_WIDTH), row),
                  pl.BlockSpec((tm, ATT_WIDTH), row),
                  pl.BlockSpec((tm, d), row),
                  pl.BlockSpec((SSM_WIDTH, SSM_WIDTH), const2),
                  pl.BlockSpec((1, SSM_WIDTH), const2),
                  pl.BlockSpec((d, d), const2),
                  pl.BlockSpec((1, 1, d), per_b),
                  pl.BlockSpec((1, d), const2),
                  pl.BlockSpec((1, 1, d), per_b),
                  pl.BlockSpec((1, 1, d), per_b)],
        out_specs=[pl.BlockSpec((tm, d), row), pl.BlockSpec((tm, d), row)],
        out_shape=[jax.ShapeDtypeStruct((n, d), F32), jax.ShapeDtypeStruct((n, d), BF16)],
        compiler_params=_cparams("parallel"),
        name="out_proj_norm2",
    )(ys, yb, yc, x2, w_glu, b_glu, w_out, g1, n2g, sc2, sh2)


def _sort16_network():
    def merge(lo, hi, r):
        step = r * 2
        if step < hi - lo:
            yield from merge(lo, hi, step)
            yield from merge(lo + r, hi, step)
            yield from [(i, i + r) for i in range(lo + r, hi - r, step)]
        else:
            yield (lo, lo + r)

    def sort(lo, hi):
        if hi - lo >= 1:
            mid = lo + (hi - lo) // 2
            yield from sort(lo, mid)
            yield from sort(mid + 1, hi)
            yield from merge(lo, hi, 1)

    return tuple(sort(0, 15))


def _extract_top(x, n_top):
    tops = []
    for _ in range(n_top):
        mx = jnp.max(x, axis=0, keepdims=True)
        tops.append(mx)
        x = jnp.where(x == mx, -jnp.inf, x)
    return tops


def _extract_top_128(x, n_top):
    sub = x.shape[0] // 16
    rows = [x[i * sub:(i + 1) * sub, :] for i in range(16)]
    for a, b in _sort16_network():
        rows[a], rows[b] = jnp.maximum(rows[a], rows[b]), jnp.minimum(rows[a], rows[b])
    tops = []
    for k in range(n_top):
        mx = jnp.max(rows[0], axis=0, keepdims=True)
        tops.append(mx)
        need = min(16, n_top - 1 - k)
        if need:
            hit = rows[0] == mx
            below = rows[1:need + 1] + ([jnp.full_like(rows[0], -jnp.inf)] if need == 16 else [])
            for d in range(need):
                rows[d] = jnp.where(hit, below[d], rows[d])
    return tops


def _peer_kernel(h2_ref, x_ref, g2_ref, fg_ref, wq_ref, kbd_ref, *rest, final_norm, n_split):
    u_refs, vt_refs = rest[:n_split], rest[n_split:2 * n_split]
    o_ref, st_ref, e1_ref, e2_ref, r2_ref, acc_ref, at_ref, ga_ref, v1_ref, v2_ref = rest[2 * n_split:]
    j = pl.program_id(1)
    tm = h2_ref.shape[0]
    u_rows = u_refs[0].shape[0]
    v_rows = vt_refs[0].shape[0]
    te = u_rows * n_split
    n1 = te // PEER_N_KEYS
    nk = PEER_N_KEYS
    h2 = h2_ref[...]

    @pl.when(j == 0)
    def _():
        q = _dot(h2, wq_ref[...]).astype(BF16)
        st_ref[...] = _dot_nt(kbd_ref[...], q)

        def per_head(h, carry):
            base = pl.multiple_of(h * 2 * nk, 2 * nk)
            for lc in range(tm // LANES):
                ls = slice(lc * LANES, (lc + 1) * LANES)
                t1 = _extract_top_128(st_ref[pl.ds(base, nk), ls], PEER_TOPK + 1)
                t2 = _extract_top_128(st_ref[pl.ds(base + nk, nk), ls], PEER_TOPK + 1)
                for k in range(PEER_TOPK):
                    v1_ref[k:k + 1, :] = t1[k]
                    v2_ref[k:k + 1, :] = t2[k]
                v2_all = v2_ref[0:PEER_TOPK, :]
                v2_top = v2_ref[0:8, :]
                cand = [t1[0] + v2_all]
                cand += [t1[a] + v2_top for a in range(1, 8)]
                cand.append(v1_ref[8:PEER_TOPK, :] + t2[0])
                top = _extract_top(jnp.concatenate(cand, axis=0), PEER_TOPK + 1)
                z = jnp.zeros_like(top[0])
                for k in range(PEER_TOPK):
                    z = z + jnp.exp(top[k] - top[0])
                nxt = jnp.maximum(top[PEER_TOPK], jnp.maximum(t1[PEER_TOPK] + t2[0], t1[0] + t2[PEER_TOPK]))
                thr = 0.5 * (top[PEER_TOPK - 1] + nxt)
                s1 = st_ref[pl.ds(base, nk), ls]
                s2 = st_ref[pl.ds(base + nk, nk), ls]
                e1_ref[h, :, ls] = jnp.exp(s1 - t1[0]) * (0.5 / z)
                e2_ref[h, :, ls] = jnp.exp(s2 - t2[0]).astype(BF16)
                rank2 = jnp.full((nk, LANES), float(PEER_TOPK), F32)
                for b in range(PEER_TOPK - 1, -1, -1):
                    rank2 = jnp.where(s2 >= t2[b], float(b), rank2)
                r2_ref[h, :, ls] = rank2.astype(BF16)
                n_sel = jnp.zeros((nk, LANES), F32)
                for a in range(PEER_TOPK):
                    n_a = jnp.zeros_like(thr)
                    for b in range(PEER_TOPK // (a + 1)):
                        n_a = n_a + jnp.where(t1[a] + t2[b] >= thr, 1.0, 0.0)
                    n_sel = jnp.where(s1 == t1[a], n_a, n_sel)
                st_ref[pl.ds(base, nk), ls] = n_sel
            return carry

        lax.fori_loop(0, PEER_HEADS, per_head, 0)
        acc_ref[...] = jnp.zeros_like(acc_ref)
        ga_ref[1] = jnp.zeros(ga_ref.shape[1:], BF16)
        for k in range(n_split):
            at_ref[0, k * u_rows:(k + 1) * u_rows, :] = _dot_nt(u_refs[k][...], h2)

    n_blocks = pl.num_programs(1) - 2
    rd = j % 2
    wr = 1 - rd

    @pl.when((j >= 1) & (j <= n_blocks))
    def _():
        for k in range(n_split):
            at_ref[rd, k * u_rows:(k + 1) * u_rows, :] = _dot_nt(u_refs[k][...], h2)
        for k in range(n_split):
            acc_ref[k * v_rows:(k + 1) * v_rows, :] += _dot(vt_refs[k][...], ga_ref[rd])
        r0 = pl.multiple_of((j - 1) * n1, n1)
        for lc in range(tm // LANES):
            ls = slice(lc * LANES, (lc + 1) * LANES)
            for il in range(n1):
                g = jnp.zeros((nk, LANES), BF16)
                for h in range(PEER_HEADS):
                    n_row = st_ref[pl.ds(h * 2 * nk + r0, n1), ls][il:il + 1, :].astype(BF16)
                    e_row = e1_ref[h, pl.ds(r0, n1), ls][il:il + 1, :].astype(BF16)
                    picked = jnp.where(r2_ref[h, :, ls] < n_row, e2_ref[h, :, ls], jnp.zeros((), BF16))
                    g = g + picked * e_row
                a = at_ref[wr, il * nk:(il + 1) * nk, ls]
                inner = a * (0.7978845608028654 + 0.035677408136300125 * (a * a))
                act2 = a + a * jnp.tanh(inner)
                ga_ref[wr, il * nk:(il + 1) * nk, ls] = g * act2.astype(BF16)

    @pl.when(j == n_blocks + 1)
    def _():
        for k in range(n_split):
            acc_ref[k * v_rows:(k + 1) * v_rows, :] += _dot(vt_refs[k][...], ga_ref[rd])
        out = x_ref[...] + g2_ref[0] * acc_ref[...].T
        if final_norm:
            ms = jnp.mean(out * out, axis=-1, keepdims=True)
            out = out * lax.rsqrt(ms + RMS_EPS) * fg_ref[...]
        o_ref[...] = out


def _peer(h2, x2, seq, g2, fg, wq, kbd, u_bf, vt_bf, final_norm, tm=512, te=1024, n_split=4):
    n, d = x2.shape
    n_blk = u_bf.shape[0] // te
    tpb = seq // tm
    trow = lambda i, j: (i, 0)
    c2 = lambda i, j: (0, 0)
    hs = (PEER_HEADS, PEER_N_KEYS, tm)
    u_map = lambda k, i, j: (jnp.minimum(j, n_blk - 1) * n_split + k, 0)
    v_map = lambda k, i, j: (k, jnp.clip(j - 2, 0, n_blk - 1))
    return pl.pallas_call(
        functools.partial(_peer_kernel, final_norm=final_norm, n_split=n_split),
        grid=(n // tm, n_blk + 2),
        in_specs=[pl.BlockSpec((tm, d), trow),
                  pl.BlockSpec((tm, d), trow),
                  pl.BlockSpec((1, 1, d), lambda i, j: (i // tpb, 0, 0)),
                  pl.BlockSpec((1, d), c2),
                  pl.BlockSpec((d, PEER_HEADS * 2 * PEER_HALF), c2),
                  pl.BlockSpec((PEER_HEADS * 2 * PEER_N_KEYS, PEER_HEADS * 2 * PEER_HALF), c2)]
        + [pl.BlockSpec((te // n_split, d), functools.partial(u_map, k)) for k in range(n_split)]
        + [pl.BlockSpec((d // n_split, te), functools.partial(v_map, k)) for k in range(n_split)],
        out_specs=pl.BlockSpec((tm, d), trow),
        out_shape=jax.ShapeDtypeStruct((n, d), F32),
        scratch_shapes=[pltpu.VMEM((PEER_HEADS * 2 * PEER_N_KEYS, tm), F32),
                        pltpu.VMEM(hs, F32), pltpu.VMEM(hs, BF16), pltpu.VMEM(hs, BF16),
                        pltpu.VMEM((d, tm), F32),
                        pltpu.VMEM((2, te, tm), F32),
                        pltpu.VMEM((2, te, tm), BF16),
                        pltpu.VMEM((PEER_TOPK, LANES), F32), pltpu.VMEM((PEER_TOPK, LANES), F32)],
        compiler_params=_cparams("parallel", "arbitrary"),
        name="peer_dense",
    )(h2, x2, g2, fg, wq, kbd, *([u_bf] * n_split), *([vt_bf] * n_split))


def _peer_key_matrix(k1, k2):
    blocks = []
    for h in range(PEER_HEADS):
        for half, kk in enumerate((k1, k2)):
            col = (2 * h + half) * PEER_HALF
            blocks.append(jnp.pad(kk, ((0, 0), (col, PEER_HEADS * 2 * PEER_HALF - col - PEER_HALF))))
    return jnp.concatenate(blocks, axis=0).astype(BF16)


def kernel(x, c, norm1_g, norm2_g, w_mod, b_mod, w_in, ssm_a_re_log, ssm_a_im, ssm_b_re, ssm_b_im, ssm_c_re, ssm_c_im, ssm_d, ssm_log_dt, ssm_w_glu, ssm_b_glu, gmlp_w_sp, gmlp_b_sp, kv_norm_g, w_uk, w_uv, w_out, peer_w_q, peer_k1, peer_k2, peer_u, peer_v, final_g):
    bsz, seq, d = x.shape
    depth = w_mod.shape[0]
    x2 = x.reshape(bsz * seq, d)
    head_of = jnp.arange(GMLP_WIDTH) // GMLP_HEAD_DIM
    pavg = ((head_of[:, None] == head_of[None, :]).astype(F32) / GMLP_HEAD_DIM).astype(BF16)
    chunk_of = jnp.arange(GMLP_BLOCK) // CHUNK
    sp_mask = chunk_of[:, None] >= chunk_of[None, :]
    for l in range(depth):
        mod = _modulation(c, w_mod[l], b_mod[l])
        sh1, sc1, g1, sh2, sc2, g2 = [mod[:, i * d:(i + 1) * d].reshape(bsz, 1, d) for i in range(6)]
        w_in_pad = jnp.pad(w_in[l], ((0, 0), (0, IN_PAD - IN_WIDTH))).astype(BF16)
        wsp = jnp.where(sp_mask[None], gmlp_w_sp[l], 0.0).astype(BF16)
        bsp = jnp.repeat(gmlp_b_sp[l].T, GMLP_HEAD_DIM, axis=1)
        ua, yb, q, ckv, qi, ki, wi, ckvt = _input_stage(
            x2, seq, norm1_g[l].reshape(1, d), sc1, sh1, w_in_pad, kv_norm_g[l].reshape(1, KV_RANK),
            pavg, wsp, bsp)
        prep = _ssm_prep_tm(ssm_a_re_log[l], ssm_a_im[l], ssm_b_re[l], ssm_b_im[l], ssm_c_re[l], ssm_c_im[l],
                            ssm_log_dt[l], SSM_T)
        ys = _ssm_scan_tm(ua, bsz, seq, prep, ssm_d[l].reshape(1, SSM_WIDTH))
        yc = _dsa(q, qi, wi, ki, ckv, ckvt, w_uk[l], w_uv[l], bsz, seq)
        x2, h2 = _output_stage(
            ys, yb, yc, x2, seq, ssm_w_glu[l].astype(BF16),
            ssm_b_glu[l].reshape(1, SSM_WIDTH), w_out[l].astype(BF16), g1, norm2_g[l].reshape(1, d), sc2, sh2)
        x2 = _peer(h2, x2, seq, g2, final_g.reshape(1, d), peer_w_q[l].astype(BF16),
                   _peer_key_matrix(peer_k1[l], peer_k2[l]), peer_u[l].astype(BF16),
                   peer_v[l].T.astype(BF16), final_norm=(l == depth - 1))
    return x2.reshape(bsz, seq, d)
```
